```python
import jax, jax.numpy as jnp
from jax import lax
import numpy as np

D_MODEL = 1024
BATCH = 16
SEQ = 4096
DEPTH = 1
DEC_BATCH = 16
DEC_SEQ = 32
PAST_LEN = 1024

CHUNK = 64
EPS = 1e-6
POOL_WIDTH = D_MODEL // 2
POOL_WINDOWS = (2, 4, 8, 16)
POOL_GROUPS = len(POOL_WINDOWS)
POOL_GROUP_DIM = POOL_WIDTH // POOL_GROUPS
POOL_HIST = max(POOL_WINDOWS) - 1
RET_HEADS = 4
RET_DK = D_MODEL // 8
RET_DV = D_MODEL // 4
RET_QK_WIDTH = RET_HEADS * RET_DK
RET_V_WIDTH = RET_HEADS * RET_DV
ROPE_BASE = 10000.0
MEM_LEN = 256
MEM_HEADS = 4
MEM_HEAD_DIM = D_MODEL // MEM_HEADS
N_EXPERTS = 32
TOP_K = 4
D_EXPERT = D_MODEL
SWIGLU_LIMIT = 7.0
SWIGLU_ALPHA = 1.702
MOE_BLOCK = 256
IN_SIZES = (POOL_WIDTH, RET_QK_WIDTH, RET_QK_WIDTH, RET_V_WIDTH, RET_V_WIDTH, D_MODEL, D_MODEL)
IN_WIDTH = sum(IN_SIZES)

kernel_name = "hybrid_pool_retention_moe_stream_step"


def rmsnorm(x, w):
    x32 = x.astype(jnp.float32)
    y = x32 * lax.rsqrt(jnp.mean(x32 * x32, axis=-1, keepdims=True) + EPS)
    return (y * w.astype(jnp.float32)).astype(x.dtype)


def split_columns(z):
    outs, off = [], 0
    for s in IN_SIZES:
        outs.append(z[..., off:off + s])
        off += s
    return outs


def rotary(x, pos):
    half = RET_DK // 2
    inv = 1.0 / (ROPE_BASE ** (jnp.arange(half, dtype=jnp.float32) / half))
    ang = pos.astype(jnp.float32)[:, None] * inv[None, :]
    cos = jnp.cos(ang)[None, :, None, :]
    sin = jnp.sin(ang)[None, :, None, :]
    x1, x2 = x[..., :half], x[..., half:]
    return jnp.concatenate([x1 * cos - x2 * sin, x1 * sin + x2 * cos], axis=-1)


def ret_log_gamma():
    return jnp.log(1.0 - 2.0 ** (-5.0 - jnp.arange(RET_HEADS, dtype=jnp.float32)))


def retention_block(S, q, k, v):
    L = q.shape[1]
    lg = ret_log_gamma()
    idx = jnp.arange(L)
    rel = idx[:, None] - idx[None, :]
    decay = jnp.where(rel[None] >= 0, jnp.exp(jnp.maximum(rel, 0)[None].astype(jnp.float32) * lg[:, None, None]), 0.0)
    scores = jnp.einsum('bihd,bjhd->bhij', q, k) * decay[None]
    o_intra = jnp.einsum('bhij,bjhe->bihe', scores, v)
    q_decay = jnp.exp((idx + 1).astype(jnp.float32)[:, None] * lg[None, :])
    o_cross = jnp.einsum('bihd,bhde->bihe', q, S) * q_decay[None, :, :, None]
    k_decay = jnp.exp((L - 1 - idx).astype(jnp.float32)[:, None] * lg[None, :])
    S_new = jnp.exp(L * lg)[None, :, None, None] * S + jnp.einsum('bjhd,bjhe->bhde', k * k_decay[None, :, :, None], v)
    return S_new, o_intra + o_cross


def retention(q, k, v, S0):
    B, L = q.shape[0], q.shape[1]
    if L <= CHUNK:
        return retention_block(S0, q, k, v)
    n = L // CHUNK

    def to_chunks(a):
        return a.reshape(B, n, CHUNK, *a.shape[2:]).swapaxes(0, 1)

    def step(S, blk):
        return retention_block(S, *blk)

    S_fin, o = lax.scan(step, S0, (to_chunks(q), to_chunks(k), to_chunks(v)))
    return S_fin, o.swapaxes(0, 1).reshape(B, L, RET_HEADS, RET_DV)


def pool_mix(u, hist, n_hist_valid, pool_w_grp, pool_scale):
    B, L, P = u.shape
    full = jnp.concatenate([hist.astype(u.dtype), u], axis=1)
    full32 = full.astype(jnp.float32)
    cs = jnp.concatenate([jnp.zeros((B, 1, P), jnp.float32), jnp.cumsum(full32, axis=1)], axis=1)
    end = cs[:, POOL_HIST + 1:]
    t = jnp.arange(L)
    means = []
    for g, w in enumerate(POOL_WINDOWS):
        sl = slice(g * POOL_GROUP_DIM, (g + 1) * POOL_GROUP_DIM)
        start = cs[:, POOL_HIST + 1 - w:POOL_HIST + 1 - w + L, sl]
        cnt = jnp.minimum(w, t + 1 + n_hist_valid).astype(jnp.float32)
        means.append((end[..., sl] - start) / cnt[None, :, None])
    d = (jnp.concatenate(means, axis=-1) - u.astype(jnp.float32)).astype(u.dtype)
    d = d.reshape(B, L, POOL_GROUPS, POOL_GROUP_DIM)
    y = jnp.einsum('blgc,gcd->blgd', d, pool_w_grp).reshape(B, L, P) * pool_scale
    return y, full[:, -POOL_HIST:]


def mixer_block(xn, pool_hist, n_hist_valid, ret_state, pos0, w_in, pool_w_grp, pool_scale, w_pool_out, ret_gn_w, w_ret_out, w_o):
    B, L, _ = xn.shape
    dt = xn.dtype
    z = xn @ w_in
    u_pool, q, k, v, g, a_pool, a_ret = split_columns(z)
    yp, new_hist = pool_mix(u_pool, pool_hist, n_hist_valid, pool_w_grp, pool_scale)
    branch_pool = yp @ w_pool_out
    pos = pos0 + jnp.arange(L)
    q = rotary(q.astype(jnp.float32).reshape(B, L, RET_HEADS, RET_DK), pos)
    k = rotary(k.astype(jnp.float32).reshape(B, L, RET_HEADS, RET_DK), pos) * (RET_DK ** -0.5)
    v = v.astype(jnp.float32).reshape(B, L, RET_HEADS, RET_DV)
    S_new, o = retention(q, k, v, ret_state.astype(jnp.float32))
    mu = jnp.mean(o, axis=-1, keepdims=True)
    var = jnp.mean((o - mu) ** 2, axis=-1, keepdims=True)
    on = ((o - mu) * lax.rsqrt(var + EPS)).reshape(B, L, RET_V_WIDTH) * ret_gn_w.astype(jnp.float32)
    r = (on * jax.nn.silu(g.astype(jnp.float32))).astype(dt)
    branch_ret = r @ w_ret_out
    merged = jax.nn.sigmoid(a_pool) * branch_pool + jax.nn.sigmoid(a_ret) * branch_ret
    return merged @ w_o, new_hist, S_new


def mem_kv(mem, norm_mem_w, w_mk, w_mv):
    B, M, _ = mem.shape
    mn = rmsnorm(mem, norm_mem_w)
    return ((mn @ w_mk).reshape(B, M, MEM_HEADS, MEM_HEAD_DIM), (mn @ w_mv).reshape(B, M, MEM_HEADS, MEM_HEAD_DIM))


def cross_attn(hn, mk, mv, w_cq, w_co):
    B, L, _ = hn.shape
    q = (hn @ w_cq).reshape(B, L, MEM_HEADS, MEM_HEAD_DIM)
    s = jnp.einsum('blhd,bmhd->bhlm', q, mk.astype(q.dtype)).astype(jnp.float32) * (MEM_HEAD_DIM ** -0.5)
    p = jax.nn.softmax(s, axis=-1).astype(hn.dtype)
    o = jnp.einsum('bhlm,bmhd->blhd', p, mv.astype(hn.dtype)).reshape(B, L, D_MODEL)
    return o @ w_co


def moe(xn, w_router, b_router, w_gu, b_gu, w_down, b_down):
    B, L, D = xn.shape
    T = B * L
    TK = T * TOP_K
    x2 = xn.reshape(T, D)
    logits = (x2 @ w_router).astype(jnp.float32) + b_router.astype(jnp.float32)
    top_v, top_i = lax.top_k(logits, TOP_K)
    gate = jax.nn.softmax(top_v, axis=-1)
    flat_e = top_i.reshape(TK)
    order = jnp.argsort(flat_e)
    sorted_e = flat_e[order]
    src_tok = (order // TOP_K).astype(jnp.int32)
    src_w = gate.reshape(TK)[order]
    counts = jnp.bincount(flat_e, length=N_EXPERTS)
    padded = (counts + MOE_BLOCK - 1) // MOE_BLOCK * MOE_BLOCK
    start = jnp.cumsum(counts) - counts
    pstart = jnp.cumsum(padded) - padded
    dest = jnp.arange(TK) - start[sorted_e] + pstart[sorted_e]
    n_blocks = -(-(TK + N_EXPERTS * (MOE_BLOCK - 1)) // MOE_BLOCK)
    rows = n_blocks * MOE_BLOCK
    row_tok = jnp.zeros((rows,), jnp.int32).at[dest].set(src_tok)
    row_w = jnp.zeros((rows,), jnp.float32).at[dest].set(src_w)
    block_e = jnp.minimum(jnp.searchsorted(jnp.cumsum(padded), jnp.arange(n_blocks) * MOE_BLOCK, side='right'), N_EXPERTS - 1)

    def expert_block(args):
        tok, e = args
        xb = x2[tok]
        gu = (xb @ w_gu[e] + b_gu[e]).astype(jnp.float32)
        gl = jnp.minimum(gu[:, :D_EXPERT], SWIGLU_LIMIT)
        up = jnp.clip(gu[:, D_EXPERT:], -SWIGLU_LIMIT, SWIGLU_LIMIT)
        h = (gl * jax.nn.sigmoid(SWIGLU_ALPHA * gl) * (up + 1.0)).astype(xn.dtype)
        return h @ w_down[e] + b_down[e]

    yb = lax.map(expert_block, (row_tok.reshape(n_blocks, MOE_BLOCK), block_e))
    y = jnp.zeros((T, D), jnp.float32).at[row_tok].add(yb.reshape(rows, D).astype(jnp.float32) * row_w[:, None])
    return y.astype(xn.dtype).reshape(B, L, D)


def trunk_layer(x, pool_hist, n_hist_valid, ret_state, pos0, mk, mv, norm_mix_w, w_in, pool_w_grp, pool_scale, w_pool_out, ret_gn_w, w_ret_out, w_o, norm_cross_w, w_cq, w_co, norm_ffn_w, w_router, b_router, w_gu, b_gu, w_down, b_down):
    mix, new_hist, S_new = mixer_block(rmsnorm(x, norm_mix_w), pool_hist, n_hist_valid, ret_state, pos0, w_in, pool_w_grp, pool_scale, w_pool_out, ret_gn_w, w_ret_out, w_o)
    h = x + mix
    h = h + cross_attn(rmsnorm(h, norm_cross_w), mk, mv, w_cq, w_co)
    h = h + moe(rmsnorm(h, norm_ffn_w), w_router, b_router, w_gu, b_gu, w_down, b_down)
    return h, new_hist, S_new


def setup_inputs(seed: int = 0) -> dict:
    key = jax.random.key(seed)
    ks = jax.random.split(key, 32)
    f32 = jnp.float32

    def nrm(k, shape, scale):
        return jax.random.normal(k, shape, f32) * scale

    def gain(k, shape):
        return 1.0 + 0.02 * jax.random.normal(k, shape, f32)

    return {
        "x_prompt": nrm(ks[0], (BATCH, SEQ, D_MODEL), 1.0),
        "x_sample": nrm(ks[1], (DEC_BATCH, DEC_SEQ, D_MODEL), 1.0),
        "mem_prompt": nrm(ks[2], (BATCH, MEM_LEN, D_MODEL), 1.0),
        "state_pool": nrm(ks[3], (DEPTH, DEC_BATCH, POOL_HIST, POOL_WIDTH), 1.0),
        "state_ret": nrm(ks[4], (DEPTH, DEC_BATCH, RET_HEADS, RET_DK, RET_DV), 0.5),
        "cache_mem_k": nrm(ks[5], (DEPTH, DEC_BATCH, MEM_LEN, MEM_HEADS, MEM_HEAD_DIM), 1.0),
        "cache_mem_v": nrm(ks[6], (DEPTH, DEC_BATCH, MEM_LEN, MEM_HEADS, MEM_HEAD_DIM), 1.0),
        "norm_mix_w": gain(ks[7], (DEPTH, D_MODEL)),
        "w_in": nrm(ks[8], (DEPTH, D_MODEL, IN_WIDTH), D_MODEL ** -0.5),
        "pool_w_grp": nrm(ks[9], (DEPTH, POOL_GROUPS, POOL_GROUP_DIM, POOL_GROUP_DIM), POOL_GROUP_DIM ** -0.5),
        "pool_scale": gain(ks[10], (DEPTH, POOL_WIDTH)),
        "w_pool_out": nrm(ks[11], (DEPTH, POOL_WIDTH, D_MODEL), POOL_WIDTH ** -0.5),
        "ret_gn_w": gain(ks[12], (DEPTH, RET_V_WIDTH)),
        "w_ret_out": nrm(ks[13], (DEPTH, RET_V_WIDTH, D_MODEL), RET_V_WIDTH ** -0.5),
        "w_o": nrm(ks[14], (DEPTH, D_MODEL, D_MODEL), D_MODEL ** -0.5),
        "norm_mem_w": gain(ks[15], (DEPTH, D_MODEL)),
        "w_mk": nrm(ks[16], (DEPTH, D_MODEL, D_MODEL), D_MODEL ** -0.5),
        "w_mv": nrm(ks[17], (DEPTH, D_MODEL, D_MODEL), D_MODEL ** -0.5),
        "norm_cross_w": gain(ks[18], (DEPTH, D_MODEL)),
        "w_cq": nrm(ks[19], (DEPTH, D_MODEL, D_MODEL), D_MODEL ** -0.5),
        "w_co": nrm(ks[20], (DEPTH, D_MODEL, D_MODEL), D_MODEL ** -0.5),
        "norm_ffn_w": gain(ks[21], (DEPTH, D_MODEL)),
        "w_router": nrm(ks[22], (DEPTH, D_MODEL, N_EXPERTS), D_MODEL ** -0.5),
        "b_router": nrm(ks[23], (DEPTH, N_EXPERTS), 0.01),
        "w_gu": nrm(ks[24], (DEPTH, N_EXPERTS, D_MODEL, 2 * D_EXPERT), D_MODEL ** -0.5),
        "b_gu": nrm(ks[25], (DEPTH, N_EXPERTS, 2 * D_EXPERT), 0.01),
        "w_down": nrm(ks[26], (DEPTH, N_EXPERTS, D_EXPERT, D_MODEL), D_EXPERT ** -0.5),
        "b_down": nrm(ks[27], (DEPTH, N_EXPERTS, D_MODEL), 0.01),
        "norm_final_w": gain(ks[28], (D_MODEL,)),
    }


def reference(x_prompt, x_sample, mem_prompt, state_pool, state_ret, cache_mem_k, cache_mem_v, norm_mix_w, w_in, pool_w_grp, pool_scale, w_pool_out, ret_gn_w, w_ret_out, w_o, norm_mem_w, w_mk, w_mv, norm_cross_w, w_cq, w_co, norm_ffn_w, w_router, b_router, w_gu, b_gu, w_down, b_down, norm_final_w):
    hp, hs = x_prompt, x_sample
    Bp = x_prompt.shape[0]
    mem_k_p, mem_v_p, pool_p, ret_p, pool_s, ret_s = [], [], [], [], [], []
    for l in range(DEPTH):
        lw = (norm_mix_w[l], w_in[l], pool_w_grp[l], pool_scale[l], w_pool_out[l], ret_gn_w[l], w_ret_out[l], w_o[l],
              norm_cross_w[l], w_cq[l], w_co[l], norm_ffn_w[l], w_router[l], b_router[l], w_gu[l], b_gu[l], w_down[l], b_down[l])
        mk, mv = mem_kv(mem_prompt, norm_mem_w[l], w_mk[l], w_mv[l])
        zero_hist = jnp.zeros((Bp, POOL_HIST, POOL_WIDTH), hp.dtype)
        zero_state = jnp.zeros((Bp, RET_HEADS, RET_DK, RET_DV), jnp.float32)
        hp, hist_p, S_p = trunk_layer(hp, zero_hist, 0, zero_state, 0, mk, mv, *lw)
        mem_k_p.append(mk)
        mem_v_p.append(mv)
        pool_p.append(hist_p)
        ret_p.append(S_p)
        hs, hist_s, S_s = trunk_layer(hs, state_pool[l], POOL_HIST, state_ret[l], PAST_LEN, cache_mem_k[l], cache_mem_v[l], *lw)
        pool_s.append(hist_s)
        ret_s.append(S_s)
    y_prompt = rmsnorm(hp, norm_final_w)
    y_sample = rmsnorm(hs, norm_final_w)
    return (y_prompt, y_sample, jnp.stack(mem_k_p), jnp.stack(mem_v_p), jnp.stack(pool_p), jnp.stack(ret_p), jnp.stack(pool_s), jnp.stack(ret_s))
```

```python
import functools

import jax
import jax.numpy as jnp
from jax import lax
from jax.experimental import pallas as pl
from jax.experimental.pallas import tpu as pltpu

F32 = jnp.float32
BF16 = jnp.bfloat16

EPS = 1e-6
POOL_WINDOWS = (2, 4, 8, 16)
POOL_GROUP_DIM = 128
POOL_WIDTH = POOL_GROUP_DIM * len(POOL_WINDOWS)
POOL_HIST = max(POOL_WINDOWS) - 1
RET_HEADS = 4
RET_DK = 128
RET_DV = 256
ROPE_BASE = 10000.0
MEM_HEADS = 4
TOP_K = 4
SWIGLU_LIMIT = 7.0
SWIGLU_ALPHA = 1.702

SUBLANES = 8
LANES = 128
VMEM_LIMIT_BYTES = 56 * 1024 * 1024

MIXER_TILE = 256
RET_CHUNK = 128
CROSS_TILE = 512
DISPATCH_TILE = 256
EXPERT_BLOCK = 512
COMBINE_TILE = 256


def _cparams(n_axes):
    return pltpu.CompilerParams(dimension_semantics=("arbitrary",) * n_axes,
                                vmem_limit_bytes=VMEM_LIMIT_BYTES)


def _const_spec(shape):
    nd = len(shape)
    return pl.BlockSpec(shape, lambda *_: (0,) * nd, pipeline_mode=pl.Buffered(1))


def _rms(x32, w_row):
    ms = jnp.mean(x32 * x32, axis=-1, keepdims=True)
    return x32 * lax.rsqrt(ms + EPS) * w_row


def _dot(a, b):
    return jnp.dot(a, b, preferred_element_type=F32)


def _dot_nt(a, b):
    return lax.dot_general(a, b, (((1,), (1,)), ((), ())), preferred_element_type=F32)


def _dot_tn(a, b):
    return lax.dot_general(a, b, (((0,), (0,)), ((), ())), preferred_element_type=F32)


def _rope_kernel(inv_ref, sign_ref, cos_ref, sin_ref, *, pos0, tile):
    i = pl.program_id(0)
    pos = (lax.broadcasted_iota(jnp.int32, (tile, RET_DK), 0) + (pos0 + i * tile)).astype(F32)
    ang = pos * inv_ref[...]
    cos_ref[...] = jnp.cos(ang)
    sin_ref[...] = jnp.sin(ang) * sign_ref[...]


def _rope_tables(length, pos0):
    half = RET_DK // 2
    inv = 1.0 / (ROPE_BASE ** (jnp.arange(half, dtype=F32) / half))
    inv2 = jnp.concatenate([inv, inv])[None, :]
    sign = jnp.concatenate([-jnp.ones((half,), F32), jnp.ones((half,), F32)])[None, :]
    tile = min(length, 512)
    assert length % tile == 0
    return pl.pallas_call(
        functools.partial(_rope_kernel, pos0=pos0, tile=tile),
        out_shape=(jax.ShapeDtypeStruct((length, RET_DK), F32),) * 2,
        grid=(length // tile,),
        in_specs=[pl.BlockSpec((1, RET_DK), lambda i: (0, 0))] * 2,
        out_specs=(pl.BlockSpec((tile, RET_DK), lambda i: (i, 0)),) * 2,
        compiler_params=_cparams(1),
        name="rope_tables",
    )(inv2, sign)


def _mixer_kernel(x_ref, hist_ref, s0_ref, cos_ref, sin_ref, nmw_ref, win_ref, pwg_ref, pscale_ref,
                  wpo_ref, gnw_ref, wro_ref, wo_ref, dec_ref, qd_ref, kd_ref, gc_ref,
                  h_ref, hist_out_ref, s_out_ref,
                  ubuf, s_scr, z_scr, r_scr, *, tile, chunk, n_hist_valid, d_model):
    t = pl.program_id(1)
    n_t = pl.num_programs(1)
    q_off = POOL_WIDTH
    k_off = q_off + RET_HEADS * RET_DK
    v_off = k_off + RET_HEADS * RET_DK
    g_off = v_off + RET_HEADS * RET_DV
    ap_off = g_off + RET_HEADS * RET_DV
    ar_off = ap_off + d_model

    @pl.when(t == 0)
    def _():
        ubuf[0:1, :] = jnp.zeros((1, POOL_WIDTH), F32)
        ubuf[1:1 + POOL_HIST, :] = hist_ref[0]
        s_scr[...] = s0_ref[0]

    x = x_ref[0]
    xn = _rms(x, nmw_ref[...]).astype(BF16)
    z_scr[...] = _dot(xn, win_ref[...])

    hist_rows = POOL_HIST + 1
    ubuf[hist_rows:hist_rows + tile, :] = z_scr[:, 0:POOL_WIDTH]
    full = ubuf[...]
    pos = t * tile + lax.broadcasted_iota(jnp.int32, (tile, 1), 0)
    ys = []
    for g, w in enumerate(POOL_WINDOWS):
        f = full[:, g * POOL_GROUP_DIM:(g + 1) * POOL_GROUP_DIM]
        s = f
        shift = 1
        while shift < w:
            s = s + pltpu.roll(s, shift, 0)
            shift *= 2
        cnt = jnp.minimum(w, pos + 1 + n_hist_valid).astype(F32)
        d = s[hist_rows:, :] / cnt - f[hist_rows:, :]
        ys.append(_dot(d.astype(BF16), pwg_ref[g]))
    yp = jnp.concatenate(ys, axis=-1) * pscale_ref[...]
    branch_pool = _dot(yp.astype(BF16), wpo_ref[...])
    new_hist = full[tile + 1:tile + 1 + POOL_HIST, :]
    ubuf[1:1 + POOL_HIST, :] = new_hist

    @pl.when(t == n_t - 1)
    def _():
        hist_out_ref[0] = new_hist

    for c in range(tile // chunk):
        rows = slice(c * chunk, (c + 1) * chunk)
        cosc = cos_ref[rows, :]
        sinc = sin_ref[rows, :]
        for h in range(RET_HEADS):
            qh = z_scr[rows, q_off + h * RET_DK:q_off + (h + 1) * RET_DK]
            kh = z_scr[rows, k_off + h * RET_DK:k_off + (h + 1) * RET_DK]
            vb = z_scr[rows, v_off + h * RET_DV:v_off + (h + 1) * RET_DV].astype(BF16)
            qr = qh * cosc + pltpu.roll(qh, RET_DK // 2, 1) * sinc
            kr = (kh * cosc + pltpu.roll(kh, RET_DK // 2, 1) * sinc) * (RET_DK ** -0.5)
            qb = qr.astype(BF16)
            kb = kr.astype(BF16)
            scores = _dot_nt(qb, kb) * dec_ref[h]
            state = s_scr[h]
            o = _dot(scores.astype(BF16), vb) + _dot(qb, state.astype(BF16)) * qd_ref[h]
            s_scr[h] = gc_ref[h] * state + _dot_tn((kr * kd_ref[h]).astype(BF16), vb)
            mu = jnp.mean(o, axis=-1, keepdims=True)
            dlt = o - mu
            var = jnp.mean(dlt * dlt, axis=-1, keepdims=True)
            on = dlt * lax.rsqrt(var + EPS) * gnw_ref[:, h * RET_DV:(h + 1) * RET_DV]
            gate = z_scr[rows, g_off + h * RET_DV:g_off + (h + 1) * RET_DV]
            r_scr[rows, h * RET_DV:(h + 1) * RET_DV] = (on * (gate * jax.nn.sigmoid(gate))).astype(BF16)

    @pl.when(t == n_t - 1)
    def _():
        s_out_ref[0] = s_scr[...]

    branch_ret = _dot(r_scr[...], wro_ref[...])
    merged = (jax.nn.sigmoid(z_scr[:, ap_off:ap_off + d_model]) * branch_pool
              + jax.nn.sigmoid(z_scr[:, ar_off:ar_off + d_model]) * branch_ret)
    h_ref[0] = x + _dot(merged.astype(BF16), wo_ref[...])


def _ret_tables(chunk):
    lg = jnp.log(1.0 - 2.0 ** (-5.0 - jnp.arange(RET_HEADS, dtype=F32)))
    idx = jnp.arange(chunk)
    rel = idx[:, None] - idx[None, :]
    decay = jnp.where(rel[None] >= 0, jnp.exp(jnp.maximum(rel, 0)[None].astype(F32) * lg[:, None, None]), 0.0)
    q_decay = jnp.exp((idx + 1).astype(F32)[None, :] * lg[:, None])[:, :, None]
    k_decay = jnp.exp((chunk - 1 - idx).astype(F32)[None, :] * lg[:, None])[:, :, None]
    g_chunk = jnp.exp(chunk * lg)
    return decay, q_decay, k_decay, g_chunk


def _mixer(x, hist, s0, pos0, n_hist_valid, w):
    batch, length, d_model = x.shape
    tile = min(MIXER_TILE, length)
    chunk = min(RET_CHUNK, tile)
    assert length % tile == 0 and tile % chunk == 0 and tile >= POOL_HIST + 1
    in_width = w["w_in"].shape[1]
    cos, sin = _rope_tables(length, pos0)
    decay, q_decay, k_decay, g_chunk = _ret_tables(chunk)
    kern = functools.partial(_mixer_kernel, tile=tile, chunk=chunk, n_hist_valid=n_hist_valid, d_model=d_model)
    return pl.pallas_call(
        kern,
        out_shape=(jax.ShapeDtypeStruct(x.shape, F32),
                   jax.ShapeDtypeStruct(hist.shape, F32),
                   jax.ShapeDtypeStruct(s0.shape, F32)),
        grid=(batch, length // tile),
        in_specs=[
            pl.BlockSpec((1, tile, d_model), lambda b, t: (b, t, 0)),
            pl.BlockSpec((1, POOL_HIST, POOL_WIDTH), lambda b, t: (b, 0, 0)),
            pl.BlockSpec((1, RET_HEADS, RET_DK, RET_DV), lambda b, t: (b, 0, 0, 0)),
            pl.BlockSpec((tile, RET_DK), lambda b, t: (t, 0)),
            pl.BlockSpec((tile, RET_DK), lambda b, t: (t, 0)),
            _const_spec((1, d_model)),
            _const_spec((d_model, in_width)),
            _const_spec(w["pool_w_grp"].shape),
            _const_spec((1, POOL_WIDTH)),
            _const_spec((POOL_WIDTH, d_model)),
            _const_spec((1, RET_HEADS * RET_DV)),
            _const_spec((RET_HEADS * RET_DV, d_model)),
            _const_spec((d_model, d_model)),
            _const_spec(decay.shape),
            _const_spec(q_decay.shape),
            _const_spec(k_decay.shape),
            pl.BlockSpec(memory_space=pltpu.SMEM),
        ],
        out_specs=(
            pl.BlockSpec((1, tile, d_model), lambda b, t: (b, t, 0)),
            pl.BlockSpec((1, POOL_HIST, POOL_WIDTH), lambda b, t: (b, 0, 0)),
            pl.BlockSpec((1, RET_HEADS, RET_DK, RET_DV), lambda b, t: (b, 0, 0, 0)),
        ),
        scratch_shapes=[
            pltpu.VMEM((POOL_HIST + 1 + tile, POOL_WIDTH), F32),
            pltpu.VMEM((RET_HEADS, RET_DK, RET_DV), F32),
            pltpu.VMEM((tile, in_width), F32),
            pltpu.VMEM((tile, RET_HEADS * RET_DV), BF16),
        ],
        compiler_params=_cparams(2),
        name="mixer",
    )(x, hist, s0, cos, sin, w["norm_mix_w"], w["w_in"], w["pool_w_grp"], w["pool_scale"], w["w_pool_out"],
      w["ret_gn_w"], w["w_ret_out"], w["w_o"], decay, q_decay, k_decay, g_chunk)


def _mem_kv_kernel(mem_ref, nw_ref, wk_ref, wv_ref, k_ref, v_ref):
    mn = _rms(mem_ref[0], nw_ref[...]).astype(BF16)
    k_ref[0] = _dot(mn, wk_ref[...])
    v_ref[0] = _dot(mn, wv_ref[...])


def _mem_kv(mem, w):
    batch, m_len, d_model = mem.shape
    out = jax.ShapeDtypeStruct((batch, m_len, d_model), F32)
    blk = pl.BlockSpec((1, m_len, d_model), lambda b: (b, 0, 0))
    return pl.pallas_call(
        _mem_kv_kernel,
        out_shape=(out, out),
        grid=(batch,),
        in_specs=[blk, _const_spec((1, d_model)), _const_spec((d_model, d_model)), _const_spec((d_model, d_model))],
        out_specs=(blk, blk),
        compiler_params=_cparams(1),
        name="mem_kv",
    )(mem, w["norm_mem_w"], w["w_mk"], w["w_mv"])


def _cross_router_kernel(h_ref, mk_ref, mv_ref, ncw_ref, wcq_ref, wco_ref, nfw_ref, wrt_ref, brt_ref,
                         h2_ref, xn_ref, idx_ref, gate_ref, rank_ref, cnt_ref,
                         o_scr, *, tile, head_dim, n_experts):
    h1 = h_ref[0]
    hn = _rms(h1, ncw_ref[...]).astype(BF16)
    q = _dot(hn, wcq_ref[...]).astype(BF16)
    for hd in range(MEM_HEADS):
        cols = slice(hd * head_dim, (hd + 1) * head_dim)
        s = _dot_nt(q[:, cols], mk_ref[0, :, cols]) * (head_dim ** -0.5)
        e = jnp.exp(s - jnp.max(s, axis=-1, keepdims=True))
        p = e / jnp.sum(e, axis=-1, keepdims=True)
        o_scr[:, cols] = _dot(p.astype(BF16), mv_ref[0, :, cols]).astype(BF16)
    h2 = h1 + _dot(o_scr[...], wco_ref[...])
    h2_ref[0] = h2
    xn = _rms(h2, nfw_ref[...])
    xn_ref[0] = xn

    logits = _dot_nt(wrt_ref[...], xn.astype(BF16)) + brt_ref[...]
    e_iota = lax.broadcasted_iota(jnp.int32, (n_experts, tile), 0)
    work = logits
    vals, sels = [], []
    for _ in range(TOP_K):
        m = jnp.max(work, axis=0, keepdims=True)
        first = jnp.min(jnp.where(work == m, e_iota, n_experts), axis=0, keepdims=True)
        sel = e_iota == first
        vals.append(m)
        sels.append(sel)
        work = jnp.where(sel, -jnp.inf, work)
    exps = [jnp.exp(v - vals[0]) for v in vals]
    denom = exps[0] + exps[1] + exps[2] + exps[3]
    assigned = jnp.zeros((n_experts, tile), F32)
    for sel in sels:
        assigned = assigned + sel.astype(F32)
    r_iota = lax.broadcasted_iota(jnp.int32, (tile, tile), 0)
    c_iota = lax.broadcasted_iota(jnp.int32, (tile, tile), 1)
    before = (r_iota < c_iota).astype(BF16)
    prior = _dot(assigned.astype(BF16), before)
    for k in range(TOP_K):
        idx_ref[0, k:k + 1, :] = jnp.sum(jnp.where(sels[k], e_iota, 0), axis=0, keepdims=True)
        gate_ref[0, k:k + 1, :] = exps[k] / denom
        rank_ref[0, k:k + 1, :] = jnp.sum(jnp.where(sels[k], prior, 0.0), axis=0, keepdims=True).astype(jnp.int32)
    cnt_ref[0, 0] = jnp.broadcast_to(jnp.sum(assigned, axis=1, keepdims=True), (n_experts, LANES)).astype(jnp.int32)


def _cross_router(h1, mk, mv, w):
    batch, length, d_model = h1.shape
    m_len = mk.shape[1]
    head_dim = d_model // MEM_HEADS
    n_experts = w["w_router_t"].shape[0]
    tile = min(CROSS_TILE, length)
    assert length % tile == 0
    n_t = length // tile
    kern = functools.partial(_cross_router_kernel, tile=tile, head_dim=head_dim, n_experts=n_experts)
    tok = pl.BlockSpec((1, tile, d_model), lambda b, t: (b, t, 0))
    mem = pl.BlockSpec((1, m_len, d_model), lambda b, t: (b, 0, 0))
    small = pl.BlockSpec((1, TOP_K, tile), lambda b, t: (b, 0, t))
    return pl.pallas_call(
        kern,
        out_shape=(jax.ShapeDtypeStruct(h1.shape, F32),
                   jax.ShapeDtypeStruct(h1.shape, F32),
                   jax.ShapeDtypeStruct((batch, TOP_K, length), jnp.int32),
                   jax.ShapeDtypeStruct((batch, TOP_K, length), F32),
                   jax.ShapeDtypeStruct((batch, TOP_K, length), jnp.int32),
                   jax.ShapeDtypeStruct((batch, n_t, n_experts, LANES), jnp.int32)),
        grid=(batch, n_t),
        in_specs=[tok, mem, mem,
                  _const_spec((1, d_model)), _const_spec((d_model, d_model)), _const_spec((d_model, d_model)),
                  _const_spec((1, d_model)), _const_spec((n_experts, d_model)), _const_spec((n_experts, 1))],
        out_specs=(tok, tok, small, small, small,
                   pl.BlockSpec((1, 1, n_experts, LANES), lambda b, t: (b, t, 0, 0))),
        scratch_shapes=[pltpu.VMEM((tile, d_model), BF16)],
        compiler_params=_cparams(2),
        name="cross_router",
    )(h1, mk, mv, w["norm_cross_w"], w["w_cq"], w["w_co"], w["norm_ffn_w"], w["w_router_t"], w["b_router_c"])


def _positions_kernel(idx_ref, rank_ref, base_ref, pos_ref, *, tile, n_experts):
    e_iota = lax.broadcasted_iota(jnp.int32, (n_experts, tile), 0)
    base = base_ref[0, 0][:, 0:1]
    for k in range(TOP_K):
        sel = e_iota == idx_ref[0, k:k + 1, :]
        pos_ref[0, k:k + 1, :] = jnp.sum(jnp.where(sel, base, 0), axis=0, keepdims=True) + rank_ref[0, k:k + 1, :]


def _positions(idx, rank, base):
    batch, _, length = idx.shape
    n_t, n_experts = base.shape[1], base.shape[2]
    tile = length // n_t
    small = pl.BlockSpec((1, TOP_K, tile), lambda b, t: (b, 0, t))
    return pl.pallas_call(
        functools.partial(_positions_kernel, tile=tile, n_experts=n_experts),
        out_shape=jax.ShapeDtypeStruct(idx.shape, jnp.int32),
        grid=(batch, n_t),
        in_specs=[small, small, pl.BlockSpec((1, 1, n_experts, LANES), lambda b, t: (b, t, 0, 0))],
        out_specs=small,
        compiler_params=_cparams(2),
        name="positions",
    )(idx, rank, base)


def _dispatch_kernel(pos_ref, x_ref, xs_in_ref, xs_ref, sem, *, tile):
    del xs_in_ref

    def row_copy(j, k):
        dst = pos_ref[0, 0, k * tile + j]
        return pltpu.make_async_copy(x_ref.at[pl.ds(j, 1)], xs_ref.at[pl.ds(dst, 1)], sem)

    def start(j, carry):
        for k in range(TOP_K):
            row_copy(j, k).start()
        return carry

    def wait(j, carry):
        for k in range(TOP_K):
            row_copy(j, k).wait()
        return carry

    lax.fori_loop(0, tile, start, 0)
    lax.fori_loop(0, tile, wait, 0)


def _dispatch(pos_tiles, xn2, n_rows):
    n_tok, d_model = xn2.shape
    n_tiles = pos_tiles.shape[0]
    tile = n_tok // n_tiles
    xs0 = jnp.zeros((n_rows, d_model), F32)
    return pl.pallas_call(
        functools.partial(_dispatch_kernel, tile=tile),
        out_shape=jax.ShapeDtypeStruct((n_rows, d_model), F32),
        grid=(n_tiles,),
        in_specs=[pl.BlockSpec((1, 1, TOP_K * tile), lambda i: (i, 0, 0), memory_space=pltpu.SMEM),
                  pl.BlockSpec((tile, d_model), lambda i: (i, 0)),
                  pl.BlockSpec(memory_space=pl.ANY)],
        out_specs=pl.BlockSpec(memory_space=pl.ANY),
        scratch_shapes=[pltpu.SemaphoreType.DMA(())],
        input_output_aliases={2: 0},
        compiler_params=_cparams(1),
        name="dispatch_rows",
    )(pos_tiles, xn2, xs0)


def _expert_kernel(be_ref, nb_ref, xs_ref, wgu_ref, bgu_ref, wd_ref, bd_ref, ys_ref, *, d_expert):
    del be_ref

    @pl.when(pl.program_id(0) < nb_ref[0])
    def _():
        gu = _dot(xs_ref[...].astype(BF16), wgu_ref[0]) + bgu_ref[0]
        gl = jnp.minimum(gu[:, :d_expert], SWIGLU_LIMIT)
        up = jnp.clip(gu[:, d_expert:], -SWIGLU_LIMIT, SWIGLU_LIMIT)
        act = (gl * jax.nn.sigmoid(SWIGLU_ALPHA * gl) * (up + 1.0)).astype(BF16)
        ys_ref[...] = _dot(act, wd_ref[0]) + bd_ref[0]


def _experts(xs, block_e, n_used, w):
    n_rows, d_model = xs.shape
    blk = EXPERT_BLOCK
    n_blocks = n_rows // blk
    d_expert = w["w_down"].shape[1]

    def row_map(b, be, nb):
        return (jnp.minimum(b, nb[0] - 1), 0)

    def exp_map(b, be, nb):
        return (be[b], 0, 0)

    grid_spec = pltpu.PrefetchScalarGridSpec(
        num_scalar_prefetch=2,
        grid=(n_blocks,),
        in_specs=[pl.BlockSpec((blk, d_model), row_map),
                  pl.BlockSpec((1, d_model, 2 * d_expert), exp_map),
                  pl.BlockSpec((1, 1, 2 * d_expert), exp_map),
                  pl.BlockSpec((1, d_expert, d_model), exp_map),
                  pl.BlockSpec((1, 1, d_model), exp_map)],
        out_specs=pl.BlockSpec((blk, d_model), row_map),
    )
    return pl.pallas_call(
        functools.partial(_expert_kernel, d_expert=d_expert),
        out_shape=jax.ShapeDtypeStruct((n_rows, d_model), F32),
        grid_spec=grid_spec,
        compiler_params=_cparams(1),
        name="experts",
    )(block_e, n_used, xs, w["w_gu"], w["b_gu"], w["w_down"], w["b_down"])


def _combine_kernel(pos_ref, h2_ref, gate_ref, nw_ref, ys_ref, out_ref, buf, sem, *, tile, final_norm):
    def row_copy(j, k):
        src = pos_ref[0, 0, k * tile + j]
        return pltpu.make_async_copy(ys_ref.at[pl.ds(src, 1)], buf.at[k, pl.ds(j, 1)], sem)

    def start(j, carry):
        for k in range(TOP_K):
            row_copy(j, k).start()
        return carry

    def wait(j, carry):
        for k in range(TOP_K):
            row_copy(j, k).wait()
        return carry

    lax.fori_loop(0, tile, start, 0)
    lax.fori_loop(0, tile, wait, 0)
    gate = gate_ref[...]
    y = buf[0] * gate[:, 0:1]
    for k in range(1, TOP_K):
        y = y + buf[k] * gate[:, k:k + 1]
    h3 = h2_ref[...] + y
    out_ref[...] = _rms(h3, nw_ref[...]) if final_norm else h3


def _combine(pos_tiles, h2, gate_t, ys, norm_w, final_norm):
    n_tok, d_model = h2.shape
    n_tiles = pos_tiles.shape[0]
    tile = n_tok // n_tiles
    return pl.pallas_call(
        functools.partial(_combine_kernel, tile=tile, final_norm=final_norm),
        out_shape=jax.ShapeDtypeStruct((n_tok, d_model), F32),
        grid=(n_tiles,),
        in_specs=[pl.BlockSpec((1, 1, TOP_K * tile), lambda i: (i, 0, 0), memory_space=pltpu.SMEM),
                  pl.BlockSpec((tile, d_model), lambda i: (i, 0)),
                  pl.BlockSpec((tile, TOP_K), lambda i: (i, 0)),
                  _const_spec((1, d_model)),
                  pl.BlockSpec(memory_space=pl.ANY)],
        out_specs=pl.BlockSpec((tile, d_model), lambda i: (i, 0)),
        scratch_shapes=[pltpu.VMEM((TOP_K, tile, d_model), F32), pltpu.SemaphoreType.DMA(())],
        compiler_params=_cparams(1),
        name="combine",
    )(pos_tiles, h2, gate_t, norm_w, ys)


def _tile_major(a, tile):
    batch, k, length = a.shape
    a = a.reshape(batch, k, length // tile, tile).transpose(0, 2, 1, 3)
    return a.reshape(batch * (length // tile), 1, k * tile)


def _moe_tail(h2, xn, idx, gate, rank, cnt, w, norm_w, final_norm):
    batch, length, d_model = h2.shape
    n_tok = batch * length
    n_experts = cnt.shape[2]
    blk = EXPERT_BLOCK
    counts = cnt[..., 0]
    per_expert = counts.reshape(-1, n_experts)
    total = jnp.sum(per_expert, axis=0)
    padded = (total + blk - 1) // blk * blk
    ends = jnp.cumsum(padded)
    tile_start = jnp.cumsum(per_expert, axis=0) - per_expert
    base = (ends - padded)[None, :] + tile_start
    base = jnp.broadcast_to(base.reshape(counts.shape)[..., None], cnt.shape).astype(jnp.int32)
    n_blocks = (n_tok * TOP_K + n_experts * (blk - 1)) // blk
    n_rows = n_blocks * blk
    block_e = jnp.minimum(jnp.searchsorted(ends, jnp.arange(n_blocks) * blk, side="right"),
                          n_experts - 1).astype(jnp.int32)
    n_used = (ends[-1] // blk).astype(jnp.int32).reshape(1)

    pos = _positions(idx, rank, base)
    d_tile = min(DISPATCH_TILE, length)
    c_tile = min(COMBINE_TILE, length)
    xs = _dispatch(_tile_major(pos, d_tile), xn.reshape(n_tok, d_model), n_rows)
    ys = _experts(xs, block_e, n_used, w)
    gate_t = gate.transpose(0, 2, 1).reshape(n_tok, TOP_K)
    out = _combine(_tile_major(pos, c_tile), h2.reshape(n_tok, d_model), gate_t, ys, norm_w, final_norm)
    return out.reshape(batch, length, d_model)


def _layer(x, hist, s0, pos0, n_hist_valid, mk, mv, w, norm_w, final_norm):
    h1, new_hist, s_new = _mixer(x, hist, s0, pos0, n_hist_valid, w)
    h2, xn, idx, gate, rank, cnt = _cross_router(h1, mk.astype(BF16), mv.astype(BF16), w)
    out = _moe_tail(h2, xn, idx, gate, rank, cnt, w, norm_w, final_norm)
    return out, new_hist, s_new


def kernel(x_prompt, x_sample, mem_prompt, state_pool, state_ret, cache_mem_k, cache_mem_v, norm_mix_w, w_in, pool_w_grp, pool_scale, w_pool_out, ret_gn_w, w_ret_out, w_o, norm_mem_w, w_mk, w_mv, norm_cross_w, w_cq, w_co, norm_ffn_w, w_router, b_router, w_gu, b_gu, w_down, b_down, norm_final_w):
    depth = w_in.shape[0]
    batch_p = x_prompt.shape[0]
    m_len = mem_prompt.shape[1]
    d_model = x_prompt.shape[-1]
    hp, hs = x_prompt, x_sample
    norm_w = norm_final_w[None, :]
    mem_k_p, mem_v_p, pool_p, ret_p, pool_s, ret_s = [], [], [], [], [], []
    for l in range(depth):
        w = {
            "norm_mix_w": norm_mix_w[l][None, :], "w_in": w_in[l].astype(BF16),
            "pool_w_grp": pool_w_grp[l].astype(BF16), "pool_scale": pool_scale[l][None, :],
            "w_pool_out": w_pool_out[l].astype(BF16), "ret_gn_w": ret_gn_w[l][None, :],
            "w_ret_out": w_ret_out[l].astype(BF16), "w_o": w_o[l].astype(BF16),
            "norm_mem_w": norm_mem_w[l][None, :], "w_mk": w_mk[l].astype(BF16), "w_mv": w_mv[l].astype(BF16),
            "norm_cross_w": norm_cross_w[l][None, :], "w_cq": w_cq[l].astype(BF16), "w_co": w_co[l].astype(BF16),
            "norm_ffn_w": norm_ffn_w[l][None, :], "w_router_t": w_router[l].T.astype(BF16),
            "b_router_c": b_router[l][:, None],
            "w_gu": w_gu[l].astype(BF16), "b_gu": b_gu[l][:, None, :],
            "w_down": w_down[l].astype(BF16), "b_down": b_down[l][:, None, :],
        }
        last = l == depth - 1
        mk, mv = _mem_kv(mem_prompt, w)
        zero_hist = jnp.zeros((batch_p, POOL_HIST, POOL_WIDTH), F32)
        zero_state = jnp.zeros((batch_p, RET_HEADS, RET_DK, RET_DV), F32)
        hp, hist_p, s_p = _layer(hp, zero_hist, zero_state, 0, 0, mk, mv, w, norm_w, last)
        mem_k_p.append(mk.reshape(batch_p, m_len, MEM_HEADS, d_model // MEM_HEADS))
        mem_v_p.append(mv.reshape(batch_p, m_len, MEM_HEADS, d_model // MEM_HEADS))
        pool_p.append(hist_p)
        ret_p.append(s_p)
        past_len = 1024
        ck = cache_mem_k[l].reshape(cache_mem_k.shape[1], m_len, d_model)
        cv = cache_mem_v[l].reshape(cache_mem_v.shape[1], m_len, d_model)
        hs, hist_s, s_s = _layer(hs, state_pool[l], state_ret[l], past_len, POOL_HIST, ck, cv, w, norm_w, last)
        pool_s.append(hist_s)
        ret_s.append(s_s)
    if depth == 0:
        raise ValueError("depth must be positive")
    return (hp, hs, jnp.stack(mem_k_p), jnp.stack(mem_v_p), jnp.stack(pool_p), jnp.stack(ret_p),
            jnp.stack(pool_s), jnp.stack(ret_s))
```

```python
import functools

import jax
import jax.numpy as jnp
from jax import lax
from jax.experimental import pallas as pl
from jax.experimental.pallas import tpu as pltpu

F32 = jnp.float32
BF16 = jnp.bfloat16
I32 = jnp.int32

EPS = 1e-6
PAST_LEN = 1024
POOL_WINDOWS = (2, 4, 8, 16)
POOL_GROUP_DIM = 128
POOL_WIDTH = POOL_GROUP_DIM * len(POOL_WINDOWS)
POOL_HIST = max(POOL_WINDOWS) - 1
RET_HEADS = 4
RET_DK = 128
RET_DV = 256
ROPE_BASE = 10000.0
MEM_HEADS = 4
TOP_K = 4
SWIGLU_LIMIT = 7.0
SWIGLU_ALPHA = 1.702

SUBLANES = 8
LANES = 128
VMEM_LIMIT_BYTES = 56 * 1024 * 1024

MIXER_TILE = 256
RET_CHUNK = 128
CROSS_TILE = 512
EXPERT_BLOCK = 512
ROW_UNIT = SUBLANES
PERM_CHUNK = 256


def _cparams(n_axes):
    return pltpu.CompilerParams(dimension_semantics=("arbitrary",) * n_axes,
                                vmem_limit_bytes=VMEM_LIMIT_BYTES)


def _const_spec(shape):
    nd = len(shape)
    return pl.BlockSpec(shape, lambda *_: (0,) * nd, pipeline_mode=pl.Buffered(1))


def _round_up(n, m):
    return (n + m - 1) // m * m


def _rms(x32, w_row):
    ms = jnp.mean(x32 * x32, axis=-1, keepdims=True)
    return x32 * lax.rsqrt(ms + EPS) * w_row


def _dot(a, b):
    return jnp.dot(a, b, preferred_element_type=F32)


def _dot_nt(a, b):
    return lax.dot_general(a, b, (((1,), (1,)), ((), ())), preferred_element_type=F32)


def _dot_tn(a, b):
    return lax.dot_general(a, b, (((0,), (0,)), ((), ())), preferred_element_type=F32)


def _rope_kernel(inv_ref, sign_ref, cos_ref, sin_ref, *, pos0, tile):
    i = pl.program_id(0)
    pos = (lax.broadcasted_iota(I32, (tile, RET_DK), 0) + (pos0 + i * tile)).astype(F32)
    ang = pos * inv_ref[...]
    cos_ref[...] = jnp.cos(ang)
    sin_ref[...] = jnp.sin(ang) * sign_ref[...]


def _rope_tables(length, pos0):
    half = RET_DK // 2
    inv = 1.0 / (ROPE_BASE ** (jnp.arange(half, dtype=F32) / half))
    inv2 = jnp.concatenate([inv, inv])[None, :]
    sign = jnp.concatenate([-jnp.ones((half,), F32), jnp.ones((half,), F32)])[None, :]
    tile = min(length, 512)
    assert length % tile == 0
    return pl.pallas_call(
        functools.partial(_rope_kernel, pos0=pos0, tile=tile),
        out_shape=(jax.ShapeDtypeStruct((length, RET_DK), F32),) * 2,
        grid=(length // tile,),
        in_specs=[pl.BlockSpec((1, RET_DK), lambda i: (0, 0))] * 2,
        out_specs=(pl.BlockSpec((tile, RET_DK), lambda i: (i, 0)),) * 2,
        compiler_params=_cparams(1),
        name="rope_tables",
    )(inv2, sign)


def _mixer_kernel(x_ref, hist_ref, s0_ref, cos_ref, sin_ref, nmw_ref, win_ref, pwg_ref, pscale_ref,
                  wpo_ref, gnw_ref, wro_ref, wo_ref, dec_ref, qd_ref, kd_ref, gc_ref,
                  h_ref, hist_out_ref, s_out_ref,
                  ubuf, s_scr, z_scr, r_scr, *, tile, chunk, n_hist_valid, d_model):
    t = pl.program_id(1)
    n_t = pl.num_programs(1)
    q_off = POOL_WIDTH
    k_off = q_off + RET_HEADS * RET_DK
    v_off = k_off + RET_HEADS * RET_DK
    g_off = v_off + RET_HEADS * RET_DV
    ap_off = g_off + RET_HEADS * RET_DV
    ar_off = ap_off + d_model

    @pl.when(t == 0)
    def _():
        ubuf[0:1, :] = jnp.zeros((1, POOL_WIDTH), F32)
        ubuf[1:1 + POOL_HIST, :] = hist_ref[0]
        s_scr[...] = s0_ref[0]

    x = x_ref[0]
    xn = _rms(x, nmw_ref[...]).astype(BF16)
    z_scr[...] = _dot(xn, win_ref[...])

    hist_rows = POOL_HIST + 1
    ubuf[hist_rows:hist_rows + tile, :] = z_scr[:, 0:POOL_WIDTH]
    full = ubuf[...]
    pos = t * tile + lax.broadcasted_iota(I32, (tile, 1), 0)
    ys = []
    for g, w in enumerate(POOL_WINDOWS):
        f = full[:, g * POOL_GROUP_DIM:(g + 1) * POOL_GROUP_DIM]
        s = f
        shift = 1
        while shift < w:
            s = s + pltpu.roll(s, shift, 0)
            shift *= 2
        cnt = jnp.minimum(w, pos + 1 + n_hist_valid).astype(F32)
        d = s[hist_rows:, :] / cnt - f[hist_rows:, :]
        ys.append(_dot(d.astype(BF16), pwg_ref[g]))
    yp = jnp.concatenate(ys, axis=-1) * pscale_ref[...]
    branch_pool = _dot(yp.astype(BF16), wpo_ref[...])
    new_hist = full[tile + 1:tile + 1 + POOL_HIST, :]
    ubuf[1:1 + POOL_HIST, :] = new_hist

    @pl.when(t == n_t - 1)
    def _():
        hist_out_ref[0] = new_hist

    for c in range(tile // chunk):
        rows = slice(c * chunk, (c + 1) * chunk)
        cosc = cos_ref[rows, :]
        sinc = sin_ref[rows, :]
        for h in range(RET_HEADS):
            qh = z_scr[rows, q_off + h * RET_DK:q_off + (h + 1) * RET_DK]
            kh = z_scr[rows, k_off + h * RET_DK:k_off + (h + 1) * RET_DK]
            vb = z_scr[rows, v_off + h * RET_DV:v_off + (h + 1) * RET_DV].astype(BF16)
            qr = qh * cosc + pltpu.roll(qh, RET_DK // 2, 1) * sinc
            kr = (kh * cosc + pltpu.roll(kh, RET_DK // 2, 1) * sinc) * (RET_DK ** -0.5)
            qb = qr.astype(BF16)
            kb = kr.astype(BF16)
            scores = _dot_nt(qb, kb) * dec_ref[h]
            state = s_scr[h]
            o = _dot(scores.astype(BF16), vb) + _dot(qb, state.astype(BF16)) * qd_ref[h]
            s_scr[h] = gc_ref[h] * state + _dot_tn((kr * kd_ref[h]).astype(BF16), vb)
            mu = jnp.mean(o, axis=-1, keepdims=True)
            dlt = o - mu
            var = jnp.mean(dlt * dlt, axis=-1, keepdims=True)
            on = dlt * lax.rsqrt(var + EPS) * gnw_ref[:, h * RET_DV:(h + 1) * RET_DV]
            gate = z_scr[rows, g_off + h * RET_DV:g_off + (h + 1) * RET_DV]
            r_scr[rows, h * RET_DV:(h + 1) * RET_DV] = (on * (gate * jax.nn.sigmoid(gate))).astype(BF16)

    @pl.when(t == n_t - 1)
    def _():
        s_out_ref[0] = s_scr[...]

    branch_ret = _dot(r_scr[...], wro_ref[...])
    merged = (jax.nn.sigmoid(z_scr[:, ap_off:ap_off + d_model]) * branch_pool
              + jax.nn.sigmoid(z_scr[:, ar_off:ar_off + d_model]) * branch_ret)
    h_ref[0] = x + _dot(merged.astype(BF16), wo_ref[...])


def _ret_tables(chunk):
    lg = jnp.log(1.0 - 2.0 ** (-5.0 - jnp.arange(RET_HEADS, dtype=F32)))
    idx = jnp.arange(chunk)
    rel = idx[:, None] - idx[None, :]
    decay = jnp.where(rel[None] >= 0, jnp.exp(jnp.maximum(rel, 0)[None].astype(F32) * lg[:, None, None]), 0.0)
    q_decay = jnp.exp((idx + 1).astype(F32)[None, :] * lg[:, None])[:, :, None]
    k_decay = jnp.exp((chunk - 1 - idx).astype(F32)[None, :] * lg[:, None])[:, :, None]
    g_chunk = jnp.exp(chunk * lg)
    return decay, q_decay, k_decay, g_chunk


def _mixer(x, hist, s0, pos0, n_hist_valid, w):
    batch, length, d_model = x.shape
    tile = min(MIXER_TILE, length)
    chunk = min(RET_CHUNK, tile)
    assert length % tile == 0 and tile % chunk == 0 and tile >= POOL_HIST + 1
    in_width = w["w_in"].shape[1]
    cos, sin = _rope_tables(length, pos0)
    decay, q_decay, k_decay, g_chunk = _ret_tables(chunk)
    kern = functools.partial(_mixer_kernel, tile=tile, chunk=chunk, n_hist_valid=n_hist_valid, d_model=d_model)
    return pl.pallas_call(
        kern,
        out_shape=(jax.ShapeDtypeStruct(x.shape, F32),
                   jax.ShapeDtypeStruct(hist.shape, F32),
                   jax.ShapeDtypeStruct(s0.shape, F32)),
        grid=(batch, length // tile),
        in_specs=[
            pl.BlockSpec((1, tile, d_model), lambda b, t: (b, t, 0)),
            pl.BlockSpec((1, POOL_HIST, POOL_WIDTH), lambda b, t: (b, 0, 0)),
            pl.BlockSpec((1, RET_HEADS, RET_DK, RET_DV), lambda b, t: (b, 0, 0, 0)),
            pl.BlockSpec((tile, RET_DK), lambda b, t: (t, 0)),
            pl.BlockSpec((tile, RET_DK), lambda b, t: (t, 0)),
            _const_spec((1, d_model)),
            _const_spec((d_model, in_width)),
            _const_spec(w["pool_w_grp"].shape),
            _const_spec((1, POOL_WIDTH)),
            _const_spec((POOL_WIDTH, d_model)),
            _const_spec((1, RET_HEADS * RET_DV)),
            _const_spec((RET_HEADS * RET_DV, d_model)),
            _const_spec((d_model, d_model)),
            _const_spec(decay.shape),
            _const_spec(q_decay.shape),
            _const_spec(k_decay.shape),
            pl.BlockSpec(memory_space=pltpu.SMEM),
        ],
        out_specs=(
            pl.BlockSpec((1, tile, d_model), lambda b, t: (b, t, 0)),
            pl.BlockSpec((1, POOL_HIST, POOL_WIDTH), lambda b, t: (b, 0, 0)),
            pl.BlockSpec((1, RET_HEADS, RET_DK, RET_DV), lambda b, t: (b, 0, 0, 0)),
        ),
        scratch_shapes=[
            pltpu.VMEM((POOL_HIST + 1 + tile, POOL_WIDTH), F32),
            pltpu.VMEM((RET_HEADS, RET_DK, RET_DV), F32),
            pltpu.VMEM((tile, in_width), F32),
            pltpu.VMEM((tile, RET_HEADS * RET_DV), BF16),
        ],
        compiler_params=_cparams(2),
        name="mixer",
    )(x, hist, s0, cos, sin, w["norm_mix_w"], w["w_in"], w["pool_w_grp"], w["pool_scale"], w["w_pool_out"],
      w["ret_gn_w"], w["w_ret_out"], w["w_o"], decay, q_decay, k_decay, g_chunk)


def _mem_kv_kernel(mem_ref, nw_ref, wk_ref, wv_ref, k_ref, v_ref):
    mn = _rms(mem_ref[0], nw_ref[...]).astype(BF16)
    k_ref[0] = _dot(mn, wk_ref[...])
    v_ref[0] = _dot(mn, wv_ref[...])


def _mem_kv(mem, w):
    batch, m_len, d_model = mem.shape
    out = jax.ShapeDtypeStruct((batch, m_len, d_model), F32)
    blk = pl.BlockSpec((1, m_len, d_model), lambda b: (b, 0, 0))
    return pl.pallas_call(
        _mem_kv_kernel,
        out_shape=(out, out),
        grid=(batch,),
        in_specs=[blk, _const_spec((1, d_model)), _const_spec((d_model, d_model)), _const_spec((d_model, d_model))],
        out_specs=(blk, blk),
        compiler_params=_cparams(1),
        name="mem_kv",
    )(mem, w["norm_mem_w"], w["w_mk"], w["w_mv"])


def _cross_router_kernel(h_ref, mk_ref, mv_ref, ncw_ref, wcq_ref, wco_ref, nfw_ref, wrt_ref, brt_ref,
                         h2_ref, xn_ref, lp_ref, gate_ref, cnt_ref,
                         o_scr, *, tile, head_dim, n_experts):
    h1 = h_ref[0]
    hn = _rms(h1, ncw_ref[...]).astype(BF16)
    q = _dot(hn, wcq_ref[...]).astype(BF16)
    for hd in range(MEM_HEADS):
        cols = slice(hd * head_dim, (hd + 1) * head_dim)
        s = _dot_nt(q[:, cols], mk_ref[0, :, cols]) * (head_dim ** -0.5)
        e = jnp.exp(s - jnp.max(s, axis=-1, keepdims=True))
        p = e / jnp.sum(e, axis=-1, keepdims=True)
        o_scr[:, cols] = _dot(p.astype(BF16), mv_ref[0, :, cols]).astype(BF16)
    h2 = h1 + _dot(o_scr[...], wco_ref[...])
    h2_ref[0] = h2
    xn = _rms(h2, nfw_ref[...]).astype(BF16)
    xn_ref[0] = xn

    logits = _dot_nt(wrt_ref[...], xn) + brt_ref[...]
    e_iota = lax.broadcasted_iota(I32, (n_experts, tile), 0)
    work = logits
    vals, sels = [], []
    for _ in range(TOP_K):
        m = jnp.max(work, axis=0, keepdims=True)
        first = jnp.min(jnp.where(work == m, e_iota, n_experts), axis=0, keepdims=True)
        sel = e_iota == first
        vals.append(m)
        sels.append(sel)
        work = jnp.where(sel, -jnp.inf, work)
    exps = [jnp.exp(v - vals[0]) for v in vals]
    denom = exps[0] + exps[1] + exps[2] + exps[3]
    assigned = jnp.zeros((n_experts, tile), F32)
    for sel in sels:
        assigned = assigned + sel.astype(F32)
    r_iota = lax.broadcasted_iota(I32, (tile, tile), 0)
    c_iota = lax.broadcasted_iota(I32, (tile, tile), 1)
    prior = _dot(assigned.astype(BF16), (r_iota < c_iota).astype(BF16))
    counts = jnp.sum(assigned, axis=1, keepdims=True)
    run = jnp.floor((counts + (ROW_UNIT - 1)) * (1.0 / ROW_UNIT)) * ROW_UNIT
    er = lax.broadcasted_iota(I32, (n_experts, n_experts), 0)
    ec = lax.broadcasted_iota(I32, (n_experts, n_experts), 1)
    run_start = _dot((ec < er).astype(BF16), jnp.broadcast_to(run, (n_experts, LANES)).astype(BF16))[:, 0:1]
    for k in range(TOP_K):
        lp_ref[0, k:k + 1, :] = jnp.sum(jnp.where(sels[k], run_start + prior, 0.0), axis=0, keepdims=True).astype(I32)
        gate_ref[0, k:k + 1, :] = exps[k] / denom
    cnt_ref[0, 0] = jnp.broadcast_to(counts, (n_experts, LANES)).astype(I32)


def _cross_router(h1, mk, mv, w):
    batch, length, d_model = h1.shape
    m_len = mk.shape[1]
    head_dim = d_model // MEM_HEADS
    n_experts = w["w_router_t"].shape[0]
    tile = min(CROSS_TILE, length)
    assert length % tile == 0
    n_t = length // tile
    kern = functools.partial(_cross_router_kernel, tile=tile, head_dim=head_dim, n_experts=n_experts)
    tok = pl.BlockSpec((1, tile, d_model), lambda b, t: (b, t, 0))
    mem = pl.BlockSpec((1, m_len, d_model), lambda b, t: (b, 0, 0))
    small = pl.BlockSpec((1, TOP_K, tile), lambda b, t: (b, 0, t))
    return pl.pallas_call(
        kern,
        out_shape=(jax.ShapeDtypeStruct(h1.shape, F32),
                   jax.ShapeDtypeStruct(h1.shape, BF16),
                   jax.ShapeDtypeStruct((batch, TOP_K, length), I32),
                   jax.ShapeDtypeStruct((batch, TOP_K, length), F32),
                   jax.ShapeDtypeStruct((batch, n_t, n_experts, LANES), I32)),
        grid=(batch, n_t),
        in_specs=[tok, mem, mem,
                  _const_spec((1, d_model)), _const_spec((d_model, d_model)), _const_spec((d_model, d_model)),
                  _const_spec((1, d_model)), _const_spec((n_experts, d_model)), _const_spec((n_experts, 1))],
        out_specs=(tok, tok, small, small,
                   pl.BlockSpec((1, 1, n_experts, LANES), lambda b, t: (b, t, 0, 0))),
        scratch_shapes=[pltpu.VMEM((tile, d_model), BF16)],
        compiler_params=_cparams(2),
        name="cross_router",
    )(h1, mk, mv, w["norm_cross_w"], w["w_cq"], w["w_co"], w["norm_ffn_w"], w["w_router_t"], w["b_router_c"])


def _unit_copy(src_ref, dst_ref, src_unit, dst_unit, sem):
    def rows_of(unit):
        start = unit * ROW_UNIT
        return pl.ds(start if isinstance(start, int) else pl.multiple_of(start, ROW_UNIT), ROW_UNIT)

    return pltpu.make_async_copy(src_ref.at[rows_of(src_unit)], dst_ref.at[rows_of(dst_unit)], sem)


def _dispatch_kernel(nun_ref, tail_ref, ntail_ref, lp_ref, gate_ref, xn_ref, dst_ref, dstp_ref, xs_ref,
                     rows, zero_buf, sems, tail_sem, *, tile, n_local, d_model):
    i = pl.program_id(0)
    n_i = pl.num_programs(0)
    slot = lax.rem(i, 2)

    @pl.when(i == 0)
    def _():
        zero_buf[...] = jnp.zeros(zero_buf.shape, F32)

        def start(j, c):
            _unit_copy(zero_buf, xs_ref, 0, tail_ref[j], tail_sem).start()
            return c

        def wait(j, c):
            _unit_copy(zero_buf, xs_ref, 0, tail_ref[j], tail_sem).wait()
            return c

        lax.fori_loop(0, ntail_ref[0], start, 0)
        lax.fori_loop(0, ntail_ref[0], wait, 0)

    xn = xn_ref[0]
    lps = [lp_ref[0, k:k + 1, :] for k in range(TOP_K)]
    gates = [gate_ref[0, k:k + 1, :] for k in range(TOP_K)]
    for c in range(n_local // PERM_CHUNK):
        r_iota = lax.broadcasted_iota(I32, (PERM_CHUNK, tile), 0) + c * PERM_CHUNK
        hit = r_iota == lps[0]
        wmat = jnp.where(hit, gates[0], 0.0)
        for k in range(1, TOP_K):
            hk = r_iota == lps[k]
            hit = hit | hk
            wmat = wmat + jnp.where(hk, gates[k], 0.0)
        perm = jnp.where(hit, 1.0, 0.0).astype(BF16)
        rs = pl.ds(c * PERM_CHUNK, PERM_CHUNK)
        rows[slot, rs, 0:d_model] = _dot(perm, xn)
        rows[slot, rs, d_model:d_model + LANES] = jnp.broadcast_to(
            jnp.sum(wmat, axis=1, keepdims=True), (PERM_CHUNK, LANES))

    def start(j, c):
        _unit_copy(rows.at[slot], xs_ref, j, dst_ref[0, 0, j], sems.at[slot]).start()
        return c

    lax.fori_loop(0, nun_ref[i], start, 0)

    @pl.when(i > 0)
    def _():
        def wait(j, c):
            _unit_copy(rows.at[1 - slot], xs_ref, j, dstp_ref[0, 0, j], sems.at[1 - slot]).wait()
            return c

        lax.fori_loop(0, nun_ref[i - 1], wait, 0)

    @pl.when(i == n_i - 1)
    def _():
        def wait(j, c):
            _unit_copy(rows.at[slot], xs_ref, j, dst_ref[0, 0, j], sems.at[slot]).wait()
            return c

        lax.fori_loop(0, nun_ref[i], wait, 0)


def _dispatch(lp, gate, xn, plan, n_rows):
    batch, length, d_model = xn.shape
    tile, n_local, n_units = plan["tile"], plan["n_local"], plan["n_units"]
    n_t = length // tile
    width = d_model + LANES
    small = pl.BlockSpec((1, TOP_K, tile), lambda i, *_: (i // n_t, 0, i % n_t))
    grid_spec = pltpu.PrefetchScalarGridSpec(
        num_scalar_prefetch=3,
        grid=(batch * n_t,),
        in_specs=[small, small,
                  pl.BlockSpec((1, tile, d_model), lambda i, *_: (i // n_t, i % n_t, 0)),
                  pl.BlockSpec((1, 1, n_units), lambda i, *_: (i, 0, 0), memory_space=pltpu.SMEM),
                  pl.BlockSpec((1, 1, n_units), lambda i, *_: (jnp.maximum(i - 1, 0), 0, 0),
                               memory_space=pltpu.SMEM)],
        out_specs=pl.BlockSpec(memory_space=pl.ANY),
        scratch_shapes=[pltpu.VMEM((2, n_local, width), F32), pltpu.VMEM((ROW_UNIT, width), F32),
                        pltpu.SemaphoreType.DMA((2,)), pltpu.SemaphoreType.DMA(())],
    )
    return pl.pallas_call(
        functools.partial(_dispatch_kernel, tile=tile, n_local=n_local, d_model=d_model),
        out_shape=jax.ShapeDtypeStruct((n_rows, width), F32),
        grid_spec=grid_spec,
        compiler_params=_cparams(1),
        name="dispatch",
    )(plan["units_per_tile"], plan["tail_units"], plan["n_tail"], lp, gate, xn, plan["dst_units"], plan["dst_units"])


def _expert_kernel(be_ref, nb_ref, xs_ref, wgu_ref, bgu_ref, wd_ref, bd_ref, ys_ref, *, d_model, d_expert):
    del be_ref

    @pl.when(pl.program_id(0) < nb_ref[0])
    def _():
        gu = _dot(xs_ref[:, 0:d_model].astype(BF16), wgu_ref[0]) + bgu_ref[0]
        gl = jnp.minimum(gu[:, :d_expert], SWIGLU_LIMIT)
        up = jnp.clip(gu[:, d_expert:], -SWIGLU_LIMIT, SWIGLU_LIMIT)
        act = (gl * jax.nn.sigmoid(SWIGLU_ALPHA * gl) * (up + 1.0)).astype(BF16)
        ys_ref[...] = (_dot(act, wd_ref[0]) + bd_ref[0]) * xs_ref[:, d_model:d_model + 1]


def _experts(xs, block_e, n_used, w):
    n_rows, width = xs.shape
    d_model = width - LANES
    blk = EXPERT_BLOCK
    d_expert = w["w_down"].shape[1]

    def row_map(b, be, nb):
        return (jnp.minimum(b, nb[0] - 1), 0)

    def exp_map(b, be, nb):
        return (be[b], 0, 0)

    grid_spec = pltpu.PrefetchScalarGridSpec(
        num_scalar_prefetch=2,
        grid=(n_rows // blk,),
        in_specs=[pl.BlockSpec((blk, width), row_map),
                  pl.BlockSpec((1, d_model, 2 * d_expert), exp_map),
                  pl.BlockSpec((1, 1, 2 * d_expert), exp_map),
                  pl.BlockSpec((1, d_expert, d_model), exp_map),
                  pl.BlockSpec((1, 1, d_model), exp_map)],
        out_specs=pl.BlockSpec((blk, d_model), row_map),
    )
    return pl.pallas_call(
        functools.partial(_expert_kernel, d_model=d_model, d_expert=d_expert),
        out_shape=jax.ShapeDtypeStruct((n_rows, d_model), F32),
        grid_spec=grid_spec,
        compiler_params=_cparams(1),
        name="experts",
    )(block_e, n_used, xs, w["w_gu"], w["b_gu"], w["w_down"], w["b_down"])


def _combine_kernel(nun_ref, lpc_ref, h2_ref, nw_ref, src_ref, srcn_ref, ys_ref, out_ref, rows, sems,
                    *, tile, n_local, final_norm):
    i = pl.program_id(0)
    n_i = pl.num_programs(0)
    slot = lax.rem(i, 2)

    def fetch(step, units_ref, to_slot):
        def start(j, c):
            _unit_copy(ys_ref, rows.at[to_slot], units_ref[0, 0, j], j, sems.at[to_slot]).start()
            return c

        lax.fori_loop(0, nun_ref[step], start, 0)

    @pl.when(i == 0)
    def _():
        rows[...] = jnp.zeros(rows.shape, F32)
        fetch(0, src_ref, 0)

    @pl.when(i + 1 < n_i)
    def _():
        fetch(i + 1, srcn_ref, 1 - slot)

    def wait(j, c):
        _unit_copy(ys_ref, rows.at[slot], src_ref[0, 0, j], j, sems.at[slot]).wait()
        return c

    lax.fori_loop(0, nun_ref[i], wait, 0)

    lps = [lpc_ref[:, k:k + 1] for k in range(TOP_K)]
    y = jnp.zeros(out_ref.shape, F32)
    for c in range(n_local // PERM_CHUNK):
        c_iota = lax.broadcasted_iota(I32, (tile, PERM_CHUNK), 1) + c * PERM_CHUNK
        hit = c_iota == lps[0]
        for k in range(1, TOP_K):
            hit = hit | (c_iota == lps[k])
        sel = jnp.where(hit, 1.0, 0.0).astype(BF16)
        y = y + _dot(sel, rows[slot, pl.ds(c * PERM_CHUNK, PERM_CHUNK), :].astype(BF16))
    h3 = h2_ref[...] + y
    out_ref[...] = _rms(h3, nw_ref[...]) if final_norm else h3


def _combine(lp_cols, h2, ys, norm_w, plan, final_norm):
    n_tok, d_model = h2.shape
    tile, n_local, n_units = plan["tile"], plan["n_local"], plan["n_units"]
    n_tiles = n_tok // tile
    grid_spec = pltpu.PrefetchScalarGridSpec(
        num_scalar_prefetch=1,
        grid=(n_tiles,),
        in_specs=[pl.BlockSpec((tile, TOP_K), lambda i, *_: (i, 0)),
                  pl.BlockSpec((tile, d_model), lambda i, *_: (i, 0)),
                  _const_spec((1, d_model)),
                  pl.BlockSpec((1, 1, n_units), lambda i, *_: (i, 0, 0), memory_space=pltpu.SMEM),
                  pl.BlockSpec((1, 1, n_units), lambda i, *_: (jnp.minimum(i + 1, n_tiles - 1), 0, 0),
                               memory_space=pltpu.SMEM),
                  pl.BlockSpec(memory_space=pl.ANY)],
        out_specs=pl.BlockSpec((tile, d_model), lambda i, *_: (i, 0)),
        scratch_shapes=[pltpu.VMEM((2, n_local, d_model), F32), pltpu.SemaphoreType.DMA((2,))],
    )
    return pl.pallas_call(
        functools.partial(_combine_kernel, tile=tile, n_local=n_local, final_norm=final_norm),
        out_shape=jax.ShapeDtypeStruct((n_tok, d_model), F32),
        grid_spec=grid_spec,
        compiler_params=_cparams(1),
        name="combine",
    )(plan["units_per_tile"], lp_cols, h2, norm_w, plan["dst_units"], plan["dst_units"], ys)


def _moe_plan(cnt, tile):
    n_experts = cnt.shape[2]
    blk = EXPERT_BLOCK
    counts = cnt[..., 0].reshape(-1, n_experts)
    n_tiles = counts.shape[0]
    run_units = (counts + ROW_UNIT - 1) // ROW_UNIT
    unit_end = jnp.cumsum(run_units, axis=1)
    expert_units = jnp.sum(run_units, axis=0)
    blk_units = blk // ROW_UNIT
    padded_units = (expert_units + blk_units - 1) // blk_units * blk_units
    expert_end = jnp.cumsum(padded_units)
    expert_start = expert_end - padded_units
    run_base = expert_start[None, :] + jnp.cumsum(run_units, axis=0) - run_units
    n_local = _round_up(TOP_K * tile + (ROW_UNIT - 1) * n_experts, PERM_CHUNK)
    n_units = n_local // ROW_UNIT
    j = jnp.arange(n_units)
    owner = jnp.sum(j[None, :, None] >= unit_end[:, None, :], axis=-1)
    owner = jnp.minimum(owner, n_experts - 1)
    run_first = jnp.take_along_axis(unit_end - run_units, owner, axis=1)
    dst_units = jnp.take_along_axis(run_base, owner, axis=1) + (j[None, :] - run_first)
    units_per_tile = unit_end[:, -1]
    dst_units = jnp.where(j[None, :] < units_per_tile[:, None], dst_units, 0)
    tj = jnp.arange(n_experts * (blk_units - 1))
    tail_len = padded_units - expert_units
    tail_end = jnp.cumsum(tail_len)
    towner = jnp.minimum(jnp.sum(tj[:, None] >= tail_end[None, :], axis=-1), n_experts - 1)
    tail_units = (expert_start + expert_units)[towner] + (tj - (tail_end - tail_len)[towner])
    n_tail = tail_end[-1]
    tail_units = jnp.where(tj < n_tail, tail_units, 0)
    n_blocks = (n_tiles * tile * TOP_K + n_tiles * n_experts * (ROW_UNIT - 1) + n_experts * (blk - 1)) // blk
    block_e = jnp.minimum(jnp.sum((jnp.arange(n_blocks) * blk_units)[:, None] >= expert_end[None, :], axis=-1),
                          n_experts - 1)
    return {
        "tile": tile, "n_local": n_local, "n_units": n_units, "n_rows": n_blocks * blk,
        "units_per_tile": units_per_tile.astype(I32),
        "dst_units": dst_units.astype(I32).reshape(n_tiles, 1, n_units),
        "tail_units": tail_units.astype(I32), "n_tail": n_tail.astype(I32).reshape(1),
        "block_e": block_e.astype(I32), "n_used": (expert_end[-1] // blk_units).astype(I32).reshape(1),
    }


def _moe_tail(h2, xn, lp, gate, cnt, w, norm_w, final_norm):
    batch, length, d_model = h2.shape
    n_tok = batch * length
    tile = length // cnt.shape[1]
    plan = _moe_plan(cnt, tile)
    xs = _dispatch(lp, gate, xn, plan, plan["n_rows"])
    ys = _experts(xs, plan["block_e"], plan["n_used"], w)
    lp_cols = lp.transpose(0, 2, 1).reshape(n_tok, TOP_K)
    out = _combine(lp_cols, h2.reshape(n_tok, d_model), ys, norm_w, plan, final_norm)
    return out.reshape(batch, length, d_model)


def _layer(x, hist, s0, pos0, n_hist_valid, mk, mv, w, norm_w, final_norm):
    h1, new_hist, s_new = _mixer(x, hist, s0, pos0, n_hist_valid, w)
    h2, xn, lp, gate, cnt = _cross_router(h1, mk.astype(BF16), mv.astype(BF16), w)
    out = _moe_tail(h2, xn, lp, gate, cnt, w, norm_w, final_norm)
    return out, new_hist, s_new


def kernel(x_prompt, x_sample, mem_prompt, state_pool, state_ret, cache_mem_k, cache_mem_v, norm_mix_w, w_in, pool_w_grp, pool_scale, w_pool_out, ret_gn_w, w_ret_out, w_o, norm_mem_w, w_mk, w_mv, norm_cross_w, w_cq, w_co, norm_ffn_w, w_router, b_router, w_gu, b_gu, w_down, b_down, norm_final_w):
    depth = w_in.shape[0]
    batch_p = x_prompt.shape[0]
    m_len = mem_prompt.shape[1]
    d_model = x_prompt.shape[-1]
    hp, hs = x_prompt, x_sample
    norm_w = norm_final_w[None, :]
    mem_k_p, mem_v_p, pool_p, ret_p, pool_s, ret_s = [], [], [], [], [], []
    for l in range(depth):
        w = {
            "norm_mix_w": norm_mix_w[l][None, :], "w_in": w_in[l].astype(BF16),
            "pool_w_grp": pool_w_grp[l].astype(BF16), "pool_scale": pool_scale[l][None, :],
            "w_pool_out": w_pool_out[l].astype(BF16), "ret_gn_w": ret_gn_w[l][None, :],
            "w_ret_out": w_ret_out[l].astype(BF16), "w_o": w_o[l].astype(BF16),
            "norm_mem_w": norm_mem_w[l][None, :], "w_mk": w_mk[l].astype(BF16), "w_mv": w_mv[l].astype(BF16),
            "norm_cross_w": norm_cross_w[l][None, :], "w_cq": w_cq[l].astype(BF16), "w_co": w_co[l].astype(BF16),
            "norm_ffn_w": norm_ffn_w[l][None, :], "w_router_t": w_router[l].T.astype(BF16),
            "b_router_c": b_router[l][:, None],
            "w_gu": w_gu[l].astype(BF16), "b_gu": b_gu[l][:, None, :],
            "w_down": w_down[l].astype(BF16), "b_down": b_down[l][:, None, :],
        }
        last = l == depth - 1
        mk, mv = _mem_kv(mem_prompt, w)
        zero_hist = jnp.zeros((batch_p, POOL_HIST, POOL_WIDTH), F32)
        zero_state = jnp.zeros((batch_p, RET_HEADS, RET_DK, RET_DV), F32)
        hp, hist_p, s_p = _layer(hp, zero_hist, zero_state, 0, 0, mk, mv, w, norm_w, last)
        mem_k_p.append(mk.reshape(batch_p, m_len, MEM_HEADS, d_model // MEM_HEADS))
        mem_v_p.append(mv.reshape(batch_p, m_len, MEM_HEADS, d_model // MEM_HEADS))
        pool_p.append(hist_p)
        ret_p.append(s_p)
        ck = cache_mem_k[l].reshape(cache_mem_k.shape[1], m_len, d_model)
        cv = cache_mem_v[l].reshape(cache_mem_v.shape[1], m_len, d_model)
        hs, hist_s, s_s = _layer(hs, state_pool[l], state_ret[l], PAST_LEN, POOL_HIST, ck, cv, w, norm_w, last)
        pool_s.append(hist_s)
        ret_s.append(s_s)
    return (hp, hs, jnp.stack(mem_k_p), jnp.stack(mem_v_p), jnp.stack(pool_p), jnp.stack(ret_p),
            jnp.stack(pool_s), jnp.stack(ret_s))
```

```python
import functools

import jax
import jax.numpy as jnp
from jax import lax
from jax.experimental import pallas as pl
from jax.experimental.pallas import tpu as pltpu

F32 = jnp.float32
BF16 = jnp.bfloat16
I32 = jnp.int32
U32 = jnp.uint32
HIGH_HALF = 0xFFFF0000

EPS = 1e-6
PAST_LEN = 1024
POOL_WINDOWS = (2, 4, 8, 16)
POOL_GROUP_DIM = 128
POOL_WIDTH = POOL_GROUP_DIM * len(POOL_WINDOWS)
POOL_HIST = max(POOL_WINDOWS) - 1
RET_HEADS = 4
RET_DK = 128
RET_DV = 256
ROPE_BASE = 10000.0
MEM_HEADS = 4
TOP_K = 4
SWIGLU_LIMIT = 7.0
SWIGLU_ALPHA = 1.702

SUBLANES = 8
LANES = 128
VMEM_LIMIT_BYTES = 56 * 1024 * 1024

MIXER_TILE = 512
RET_CHUNK = 256
CROSS_TILE = 512
EXPERT_BLOCK = 512
ROW_UNIT = SUBLANES
PERM_CHUNK = 256


def _cparams(n_axes):
    return pltpu.CompilerParams(dimension_semantics=("arbitrary",) * n_axes,
                                vmem_limit_bytes=VMEM_LIMIT_BYTES)


def _const_spec(shape):
    nd = len(shape)
    return pl.BlockSpec(shape, lambda *_: (0,) * nd, pipeline_mode=pl.Buffered(1))


def _round_up(n, m):
    return (n + m - 1) // m * m


def _rms(x32, w_row):
    ms = jnp.mean(x32 * x32, axis=-1, keepdims=True)
    return x32 * lax.rsqrt(ms + EPS) * w_row


def _dot(a, b):
    return jnp.dot(a, b, preferred_element_type=F32)


def _dot_nt(a, b):
    return lax.dot_general(a, b, (((1,), (1,)), ((), ())), preferred_element_type=F32)


def _dot_tn(a, b):
    return lax.dot_general(a, b, (((0,), (0,)), ((), ())), preferred_element_type=F32)


def _rope_kernel(inv_ref, sign_ref, cos_ref, sin_ref, *, pos0, tile):
    i = pl.program_id(0)
    pos = (lax.broadcasted_iota(I32, (tile, RET_DK), 0) + (pos0 + i * tile)).astype(F32)
    ang = pos * inv_ref[...]
    cos_ref[...] = jnp.cos(ang)
    sin_ref[...] = jnp.sin(ang) * sign_ref[...]


def _rope_tables(length, pos0):
    half = RET_DK // 2
    inv = 1.0 / (ROPE_BASE ** (jnp.arange(half, dtype=F32) / half))
    inv2 = jnp.concatenate([inv, inv])[None, :]
    sign = jnp.concatenate([-jnp.ones((half,), F32), jnp.ones((half,), F32)])[None, :]
    tile = min(length, 512)
    assert length % tile == 0
    return pl.pallas_call(
        functools.partial(_rope_kernel, pos0=pos0, tile=tile),
        out_shape=(jax.ShapeDtypeStruct((length, RET_DK), F32),) * 2,
        grid=(length // tile,),
        in_specs=[pl.BlockSpec((1, RET_DK), lambda i: (0, 0))] * 2,
        out_specs=(pl.BlockSpec((tile, RET_DK), lambda i: (i, 0)),) * 2,
        compiler_params=_cparams(1),
        name="rope_tables",
    )(inv2, sign)


def _mixer_kernel(x_ref, hist_ref, s0_ref, cos_ref, sin_ref, nmw_ref, win_ref, pwg_ref, pscale_ref,
                  wpo_ref, gnw_ref, wro_ref, wo_ref, dec_ref, qd_ref, kd_ref, gc_ref,
                  h_ref, hist_out_ref, s_out_ref,
                  ubuf, s_scr, z_scr, r_scr, yp_scr, *, tile, chunk, n_hist_valid, d_model):
    t = pl.program_id(1)
    n_t = pl.num_programs(1)
    q_off = POOL_WIDTH
    k_off = q_off + RET_HEADS * RET_DK
    v_off = k_off + RET_HEADS * RET_DK
    g_off = v_off + RET_HEADS * RET_DV
    ap_off = g_off + RET_HEADS * RET_DV
    ar_off = ap_off + d_model

    @pl.when(t == 0)
    def _():
        ubuf[0:1, :] = jnp.zeros((1, POOL_WIDTH), F32)
        ubuf[1:1 + POOL_HIST, :] = hist_ref[0]
        s_scr[...] = s0_ref[0]

    n_chunks = tile // chunk
    hist_rows = POOL_HIST + 1
    for c in range(n_chunks):
        rows = slice(c * chunk, (c + 1) * chunk)
        xn = _rms(x_ref[0, rows, :], nmw_ref[...]).astype(BF16)
        z_scr[rows, :] = _dot(xn, win_ref[...])
        ubuf[hist_rows + c * chunk:hist_rows + (c + 1) * chunk, :] = z_scr[rows, 0:POOL_WIDTH]

    for c in range(n_chunks):
        rows = slice(c * chunk, (c + 1) * chunk)
        full = ubuf[c * chunk:c * chunk + hist_rows + chunk, :]
        pos = t * tile + c * chunk + lax.broadcasted_iota(I32, (chunk, 1), 0)
        ys = []
        for g, w in enumerate(POOL_WINDOWS):
            f = full[:, g * POOL_GROUP_DIM:(g + 1) * POOL_GROUP_DIM]
            s = f
            shift = 1
            while shift < w:
                s = s + pltpu.roll(s, shift, 0)
                shift *= 2
            cnt = jnp.minimum(w, pos + 1 + n_hist_valid).astype(F32)
            d = s[hist_rows:, :] / cnt - f[hist_rows:, :]
            ys.append(_dot(d.astype(BF16), pwg_ref[g]))
        yp_scr[rows, :] = (jnp.concatenate(ys, axis=-1) * pscale_ref[...]).astype(BF16)
    ubuf[1:1 + POOL_HIST, :] = ubuf[tile + 1:tile + 1 + POOL_HIST, :]

    for c in range(n_chunks):
        rows = slice(c * chunk, (c + 1) * chunk)
        cosc = cos_ref[rows, :]
        sinc = sin_ref[rows, :]
        for h in range(RET_HEADS):
            qh = z_scr[rows, q_off + h * RET_DK:q_off + (h + 1) * RET_DK]
            kh = z_scr[rows, k_off + h * RET_DK:k_off + (h + 1) * RET_DK]
            vb = z_scr[rows, v_off + h * RET_DV:v_off + (h + 1) * RET_DV].astype(BF16)
            qr = qh * cosc + pltpu.roll(qh, RET_DK // 2, 1) * sinc
            kr = (kh * cosc + pltpu.roll(kh, RET_DK // 2, 1) * sinc) * (RET_DK ** -0.5)
            qb = qr.astype(BF16)
            kb = kr.astype(BF16)
            scores = _dot_nt(qb, kb) * dec_ref[h]
            state = s_scr[h]
            o = _dot(scores.astype(BF16), vb) + _dot(qb, state.astype(BF16)) * qd_ref[h]
            s_scr[h] = gc_ref[h] * state + _dot_tn((kr * kd_ref[h]).astype(BF16), vb)
            mu = jnp.mean(o, axis=-1, keepdims=True)
            dlt = o - mu
            var = jnp.mean(dlt * dlt, axis=-1, keepdims=True)
            on = dlt * lax.rsqrt(var + EPS) * gnw_ref[:, h * RET_DV:(h + 1) * RET_DV]
            gate = z_scr[rows, g_off + h * RET_DV:g_off + (h + 1) * RET_DV]
            r_scr[rows, h * RET_DV:(h + 1) * RET_DV] = (on * (gate * jax.nn.sigmoid(gate))).astype(BF16)

    for c in range(n_chunks):
        rows = slice(c * chunk, (c + 1) * chunk)
        branch_pool = _dot(yp_scr[rows, :], wpo_ref[...])
        branch_ret = _dot(r_scr[rows, :], wro_ref[...])
        merged = (jax.nn.sigmoid(z_scr[rows, ap_off:ap_off + d_model]) * branch_pool
                  + jax.nn.sigmoid(z_scr[rows, ar_off:ar_off + d_model]) * branch_ret)
        h_ref[0, rows, :] = x_ref[0, rows, :] + _dot(merged.astype(BF16), wo_ref[...])

    @pl.when(t == n_t - 1)
    def _():
        hist_out_ref[0] = ubuf[1:1 + POOL_HIST, :]
        s_out_ref[0] = s_scr[...]


def _ret_tables(chunk):
    lg = jnp.log(1.0 - 2.0 ** (-5.0 - jnp.arange(RET_HEADS, dtype=F32)))
    idx = jnp.arange(chunk)
    rel = idx[:, None] - idx[None, :]
    decay = jnp.where(rel[None] >= 0, jnp.exp(jnp.maximum(rel, 0)[None].astype(F32) * lg[:, None, None]), 0.0)
    q_decay = jnp.exp((idx + 1).astype(F32)[None, :] * lg[:, None])[:, :, None]
    k_decay = jnp.exp((chunk - 1 - idx).astype(F32)[None, :] * lg[:, None])[:, :, None]
    g_chunk = jnp.exp(chunk * lg)
    return decay, q_decay, k_decay, g_chunk


def _mixer(x, hist, s0, pos0, n_hist_valid, w):
    batch, length, d_model = x.shape
    tile = min(MIXER_TILE, length)
    chunk = min(RET_CHUNK, tile)
    assert length % tile == 0 and tile % chunk == 0 and tile >= POOL_HIST + 1
    in_width = w["w_in"].shape[1]
    cos, sin = _rope_tables(length, pos0)
    decay, q_decay, k_decay, g_chunk = _ret_tables(chunk)
    kern = functools.partial(_mixer_kernel, tile=tile, chunk=chunk, n_hist_valid=n_hist_valid, d_model=d_model)
    return pl.pallas_call(
        kern,
        out_shape=(jax.ShapeDtypeStruct(x.shape, F32),
                   jax.ShapeDtypeStruct(hist.shape, F32),
                   jax.ShapeDtypeStruct(s0.shape, F32)),
        grid=(batch, length // tile),
        in_specs=[
            pl.BlockSpec((1, tile, d_model), lambda b, t: (b, t, 0)),
            pl.BlockSpec((1, POOL_HIST, POOL_WIDTH), lambda b, t: (b, 0, 0)),
            pl.BlockSpec((1, RET_HEADS, RET_DK, RET_DV), lambda b, t: (b, 0, 0, 0)),
            pl.BlockSpec((tile, RET_DK), lambda b, t: (t, 0)),
            pl.BlockSpec((tile, RET_DK), lambda b, t: (t, 0)),
            _const_spec((1, d_model)),
            _const_spec((d_model, in_width)),
            _const_spec(w["pool_w_grp"].shape),
            _const_spec((1, POOL_WIDTH)),
            _const_spec((POOL_WIDTH, d_model)),
            _const_spec((1, RET_HEADS * RET_DV)),
            _const_spec((RET_HEADS * RET_DV, d_model)),
            _const_spec((d_model, d_model)),
            _const_spec(decay.shape),
            _const_spec(q_decay.shape),
            _const_spec(k_decay.shape),
            pl.BlockSpec(memory_space=pltpu.SMEM),
        ],
        out_specs=(
            pl.BlockSpec((1, tile, d_model), lambda b, t: (b, t, 0)),
            pl.BlockSpec((1, POOL_HIST, POOL_WIDTH), lambda b, t: (b, 0, 0)),
            pl.BlockSpec((1, RET_HEADS, RET_DK, RET_DV), lambda b, t: (b, 0, 0, 0)),
        ),
        scratch_shapes=[
            pltpu.VMEM((POOL_HIST + 1 + tile, POOL_WIDTH), F32),
            pltpu.VMEM((RET_HEADS, RET_DK, RET_DV), F32),
            pltpu.VMEM((tile, in_width), F32),
            pltpu.VMEM((tile, RET_HEADS * RET_DV), BF16),
            pltpu.VMEM((tile, POOL_WIDTH), BF16),
        ],
        compiler_params=_cparams(2),
        name="mixer",
    )(x, hist, s0, cos, sin, w["norm_mix_w"], w["w_in"], w["pool_w_grp"], w["pool_scale"], w["w_pool_out"],
      w["ret_gn_w"], w["w_ret_out"], w["w_o"], decay, q_decay, k_decay, g_chunk)


def _mem_kv_kernel(mem_ref, nw_ref, wk_ref, wv_ref, k_ref, v_ref):
    mn = _rms(mem_ref[0], nw_ref[...]).astype(BF16)
    k_ref[0] = _dot(mn, wk_ref[...])
    v_ref[0] = _dot(mn, wv_ref[...])


def _mem_kv(mem, w):
    batch, m_len, d_model = mem.shape
    out = jax.ShapeDtypeStruct((batch, m_len, d_model), F32)
    blk = pl.BlockSpec((1, m_len, d_model), lambda b: (b, 0, 0))
    return pl.pallas_call(
        _mem_kv_kernel,
        out_shape=(out, out),
        grid=(batch,),
        in_specs=[blk, _const_spec((1, d_model)), _const_spec((d_model, d_model)), _const_spec((d_model, d_model))],
        out_specs=(blk, blk),
        compiler_params=_cparams(1),
        name="mem_kv",
    )(mem, w["norm_mem_w"], w["w_mk"], w["w_mv"])


def _cross_router_kernel(h_ref, mk_ref, mv_ref, ncw_ref, wcq_ref, wco_ref, nfw_ref, wrt_ref, brt_ref,
                         h2_ref, xn_ref, lp_ref, gate_ref, cnt_ref,
                         o_scr, *, tile, head_dim, n_experts):
    h1 = h_ref[0]
    hn = _rms(h1, ncw_ref[...]).astype(BF16)
    q = _dot(hn, wcq_ref[...]).astype(BF16)
    for hd in range(MEM_HEADS):
        cols = slice(hd * head_dim, (hd + 1) * head_dim)
        s = _dot_nt(q[:, cols], mk_ref[0, :, cols]) * (head_dim ** -0.5)
        e = jnp.exp(s - jnp.max(s, axis=-1, keepdims=True))
        p = e / jnp.sum(e, axis=-1, keepdims=True)
        o_scr[:, cols] = _dot(p.astype(BF16), mv_ref[0, :, cols]).astype(BF16)
    h2 = h1 + _dot(o_scr[...], wco_ref[...])
    h2_ref[0] = h2
    xn = _rms(h2, nfw_ref[...]).astype(BF16)
    xn_ref[0] = xn

    logits = _dot_nt(wrt_ref[...], xn) + brt_ref[...]
    e_iota = lax.broadcasted_iota(I32, (n_experts, tile), 0)
    work = logits
    vals, sels = [], []
    for _ in range(TOP_K):
        m = jnp.max(work, axis=0, keepdims=True)
        first = jnp.min(jnp.where(work == m, e_iota, n_experts), axis=0, keepdims=True)
        sel = e_iota == first
        vals.append(m)
        sels.append(sel)
        work = jnp.where(sel, -jnp.inf, work)
    exps = [jnp.exp(v - vals[0]) for v in vals]
    denom = exps[0] + exps[1] + exps[2] + exps[3]
    assigned = jnp.zeros((n_experts, tile), F32)
    for sel in sels:
        assigned = assigned + sel.astype(F32)
    r_iota = lax.broadcasted_iota(I32, (tile, tile), 0)
    c_iota = lax.broadcasted_iota(I32, (tile, tile), 1)
    prior = _dot(assigned.astype(BF16), (r_iota < c_iota).astype(BF16))
    counts = jnp.sum(assigned, axis=1, keepdims=True)
    run = jnp.floor((counts + (ROW_UNIT - 1)) * (1.0 / ROW_UNIT)) * ROW_UNIT
    er = lax.broadcasted_iota(I32, (n_experts, n_experts), 0)
    ec = lax.broadcasted_iota(I32, (n_experts, n_experts), 1)
    run_start = _dot((ec < er).astype(BF16), jnp.broadcast_to(run, (n_experts, LANES)).astype(BF16))[:, 0:1]
    for k in range(TOP_K):
        lp_ref[0, k:k + 1, :] = jnp.sum(jnp.where(sels[k], run_start + prior, 0.0), axis=0, keepdims=True).astype(I32)
        gate_ref[0, k:k + 1, :] = exps[k] / denom
    cnt_ref[0, 0] = jnp.broadcast_to(counts, (n_experts, LANES)).astype(I32)


def _cross_router(h1, mk, mv, w):
    batch, length, d_model = h1.shape
    m_len = mk.shape[1]
    head_dim = d_model // MEM_HEADS
    n_experts = w["w_router_t"].shape[0]
    tile = min(CROSS_TILE, length)
    assert length % tile == 0
    n_t = length // tile
    kern = functools.partial(_cross_router_kernel, tile=tile, head_dim=head_dim, n_experts=n_experts)
    tok = pl.BlockSpec((1, tile, d_model), lambda b, t: (b, t, 0))
    mem = pl.BlockSpec((1, m_len, d_model), lambda b, t: (b, 0, 0))
    small = pl.BlockSpec((1, TOP_K, tile), lambda b, t: (b, 0, t))
    return pl.pallas_call(
        kern,
        out_shape=(jax.ShapeDtypeStruct(h1.shape, F32),
                   jax.ShapeDtypeStruct(h1.shape, BF16),
                   jax.ShapeDtypeStruct((batch, TOP_K, length), I32),
                   jax.ShapeDtypeStruct((batch, TOP_K, length), F32),
                   jax.ShapeDtypeStruct((batch, n_t, n_experts, LANES), I32)),
        grid=(batch, n_t),
        in_specs=[tok, mem, mem,
                  _const_spec((1, d_model)), _const_spec((d_model, d_model)), _const_spec((d_model, d_model)),
                  _const_spec((1, d_model)), _const_spec((n_experts, d_model)), _const_spec((n_experts, 1))],
        out_specs=(tok, tok, small, small,
                   pl.BlockSpec((1, 1, n_experts, LANES), lambda b, t: (b, t, 0, 0))),
        scratch_shapes=[pltpu.VMEM((tile, d_model), BF16)],
        compiler_params=_cparams(2),
        name="cross_router",
    )(h1, mk, mv, w["norm_cross_w"], w["w_cq"], w["w_co"], w["norm_ffn_w"], w["w_router_t"], w["b_router_c"])


def _unit_copy(src_ref, dst_ref, src_unit, dst_unit, sem):
    def rows_of(unit):
        start = unit * ROW_UNIT
        return pl.ds(start if isinstance(start, int) else pl.multiple_of(start, ROW_UNIT), ROW_UNIT)

    return pltpu.make_async_copy(src_ref.at[rows_of(src_unit)], dst_ref.at[rows_of(dst_unit)], sem)


def _start_units(n, copy_of):
    def pair(p, c):
        copy_of(2 * p).start(priority=0)
        copy_of(2 * p + 1).start(priority=1)
        return c

    lax.fori_loop(0, lax.shift_right_logical(n, 1), pair, 0)

    @pl.when(lax.rem(n, 2) == 1)
    def _():
        copy_of(n - 1).start(priority=0)


def _wait_units(n, copy_of):
    def one(j, c):
        copy_of(j).wait()
        return c

    lax.fori_loop(0, n, one, 0)


def _pack_pairs(a, b):
    ua = lax.bitcast_convert_type(a, U32)
    ub = lax.bitcast_convert_type(b, U32)
    return lax.shift_right_logical(ua, U32(16)) | (ub & U32(HIGH_HALF))


def _unpack_pairs(u):
    a = lax.bitcast_convert_type(lax.shift_left(u, U32(16)), F32)
    b = lax.bitcast_convert_type(u & U32(HIGH_HALF), F32)
    return a, b


def _dispatch_kernel(nun_ref, tail_ref, ntail_ref, lp_ref, gate_ref, xn_ref, dst_ref, dstp_ref, xs_ref,
                     rows, zero_buf, sems, tail_sem, *, tile, n_local, d_model):
    i = pl.program_id(0)
    n_i = pl.num_programs(0)
    slot = lax.rem(i, 2)

    @pl.when(i == 0)
    def _():
        zero_buf[...] = jnp.zeros(zero_buf.shape, U32)

        def tail_copy(j):
            return _unit_copy(zero_buf, xs_ref, 0, tail_ref[j], tail_sem)

        _start_units(ntail_ref[0], tail_copy)
        _wait_units(ntail_ref[0], tail_copy)

    half = d_model // 2
    xn = xn_ref[0]
    lps = [lp_ref[0, k:k + 1, :] for k in range(TOP_K)]
    gates = [gate_ref[0, k:k + 1, :] for k in range(TOP_K)]
    for c in range(n_local // PERM_CHUNK):
        r_iota = lax.broadcasted_iota(I32, (PERM_CHUNK, tile), 0) + c * PERM_CHUNK
        hit = r_iota == lps[0]
        wmat = jnp.where(hit, gates[0], 0.0)
        for k in range(1, TOP_K):
            hk = r_iota == lps[k]
            hit = hit | hk
            wmat = wmat + jnp.where(hk, gates[k], 0.0)
        perm = jnp.where(hit, 1.0, 0.0).astype(BF16)
        rs = pl.ds(c * PERM_CHUNK, PERM_CHUNK)
        xr = _dot(perm, xn)
        rows[slot, rs, 0:half] = _pack_pairs(xr[:, 0:half], xr[:, half:d_model])
        rows[slot, rs, half:half + LANES] = lax.bitcast_convert_type(
            jnp.broadcast_to(jnp.sum(wmat, axis=1, keepdims=True), (PERM_CHUNK, LANES)), U32)

    _start_units(nun_ref[i], lambda j: _unit_copy(rows.at[slot], xs_ref, j, dst_ref[0, 0, j], sems.at[slot]))

    @pl.when(i > 0)
    def _():
        _wait_units(nun_ref[i - 1],
                    lambda j: _unit_copy(rows.at[1 - slot], xs_ref, j, dstp_ref[0, 0, j], sems.at[1 - slot]))

    @pl.when(i == n_i - 1)
    def _():
        _wait_units(nun_ref[i], lambda j: _unit_copy(rows.at[slot], xs_ref, j, dst_ref[0, 0, j], sems.at[slot]))


def _dispatch(lp, gate, xn, plan, n_rows):
    batch, length, d_model = xn.shape
    tile, n_local, n_units = plan["tile"], plan["n_local"], plan["n_units"]
    n_t = length // tile
    width = d_model // 2 + LANES
    small = pl.BlockSpec((1, TOP_K, tile), lambda i, *_: (i // n_t, 0, i % n_t))
    grid_spec = pltpu.PrefetchScalarGridSpec(
        num_scalar_prefetch=3,
        grid=(batch * n_t,),
        in_specs=[small, small,
                  pl.BlockSpec((1, tile, d_model), lambda i, *_: (i // n_t, i % n_t, 0)),
                  pl.BlockSpec((1, 1, n_units), lambda i, *_: (i, 0, 0), memory_space=pltpu.SMEM),
                  pl.BlockSpec((1, 1, n_units), lambda i, *_: (jnp.maximum(i - 1, 0), 0, 0),
                               memory_space=pltpu.SMEM)],
        out_specs=pl.BlockSpec(memory_space=pl.ANY),
        scratch_shapes=[pltpu.VMEM((2, n_local, width), U32), pltpu.VMEM((ROW_UNIT, width), U32),
                        pltpu.SemaphoreType.DMA((2,)), pltpu.SemaphoreType.DMA(())],
    )
    return pl.pallas_call(
        functools.partial(_dispatch_kernel, tile=tile, n_local=n_local, d_model=d_model),
        out_shape=jax.ShapeDtypeStruct((n_rows, width), U32),
        grid_spec=grid_spec,
        compiler_params=_cparams(1),
        name="dispatch",
    )(plan["units_per_tile"], plan["tail_units"], plan["n_tail"], lp, gate, xn, plan["dst_units"], plan["dst_units"])


def _expert_kernel(be_ref, nb_ref, xs_ref, wgu_ref, bgu_ref, wd_ref, bd_ref, ys_ref, *, d_model, d_expert):
    del be_ref

    @pl.when(pl.program_id(0) < nb_ref[0])
    def _():
        half = d_model // 2
        xa, xb = _unpack_pairs(xs_ref[:, 0:half])
        row_gate = lax.bitcast_convert_type(xs_ref[:, half:half + 1], F32)
        gu = _dot(jnp.concatenate([xa, xb], axis=-1).astype(BF16), wgu_ref[0]) + bgu_ref[0]
        gl = jnp.minimum(gu[:, :d_expert], SWIGLU_LIMIT)
        up = jnp.clip(gu[:, d_expert:], -SWIGLU_LIMIT, SWIGLU_LIMIT)
        act = (gl * jax.nn.sigmoid(SWIGLU_ALPHA * gl) * (up + 1.0)).astype(BF16)
        y = ((_dot(act, wd_ref[0]) + bd_ref[0]) * row_gate).astype(BF16).astype(F32)
        ys_ref[...] = _pack_pairs(y[:, 0:half], y[:, half:d_model])


def _experts(xs, block_e, n_used, w):
    n_rows, width = xs.shape
    d_model = 2 * (width - LANES)
    blk = EXPERT_BLOCK
    d_expert = w["w_down"].shape[1]

    def row_map(b, be, nb):
        return (jnp.minimum(b, nb[0] - 1), 0)

    def exp_map(b, be, nb):
        return (be[b], 0, 0)

    grid_spec = pltpu.PrefetchScalarGridSpec(
        num_scalar_prefetch=2,
        grid=(n_rows // blk,),
        in_specs=[pl.BlockSpec((blk, width), row_map),
                  pl.BlockSpec((1, d_model, 2 * d_expert), exp_map),
                  pl.BlockSpec((1, 1, 2 * d_expert), exp_map),
                  pl.BlockSpec((1, d_expert, d_model), exp_map),
                  pl.BlockSpec((1, 1, d_model), exp_map)],
        out_specs=pl.BlockSpec((blk, d_model // 2), row_map),
    )
    return pl.pallas_call(
        functools.partial(_expert_kernel, d_model=d_model, d_expert=d_expert),
        out_shape=jax.ShapeDtypeStruct((n_rows, d_model // 2), U32),
        grid_spec=grid_spec,
        compiler_params=_cparams(1),
        name="experts",
    )(block_e, n_used, xs, w["w_gu"], w["b_gu"], w["w_down"], w["b_down"])


def _combine_kernel(nun_ref, lpc_ref, h2_ref, nw_ref, src_ref, srcn_ref, ys_ref, out_ref, rows, sems,
                    *, tile, n_local, final_norm):
    i = pl.program_id(0)
    n_i = pl.num_programs(0)
    slot = lax.rem(i, 2)

    def fetch(step, units_ref, to_slot):
        _start_units(nun_ref[step],
                     lambda j: _unit_copy(ys_ref, rows.at[to_slot], units_ref[0, 0, j], j, sems.at[to_slot]))

    @pl.when(i == 0)
    def _():
        rows[...] = jnp.zeros(rows.shape, U32)
        fetch(0, src_ref, 0)

    @pl.when(i + 1 < n_i)
    def _():
        fetch(i + 1, srcn_ref, 1 - slot)

    _wait_units(nun_ref[i], lambda j: _unit_copy(ys_ref, rows.at[slot], src_ref[0, 0, j], j, sems.at[slot]))

    lps = [lpc_ref[:, k:k + 1] for k in range(TOP_K)]
    half = out_ref.shape[1] // 2
    ya = jnp.zeros((tile, half), F32)
    yb = jnp.zeros((tile, half), F32)
    for c in range(n_local // PERM_CHUNK):
        c_iota = lax.broadcasted_iota(I32, (tile, PERM_CHUNK), 1) + c * PERM_CHUNK
        hit = c_iota == lps[0]
        for k in range(1, TOP_K):
            hit = hit | (c_iota == lps[k])
        sel = jnp.where(hit, 1.0, 0.0).astype(BF16)
        ra, rb = _unpack_pairs(rows[slot, pl.ds(c * PERM_CHUNK, PERM_CHUNK), :])
        ya = ya + _dot(sel, ra.astype(BF16))
        yb = yb + _dot(sel, rb.astype(BF16))
    h3 = h2_ref[...] + jnp.concatenate([ya, yb], axis=-1)
    out_ref[...] = _rms(h3, nw_ref[...]) if final_norm else h3


def _combine(lp_cols, h2, ys, norm_w, plan, final_norm):
    n_tok, d_model = h2.shape
    tile, n_local, n_units = plan["tile"], plan["n_local"], plan["n_units"]
    n_tiles = n_tok // tile
    grid_spec = pltpu.PrefetchScalarGridSpec(
        num_scalar_prefetch=1,
        grid=(n_tiles,),
        in_specs=[pl.BlockSpec((tile, TOP_K), lambda i, *_: (i, 0)),
                  pl.BlockSpec((tile, d_model), lambda i, *_: (i, 0)),
                  _const_spec((1, d_model)),
                  pl.BlockSpec((1, 1, n_units), lambda i, *_: (i, 0, 0), memory_space=pltpu.SMEM),
                  pl.BlockSpec((1, 1, n_units), lambda i, *_: (jnp.minimum(i + 1, n_tiles - 1), 0, 0),
                               memory_space=pltpu.SMEM),
                  pl.BlockSpec(memory_space=pl.ANY)],
        out_specs=pl.BlockSpec((tile, d_model), lambda i, *_: (i, 0)),
        scratch_shapes=[pltpu.VMEM((2, n_local, d_model // 2), U32), pltpu.SemaphoreType.DMA((2,))],
    )
    return pl.pallas_call(
        functools.partial(_combine_kernel, tile=tile, n_local=n_local, final_norm=final_norm),
        out_shape=jax.ShapeDtypeStruct((n_tok, d_model), F32),
        grid_spec=grid_spec,
        compiler_params=_cparams(1),
        name="combine",
    )(plan["units_per_tile"], lp_cols, h2, norm_w, plan["dst_units"], plan["dst_units"], ys)


def _moe_plan(cnt, tile):
    n_experts = cnt.shape[2]
    blk = EXPERT_BLOCK
    counts = cnt[..., 0].reshape(-1, n_experts)
    n_tiles = counts.shape[0]
    run_units = (counts + ROW_UNIT - 1) // ROW_UNIT
    unit_end = jnp.cumsum(run_units, axis=1)
    expert_units = jnp.sum(run_units, axis=0)
    blk_units = blk // ROW_UNIT
    padded_units = (expert_units + blk_units - 1) // blk_units * blk_units
    expert_end = jnp.cumsum(padded_units)
    expert_start = expert_end - padded_units
    run_base = expert_start[None, :] + jnp.cumsum(run_units, axis=0) - run_units
    n_local = _round_up(TOP_K * tile + (ROW_UNIT - 1) * n_experts, PERM_CHUNK)
    n_units = n_local // ROW_UNIT
    j = jnp.arange(n_units)
    shift = run_base - (unit_end - run_units)
    step = shift[:, 1:] - shift[:, :-1]
    past = j[None, :, None] >= unit_end[:, None, :-1]
    dst_units = j[None, :] + shift[:, 0:1] + jnp.sum(jnp.where(past, step[:, None, :], 0), axis=-1)
    units_per_tile = unit_end[:, -1]
    dst_units = jnp.where(j[None, :] < units_per_tile[:, None], dst_units, 0)
    tj = jnp.arange(n_experts * (blk_units - 1))
    tail_len = padded_units - expert_units
    tail_end = jnp.cumsum(tail_len)
    tshift = expert_start + expert_units - (tail_end - tail_len)
    tpast = tj[:, None] >= tail_end[None, :-1]
    tail_units = tj + tshift[0] + jnp.sum(jnp.where(tpast, (tshift[1:] - tshift[:-1])[None, :], 0), axis=-1)
    n_tail = tail_end[-1]
    tail_units = jnp.where(tj < n_tail, tail_units, 0)
    n_blocks = (n_tiles * tile * TOP_K + n_tiles * n_experts * (ROW_UNIT - 1) + n_experts * (blk - 1)) // blk
    block_e = jnp.minimum(jnp.sum((jnp.arange(n_blocks) * blk_units)[:, None] >= expert_end[None, :], axis=-1),
                          n_experts - 1)
    return {
        "tile": tile, "n_local": n_local, "n_units": n_units, "n_rows": n_blocks * blk,
        "units_per_tile": units_per_tile.astype(I32),
        "dst_units": dst_units.astype(I32).reshape(n_tiles, 1, n_units),
        "tail_units": tail_units.astype(I32), "n_tail": n_tail.astype(I32).reshape(1),
        "block_e": block_e.astype(I32), "n_used": (expert_end[-1] // blk_units).astype(I32).reshape(1),
    }


def _moe_tail(h2, xn, lp, gate, cnt, w, norm_w, final_norm):
    batch, length, d_model = h2.shape
    n_tok = batch * length
    tile = length // cnt.shape[1]
    plan = _moe_plan(cnt, tile)
    xs = _dispatch(lp, gate, xn, plan, plan["n_rows"])
    ys = _experts(xs, plan["block_e"], plan["n_used"], w)
    lp_cols = lp.transpose(0, 2, 1).reshape(n_tok, TOP_K)
    out = _combine(lp_cols, h2.reshape(n_tok, d_model), ys, norm_w, plan, final_norm)
    return out.reshape(batch, length, d_model)


def _layer(x, hist, s0, pos0, n_hist_valid, mk, mv, w, norm_w, final_norm):
    h1, new_hist, s_new = _mixer(x, hist, s0, pos0, n_hist_valid, w)
    h2, xn, lp, gate, cnt = _cross_router(h1, mk.astype(BF16), mv.astype(BF16), w)
    out = _moe_tail(h2, xn, lp, gate, cnt, w, norm_w, final_norm)
    return out, new_hist, s_new


def kernel(x_prompt, x_sample, mem_prompt, state_pool, state_ret, cache_mem_k, cache_mem_v, norm_mix_w, w_in, pool_w_grp, pool_scale, w_pool_out, ret_gn_w, w_ret_out, w_o, norm_mem_w, w_mk, w_mv, norm_cross_w, w_cq, w_co, norm_ffn_w, w_router, b_router, w_gu, b_gu, w_down, b_down, norm_final_w):
    depth = w_in.shape[0]
    batch_p = x_prompt.shape[0]
    m_len = mem_prompt.shape[1]
    d_model = x_prompt.shape[-1]
    hp, hs = x_prompt, x_sample
    norm_w = norm_final_w[None, :]
    mem_k_p, mem_v_p, pool_p, ret_p, pool_s, ret_s = [], [], [], [], [], []
    for l in range(depth):
        w = {
            "norm_mix_w": norm_mix_w[l][None, :], "w_in": w_in[l].astype(BF16),
            "pool_w_grp": pool_w_grp[l].astype(BF16), "pool_scale": pool_scale[l][None, :],
            "w_pool_out": w_pool_out[l].astype(BF16), "ret_gn_w": ret_gn_w[l][None, :],
            "w_ret_out": w_ret_out[l].astype(BF16), "w_o": w_o[l].astype(BF16),
            "norm_mem_w": norm_mem_w[l][None, :], "w_mk": w_mk[l].astype(BF16), "w_mv": w_mv[l].astype(BF16),
            "norm_cross_w": norm_cross_w[l][None, :], "w_cq": w_cq[l].astype(BF16), "w_co": w_co[l].astype(BF16),
            "norm_ffn_w": norm_ffn_w[l][None, :], "w_router_t": w_router[l].T.astype(BF16),
            "b_router_c": b_router[l][:, None],
            "w_gu": w_gu[l].astype(BF16), "b_gu": b_gu[l][:, None, :],
            "w_down": w_down[l].astype(BF16), "b_down": b_down[l][:, None, :],
        }
        last = l == depth - 1
        mk, mv = _mem_kv(mem_prompt, w)
        zero_hist = jnp.zeros((batch_p, POOL_HIST, POOL_WIDTH), F32)
        zero_state = jnp.zeros((batch_p, RET_HEADS, RET_DK, RET_DV), F32)
        hp, hist_p, s_p = _layer(hp, zero_hist, zero_state, 0, 0, mk, mv, w, norm_w, last)
        mem_k_p.append(mk.reshape(batch_p, m_len, MEM_HEADS, d_model // MEM_HEADS))
        mem_v_p.append(mv.reshape(batch_p, m_len, MEM_HEADS, d_model // MEM_HEADS))
        pool_p.append(hist_p)
        ret_p.append(s_p)
        ck = cache_mem_k[l].reshape(cache_mem_k.shape[1], m_len, d_model)
        cv = cache_mem_v[l].reshape(cache_mem_v.shape[1], m_len, d_model)
        hs, hist_s, s_s = _layer(hs, state_pool[l], state_ret[l], PAST_LEN, POOL_HIST, ck, cv, w, norm_w, last)
        pool_s.append(hist_s)
        ret_s.append(s_s)
    return (hp, hs, jnp.stack(mem_k_p), jnp.stack(mem_v_p), jnp.stack(pool_p), jnp.stack(ret_p),
            jnp.stack(pool_s), jnp.stack(ret_s))
```

```python
import functools

import jax
import jax.numpy as jnp
from jax import lax
from jax.experimental import pallas as pl
from jax.experimental.pallas import tpu as pltpu

F32 = jnp.float32
BF16 = jnp.bfloat16
I32 = jnp.int32
U32 = jnp.uint32
HIGH_HALF = 0xFFFF0000

EPS = 1e-6
PAST_LEN = 1024
POOL_WINDOWS = (2, 4, 8, 16)
POOL_GROUP_DIM = 128
POOL_WIDTH = POOL_GROUP_DIM * len(POOL_WINDOWS)
POOL_HIST = max(POOL_WINDOWS) - 1
RET_HEADS = 4
RET_DK = 128
RET_DV = 256
ROPE_BASE = 10000.0
MEM_HEADS = 4
TOP_K = 4
SWIGLU_LIMIT = 7.0
SWIGLU_ALPHA = 1.702

SUBLANES = 8
LANES = 128
VMEM_LIMIT_BYTES = 56 * 1024 * 1024

MIXER_TILE = 512
RET_CHUNK = 256
CROSS_TILE = 512
CROSS_CHUNK = 512
EXPERT_BLOCK = 512
ROW_UNIT = SUBLANES
PERM_CHUNK = 256
BIG_UNITS = 4
DMA_UNROLL = 4


def _cparams(n_axes):
    return pltpu.CompilerParams(dimension_semantics=("arbitrary",) * n_axes,
                                vmem_limit_bytes=VMEM_LIMIT_BYTES)


def _const_spec(shape):
    nd = len(shape)
    return pl.BlockSpec(shape, lambda *_: (0,) * nd, pipeline_mode=pl.Buffered(1))


def _round_up(n, m):
    return (n + m - 1) // m * m


def _rms(x32, w_row):
    ms = jnp.mean(x32 * x32, axis=-1, keepdims=True)
    return x32 * lax.rsqrt(ms + EPS) * w_row


def _dot(a, b):
    return jnp.dot(a, b, preferred_element_type=F32)


def _dot_nt(a, b):
    return lax.dot_general(a, b, (((1,), (1,)), ((), ())), preferred_element_type=F32)


def _dot_tn(a, b):
    return lax.dot_general(a, b, (((0,), (0,)), ((), ())), preferred_element_type=F32)


def _rope_kernel(inv_ref, sign_ref, cos_ref, sin_ref, *, pos0, tile):
    i = pl.program_id(0)
    pos = (lax.broadcasted_iota(I32, (tile, RET_DK), 0) + (pos0 + i * tile)).astype(F32)
    ang = pos * inv_ref[...]
    cos_ref[...] = jnp.cos(ang)
    sin_ref[...] = jnp.sin(ang) * sign_ref[...]


def _rope_tables(length, pos0):
    half = RET_DK // 2
    inv = 1.0 / (ROPE_BASE ** (jnp.arange(half, dtype=F32) / half))
    inv2 = jnp.concatenate([inv, inv])[None, :]
    sign = jnp.concatenate([-jnp.ones((half,), F32), jnp.ones((half,), F32)])[None, :]
    tile = min(length, 512)
    assert length % tile == 0
    return pl.pallas_call(
        functools.partial(_rope_kernel, pos0=pos0, tile=tile),
        out_shape=(jax.ShapeDtypeStruct((length, RET_DK), F32),) * 2,
        grid=(length // tile,),
        in_specs=[pl.BlockSpec((1, RET_DK), lambda i: (0, 0))] * 2,
        out_specs=(pl.BlockSpec((tile, RET_DK), lambda i: (i, 0)),) * 2,
        compiler_params=_cparams(1),
        name="rope_tables",
    )(inv2, sign)


def _mixer_kernel(x_ref, hist_ref, s0_ref, cos_ref, sin_ref, nmw_ref, win_ref, pwg_ref, pscale_ref,
                  wpo_ref, gnw_ref, wro_ref, wo_ref, dec_ref, qd_ref, kd_ref, gc_ref,
                  h_ref, hist_out_ref, s_out_ref,
                  ubuf, s_scr, z_scr, r_scr, yp_scr, *, tile, chunk, n_hist_valid, d_model):
    t = pl.program_id(1)
    n_t = pl.num_programs(1)
    q_off = POOL_WIDTH
    k_off = q_off + RET_HEADS * RET_DK
    v_off = k_off + RET_HEADS * RET_DK
    g_off = v_off + RET_HEADS * RET_DV
    ap_off = g_off + RET_HEADS * RET_DV
    ar_off = ap_off + d_model

    @pl.when(t == 0)
    def _():
        ubuf[0:1, :] = jnp.zeros((1, POOL_WIDTH), F32)
        ubuf[1:1 + POOL_HIST, :] = hist_ref[0]
        s_scr[...] = s0_ref[0]

    n_chunks = tile // chunk
    hist_rows = POOL_HIST + 1
    for c in range(n_chunks):
        rows = slice(c * chunk, (c + 1) * chunk)
        xn = _rms(x_ref[0, rows, :], nmw_ref[...]).astype(BF16)
        z_scr[rows, :] = _dot(xn, win_ref[...])
        ubuf[hist_rows + c * chunk:hist_rows + (c + 1) * chunk, :] = z_scr[rows, 0:POOL_WIDTH]

    for c in range(n_chunks):
        rows = slice(c * chunk, (c + 1) * chunk)
        full = ubuf[c * chunk:c * chunk + hist_rows + chunk, :]
        pos = t * tile + c * chunk + lax.broadcasted_iota(I32, (chunk, 1), 0)
        ys = []
        for g, w in enumerate(POOL_WINDOWS):
            f = full[:, g * POOL_GROUP_DIM:(g + 1) * POOL_GROUP_DIM]
            s = f
            shift = 1
            while shift < w:
                s = s + pltpu.roll(s, shift, 0)
                shift *= 2
            cnt = jnp.minimum(w, pos + 1 + n_hist_valid).astype(F32)
            d = s[hist_rows:, :] / cnt - f[hist_rows:, :]
            ys.append(_dot(d.astype(BF16), pwg_ref[g]))
        yp_scr[rows, :] = (jnp.concatenate(ys, axis=-1) * pscale_ref[...]).astype(BF16)
    ubuf[1:1 + POOL_HIST, :] = ubuf[tile + 1:tile + 1 + POOL_HIST, :]

    for c in range(n_chunks):
        rows = slice(c * chunk, (c + 1) * chunk)
        cosc = cos_ref[rows, :]
        sinc = sin_ref[rows, :]
        for h in range(RET_HEADS):
            qh = z_scr[rows, q_off + h * RET_DK:q_off + (h + 1) * RET_DK]
            kh = z_scr[rows, k_off + h * RET_DK:k_off + (h + 1) * RET_DK]
            vb = z_scr[rows, v_off + h * RET_DV:v_off + (h + 1) * RET_DV].astype(BF16)
            qr = qh * cosc + pltpu.roll(qh, RET_DK // 2, 1) * sinc
            kr = (kh * cosc + pltpu.roll(kh, RET_DK // 2, 1) * sinc) * (RET_DK ** -0.5)
            qb = qr.astype(BF16)
            kb = kr.astype(BF16)
            scores = _dot_nt(qb, kb) * dec_ref[h]
            state = s_scr[h]
            o = _dot(scores.astype(BF16), vb) + _dot(qb, state.astype(BF16)) * qd_ref[h]
            s_scr[h] = gc_ref[h] * state + _dot_tn((kr * kd_ref[h]).astype(BF16), vb)
            mu = jnp.mean(o, axis=-1, keepdims=True)
            dlt = o - mu
            var = jnp.mean(dlt * dlt, axis=-1, keepdims=True)
            on = dlt * lax.rsqrt(var + EPS) * gnw_ref[:, h * RET_DV:(h + 1) * RET_DV]
            gate = z_scr[rows, g_off + h * RET_DV:g_off + (h + 1) * RET_DV]
            r_scr[rows, h * RET_DV:(h + 1) * RET_DV] = (on * (gate * jax.nn.sigmoid(gate))).astype(BF16)

    for c in range(n_chunks):
        rows = slice(c * chunk, (c + 1) * chunk)
        branch_pool = _dot(yp_scr[rows, :], wpo_ref[...])
        branch_ret = _dot(r_scr[rows, :], wro_ref[...])
        merged = (jax.nn.sigmoid(z_scr[rows, ap_off:ap_off + d_model]) * branch_pool
                  + jax.nn.sigmoid(z_scr[rows, ar_off:ar_off + d_model]) * branch_ret)
        h_ref[0, rows, :] = x_ref[0, rows, :] + _dot(merged.astype(BF16), wo_ref[...])

    @pl.when(t == n_t - 1)
    def _():
        hist_out_ref[0] = ubuf[1:1 + POOL_HIST, :]
        s_out_ref[0] = s_scr[...]


def _ret_tables(chunk):
    lg = jnp.log(1.0 - 2.0 ** (-5.0 - jnp.arange(RET_HEADS, dtype=F32)))
    idx = jnp.arange(chunk)
    rel = idx[:, None] - idx[None, :]
    decay = jnp.where(rel[None] >= 0, jnp.exp(jnp.maximum(rel, 0)[None].astype(F32) * lg[:, None, None]), 0.0)
    q_decay = jnp.exp((idx + 1).astype(F32)[None, :] * lg[:, None])[:, :, None]
    k_decay = jnp.exp((chunk - 1 - idx).astype(F32)[None, :] * lg[:, None])[:, :, None]
    g_chunk = jnp.exp(chunk * lg)
    return decay, q_decay, k_decay, g_chunk


def _mixer(x, hist, s0, pos0, n_hist_valid, w):
    batch, length, d_model = x.shape
    tile = min(MIXER_TILE, length)
    chunk = min(RET_CHUNK, tile)
    assert length % tile == 0 and tile % chunk == 0 and tile >= POOL_HIST + 1
    in_width = w["w_in"].shape[1]
    cos, sin = _rope_tables(length, pos0)
    decay, q_decay, k_decay, g_chunk = _ret_tables(chunk)
    kern = functools.partial(_mixer_kernel, tile=tile, chunk=chunk, n_hist_valid=n_hist_valid, d_model=d_model)
    return pl.pallas_call(
        kern,
        out_shape=(jax.ShapeDtypeStruct(x.shape, F32),
                   jax.ShapeDtypeStruct(hist.shape, F32),
                   jax.ShapeDtypeStruct(s0.shape, F32)),
        grid=(batch, length // tile),
        in_specs=[
            pl.BlockSpec((1, tile, d_model), lambda b, t: (b, t, 0)),
            pl.BlockSpec((1, POOL_HIST, POOL_WIDTH), lambda b, t: (b, 0, 0)),
            pl.BlockSpec((1, RET_HEADS, RET_DK, RET_DV), lambda b, t: (b, 0, 0, 0)),
            pl.BlockSpec((tile, RET_DK), lambda b, t: (t, 0)),
            pl.BlockSpec((tile, RET_DK), lambda b, t: (t, 0)),
            _const_spec((1, d_model)),
            _const_spec((d_model, in_width)),
            _const_spec(w["pool_w_grp"].shape),
            _const_spec((1, POOL_WIDTH)),
            _const_spec((POOL_WIDTH, d_model)),
            _const_spec((1, RET_HEADS * RET_DV)),
            _const_spec((RET_HEADS * RET_DV, d_model)),
            _const_spec((d_model, d_model)),
            _const_spec(decay.shape),
            _const_spec(q_decay.shape),
            _const_spec(k_decay.shape),
            pl.BlockSpec(memory_space=pltpu.SMEM),
        ],
        out_specs=(
            pl.BlockSpec((1, tile, d_model), lambda b, t: (b, t, 0)),
            pl.BlockSpec((1, POOL_HIST, POOL_WIDTH), lambda b, t: (b, 0, 0)),
            pl.BlockSpec((1, RET_HEADS, RET_DK, RET_DV), lambda b, t: (b, 0, 0, 0)),
        ),
        scratch_shapes=[
            pltpu.VMEM((POOL_HIST + 1 + tile, POOL_WIDTH), F32),
            pltpu.VMEM((RET_HEADS, RET_DK, RET_DV), F32),
            pltpu.VMEM((tile, in_width), F32),
            pltpu.VMEM((tile, RET_HEADS * RET_DV), BF16),
            pltpu.VMEM((tile, POOL_WIDTH), BF16),
        ],
        compiler_params=_cparams(2),
        name="mixer",
    )(x, hist, s0, cos, sin, w["norm_mix_w"], w["w_in"], w["pool_w_grp"], w["pool_scale"], w["w_pool_out"],
      w["ret_gn_w"], w["w_ret_out"], w["w_o"], decay, q_decay, k_decay, g_chunk)


def _mem_kv_kernel(mem_ref, nw_ref, wk_ref, wv_ref, k_ref, v_ref):
    mn = _rms(mem_ref[0], nw_ref[...]).astype(BF16)
    k_ref[0] = _dot(mn, wk_ref[...])
    v_ref[0] = _dot(mn, wv_ref[...])


def _mem_kv(mem, w):
    batch, m_len, d_model = mem.shape
    out = jax.ShapeDtypeStruct((batch, m_len, d_model), F32)
    blk = pl.BlockSpec((1, m_len, d_model), lambda b: (b, 0, 0))
    return pl.pallas_call(
        _mem_kv_kernel,
        out_shape=(out, out),
        grid=(batch,),
        in_specs=[blk, _const_spec((1, d_model)), _const_spec((d_model, d_model)), _const_spec((d_model, d_model))],
        out_specs=(blk, blk),
        compiler_params=_cparams(1),
        name="mem_kv",
    )(mem, w["norm_mem_w"], w["w_mk"], w["w_mv"])


def _cross_router_kernel(h_ref, mk_ref, mv_ref, ncw_ref, wcq_ref, wco_ref, nfw_ref, wrt_ref, brt_ref,
                         h2_ref, xn_ref, lp_ref, gate_ref, cnt_ref,
                         o_scr, *, tile, chunk, head_dim, n_experts):
    e_iota = lax.broadcasted_iota(I32, (n_experts, chunk), 0)
    r_iota = lax.broadcasted_iota(I32, (chunk, chunk), 0)
    c_iota = lax.broadcasted_iota(I32, (chunk, chunk), 1)
    earlier = (r_iota < c_iota).astype(BF16)
    counts = jnp.zeros((n_experts, 1), F32)
    chunk_sels, chunk_prior = [], []
    for c in range(tile // chunk):
        rows = slice(c * chunk, (c + 1) * chunk)
        h1 = h_ref[0, rows, :]
        hn = _rms(h1, ncw_ref[...]).astype(BF16)
        q = _dot(hn, wcq_ref[...]).astype(BF16)
        for hd in range(MEM_HEADS):
            cols = slice(hd * head_dim, (hd + 1) * head_dim)
            s = _dot_nt(q[:, cols], mk_ref[0, :, cols]) * (head_dim ** -0.5)
            e = jnp.exp(s - jnp.max(s, axis=-1, keepdims=True))
            p = e / jnp.sum(e, axis=-1, keepdims=True)
            o_scr[rows, cols] = _dot(p.astype(BF16), mv_ref[0, :, cols]).astype(BF16)
        h2 = h1 + _dot(o_scr[rows, :], wco_ref[...])
        h2_ref[0, rows, :] = h2
        xn_ref[0, rows, :] = _rms(h2, nfw_ref[...]).astype(BF16)

    for c in range(tile // chunk):
        rows = slice(c * chunk, (c + 1) * chunk)
        work = _dot_nt(wrt_ref[...], xn_ref[0, rows, :]) + brt_ref[...]
        vals, sels = [], []
        for _ in range(TOP_K):
            m = jnp.max(work, axis=0, keepdims=True)
            first = jnp.min(jnp.where(work == m, e_iota, n_experts), axis=0, keepdims=True)
            sel = e_iota == first
            vals.append(m)
            sels.append(sel)
            work = jnp.where(sel, -jnp.inf, work)
        exps = [jnp.exp(v - vals[0]) for v in vals]
        denom = exps[0] + exps[1] + exps[2] + exps[3]
        for k in range(TOP_K):
            gate_ref[0, k:k + 1, rows] = exps[k] / denom
        assigned = jnp.zeros((n_experts, chunk), F32)
        for sel in sels:
            assigned = assigned + sel.astype(F32)
        chunk_prior.append(counts + _dot(assigned.astype(BF16), earlier))
        chunk_sels.append(sels)
        counts = counts + jnp.sum(assigned, axis=1, keepdims=True)

    run = jnp.floor((counts + (ROW_UNIT - 1)) * (1.0 / ROW_UNIT)) * ROW_UNIT
    er = lax.broadcasted_iota(I32, (n_experts, n_experts), 0)
    ec = lax.broadcasted_iota(I32, (n_experts, n_experts), 1)
    run_start = _dot((ec < er).astype(BF16), jnp.broadcast_to(run, (n_experts, LANES)).astype(BF16))[:, 0:1]
    for c in range(tile // chunk):
        rows = slice(c * chunk, (c + 1) * chunk)
        place = run_start + chunk_prior[c]
        for k in range(TOP_K):
            lp_ref[0, k:k + 1, rows] = jnp.sum(jnp.where(chunk_sels[c][k], place, 0.0),
                                               axis=0, keepdims=True).astype(I32)
    cnt_ref[0, 0] = jnp.broadcast_to(counts, (n_experts, LANES)).astype(I32)


def _cross_router(h1, mk, mv, w):
    batch, length, d_model = h1.shape
    m_len = mk.shape[1]
    head_dim = d_model // MEM_HEADS
    n_experts = w["w_router_t"].shape[0]
    tile = min(CROSS_TILE, length)
    assert length % tile == 0
    n_t = length // tile
    chunk = min(CROSS_CHUNK, tile)
    assert tile % chunk == 0
    kern = functools.partial(_cross_router_kernel, tile=tile, chunk=chunk, head_dim=head_dim, n_experts=n_experts)
    tok = pl.BlockSpec((1, tile, d_model), lambda b, t: (b, t, 0))
    mem = pl.BlockSpec((1, m_len, d_model), lambda b, t: (b, 0, 0))
    small = pl.BlockSpec((1, TOP_K, tile), lambda b, t: (b, 0, t))
    return pl.pallas_call(
        kern,
        out_shape=(jax.ShapeDtypeStruct(h1.shape, F32),
                   jax.ShapeDtypeStruct(h1.shape, BF16),
                   jax.ShapeDtypeStruct((batch, TOP_K, length), I32),
                   jax.ShapeDtypeStruct((batch, TOP_K, length), F32),
                   jax.ShapeDtypeStruct((batch, n_t, n_experts, LANES), I32)),
        grid=(batch, n_t),
        in_specs=[tok, mem, mem,
                  _const_spec((1, d_model)), _const_spec((d_model, d_model)), _const_spec((d_model, d_model)),
                  _const_spec((1, d_model)), _const_spec((n_experts, d_model)), _const_spec((n_experts, 1))],
        out_specs=(tok, tok, small, small,
                   pl.BlockSpec((1, 1, n_experts, LANES), lambda b, t: (b, t, 0, 0))),
        scratch_shapes=[pltpu.VMEM((tile, d_model), BF16)],
        compiler_params=_cparams(2),
        name="cross_router",
    )(h1, mk, mv, w["norm_cross_w"], w["w_cq"], w["w_co"], w["norm_ffn_w"], w["w_router_t"], w["b_router_c"])


def _unit_copy(src_ref, dst_ref, src_unit, dst_unit, sem, n_units=1):
    def rows_of(unit):
        start = unit * ROW_UNIT
        return pl.ds(start if isinstance(start, int) else pl.multiple_of(start, ROW_UNIT), n_units * ROW_UNIT)

    return pltpu.make_async_copy(src_ref.at[rows_of(src_unit)], dst_ref.at[rows_of(dst_unit)], sem)


def _run_copies(lists_ref, local_ref, global_ref, sem, n_big_max, n_small_max, to_global):
    def make(local_at, global_at, n_units):
        def copy_of(j):
            loc, glo = lists_ref[0, 0, local_at + j], lists_ref[0, 0, global_at + j]
            if to_global:
                return _unit_copy(local_ref, global_ref, loc, glo, sem, n_units)
            return _unit_copy(global_ref, local_ref, glo, loc, sem, n_units)
        return copy_of

    return (make(0, n_big_max, BIG_UNITS),
            make(2 * n_big_max, 2 * n_big_max + n_small_max, 1))


def _for_units(n, body):
    def group(g, c):
        for lane in range(DMA_UNROLL):
            body(g * DMA_UNROLL + lane, lane)
        return c

    n_groups = lax.div(n, DMA_UNROLL)
    lax.fori_loop(0, n_groups, group, 0)

    def rest(j, c):
        body(j, 0)
        return c

    lax.fori_loop(n_groups * DMA_UNROLL, n, rest, 0)


def _start_units(n, copy_of):
    _for_units(n, lambda j, lane: copy_of(j).start(priority=lane % 2))


def _wait_units(n, copy_of):
    _for_units(n, lambda j, lane: copy_of(j).wait())


def _pack_pairs(a, b):
    ua = lax.bitcast_convert_type(a, U32)
    ub = lax.bitcast_convert_type(b, U32)
    return lax.shift_right_logical(ua, U32(16)) | (ub & U32(HIGH_HALF))


def _unpack_pairs(u):
    a = lax.bitcast_convert_type(lax.shift_left(u, U32(16)), F32)
    b = lax.bitcast_convert_type(u & U32(HIGH_HALF), F32)
    return a, b


def _dispatch_kernel(nbig_ref, nsmall_ref, tail_ref, ntail_ref, lp_ref, gate_ref, xn_ref, lists_ref, listsp_ref,
                     xs_ref, rows, zero_buf, sems, tail_sem, *, tile, n_local, d_model, n_big_max, n_small_max):
    i = pl.program_id(0)
    n_i = pl.num_programs(0)
    slot = lax.rem(i, 2)

    def for_tile(step, lists, at_slot, action):
        big, small = _run_copies(lists, rows.at[at_slot], xs_ref, sems.at[at_slot], n_big_max, n_small_max, True)
        action(nbig_ref[step], big)
        action(nsmall_ref[step], small)

    @pl.when(i == 0)
    def _():
        zero_buf[...] = jnp.zeros(zero_buf.shape, U32)

        def tail_copy(j):
            return _unit_copy(zero_buf, xs_ref, 0, tail_ref[j], tail_sem)

        _start_units(ntail_ref[0], tail_copy)
        _wait_units(ntail_ref[0], tail_copy)

    half = d_model // 2
    xn = xn_ref[0]
    lps = [lp_ref[0, k:k + 1, :] for k in range(TOP_K)]
    gates = [gate_ref[0, k:k + 1, :] for k in range(TOP_K)]
    for c in range(n_local // PERM_CHUNK):
        r_iota = lax.broadcasted_iota(I32, (PERM_CHUNK, tile), 0) + c * PERM_CHUNK
        gmat = jnp.full((PERM_CHUNK, tile), -1.0, F32)
        for k in range(TOP_K):
            gmat = jnp.where(r_iota == lps[k], gates[k], gmat)
        perm = jnp.where(gmat >= 0.0, 1.0, 0.0).astype(BF16)
        wmat = jnp.maximum(gmat, 0.0)
        rs = pl.ds(c * PERM_CHUNK, PERM_CHUNK)
        xr = _dot(perm, xn)
        rows[slot, rs, 0:half] = _pack_pairs(xr[:, 0:half], xr[:, half:d_model])
        rows[slot, rs, half:half + LANES] = lax.bitcast_convert_type(
            jnp.broadcast_to(jnp.sum(wmat, axis=1, keepdims=True), (PERM_CHUNK, LANES)), U32)

    for_tile(i, lists_ref, slot, _start_units)

    @pl.when(i > 0)
    def _():
        for_tile(i - 1, listsp_ref, 1 - slot, _wait_units)

    @pl.when(i == n_i - 1)
    def _():
        for_tile(i, lists_ref, slot, _wait_units)


def _dispatch(lp, gate, xn, plan, n_rows):
    batch, length, d_model = xn.shape
    tile, n_local = plan["tile"], plan["n_local"]
    n_big_max, n_small_max = plan["n_big_max"], plan["n_small_max"]
    n_list = 2 * (n_big_max + n_small_max)
    n_t = length // tile
    width = d_model // 2 + LANES
    small = pl.BlockSpec((1, TOP_K, tile), lambda i, *_: (i // n_t, 0, i % n_t))
    grid_spec = pltpu.PrefetchScalarGridSpec(
        num_scalar_prefetch=4,
        grid=(batch * n_t,),
        in_specs=[small, small,
                  pl.BlockSpec((1, tile, d_model), lambda i, *_: (i // n_t, i % n_t, 0)),
                  pl.BlockSpec((1, 1, n_list), lambda i, *_: (i, 0, 0), memory_space=pltpu.SMEM),
                  pl.BlockSpec((1, 1, n_list), lambda i, *_: (jnp.maximum(i - 1, 0), 0, 0),
                               memory_space=pltpu.SMEM)],
        out_specs=pl.BlockSpec(memory_space=pl.ANY),
        scratch_shapes=[pltpu.VMEM((2, n_local, width), U32), pltpu.VMEM((ROW_UNIT, width), U32),
                        pltpu.SemaphoreType.DMA((2,)), pltpu.SemaphoreType.DMA(())],
    )
    return pl.pallas_call(
        functools.partial(_dispatch_kernel, tile=tile, n_local=n_local, d_model=d_model,
                          n_big_max=n_big_max, n_small_max=n_small_max),
        out_shape=jax.ShapeDtypeStruct((n_rows, width), U32),
        grid_spec=grid_spec,
        compiler_params=_cparams(1),
        name="dispatch",
    )(plan["n_big"], plan["n_small"], plan["tail_units"], plan["n_tail"], lp, gate, xn,
      plan["copy_lists"], plan["copy_lists"])


def _expert_kernel(be_ref, nb_ref, xs_ref, wgu_ref, bgu_ref, wd_ref, bd_ref, ys_ref, *, d_model, d_expert):
    del be_ref

    @pl.when(pl.program_id(0) < nb_ref[0])
    def _():
        half = d_model // 2
        xa, xb = _unpack_pairs(xs_ref[:, 0:half])
        row_gate = lax.bitcast_convert_type(xs_ref[:, half:half + 1], F32)
        gu = _dot(jnp.concatenate([xa, xb], axis=-1).astype(BF16), wgu_ref[0]) + bgu_ref[0]
        gl = jnp.minimum(gu[:, :d_expert], SWIGLU_LIMIT)
        up = jnp.clip(gu[:, d_expert:], -SWIGLU_LIMIT, SWIGLU_LIMIT)
        act = (gl * jax.nn.sigmoid(SWIGLU_ALPHA * gl) * (up + 1.0)).astype(BF16)
        y = ((_dot(act, wd_ref[0]) + bd_ref[0]) * row_gate).astype(BF16).astype(F32)
        ys_ref[...] = _pack_pairs(y[:, 0:half], y[:, half:d_model])


def _experts(xs, block_e, n_used, w):
    n_rows, width = xs.shape
    d_model = 2 * (width - LANES)
    blk = EXPERT_BLOCK
    d_expert = w["w_down"].shape[1]

    def row_map(b, be, nb):
        return (jnp.minimum(b, nb[0] - 1), 0)

    def exp_map(b, be, nb):
        return (be[b], 0, 0)

    grid_spec = pltpu.PrefetchScalarGridSpec(
        num_scalar_prefetch=2,
        grid=(n_rows // blk,),
        in_specs=[pl.BlockSpec((blk, width), row_map),
                  pl.BlockSpec((1, d_model, 2 * d_expert), exp_map),
                  pl.BlockSpec((1, 1, 2 * d_expert), exp_map),
                  pl.BlockSpec((1, d_expert, d_model), exp_map),
                  pl.BlockSpec((1, 1, d_model), exp_map)],
        out_specs=pl.BlockSpec((blk, d_model // 2), row_map),
    )
    return pl.pallas_call(
        functools.partial(_expert_kernel, d_model=d_model, d_expert=d_expert),
        out_shape=jax.ShapeDtypeStruct((n_rows, d_model // 2), U32),
        grid_spec=grid_spec,
        compiler_params=_cparams(1),
        name="experts",
    )(block_e, n_used, xs, w["w_gu"], w["b_gu"], w["w_down"], w["b_down"])


def _combine_kernel(nbig_ref, nsmall_ref, lpc_ref, h2_ref, nw_ref, lists_ref, listsn_ref, ys_ref, out_ref,
                    rows, sems, *, tile, n_local, final_norm, n_big_max, n_small_max):
    i = pl.program_id(0)
    n_i = pl.num_programs(0)
    slot = lax.rem(i, 2)

    def for_tile(step, lists, at_slot, action):
        big, small = _run_copies(lists, rows.at[at_slot], ys_ref, sems.at[at_slot], n_big_max, n_small_max, False)
        action(nbig_ref[step], big)
        action(nsmall_ref[step], small)

    @pl.when(i == 0)
    def _():
        rows[...] = jnp.zeros(rows.shape, U32)
        for_tile(0, lists_ref, 0, _start_units)

    @pl.when(i + 1 < n_i)
    def _():
        for_tile(i + 1, listsn_ref, 1 - slot, _start_units)

    for_tile(i, lists_ref, slot, _wait_units)

    lps = [lpc_ref[:, k:k + 1] for k in range(TOP_K)]
    half = out_ref.shape[1] // 2
    ya = jnp.zeros((tile, half), F32)
    yb = jnp.zeros((tile, half), F32)
    for c in range(n_local // PERM_CHUNK):
        c_iota = lax.broadcasted_iota(I32, (tile, PERM_CHUNK), 1) + c * PERM_CHUNK
        sel = jnp.zeros((tile, PERM_CHUNK), F32)
        for k in range(TOP_K):
            sel = jnp.where(c_iota == lps[k], 1.0, sel)
        sel = sel.astype(BF16)
        ra, rb = _unpack_pairs(rows[slot, pl.ds(c * PERM_CHUNK, PERM_CHUNK), :])
        ya = ya + _dot(sel, ra.astype(BF16))
        yb = yb + _dot(sel, rb.astype(BF16))
    h3 = h2_ref[...] + jnp.concatenate([ya, yb], axis=-1)
    out_ref[...] = _rms(h3, nw_ref[...]) if final_norm else h3


def _combine(lp_cols, h2, ys, norm_w, plan, final_norm):
    n_tok, d_model = h2.shape
    tile, n_local = plan["tile"], plan["n_local"]
    n_big_max, n_small_max = plan["n_big_max"], plan["n_small_max"]
    n_list = 2 * (n_big_max + n_small_max)
    n_tiles = n_tok // tile
    grid_spec = pltpu.PrefetchScalarGridSpec(
        num_scalar_prefetch=2,
        grid=(n_tiles,),
        in_specs=[pl.BlockSpec((tile, TOP_K), lambda i, *_: (i, 0)),
                  pl.BlockSpec((tile, d_model), lambda i, *_: (i, 0)),
                  _const_spec((1, d_model)),
                  pl.BlockSpec((1, 1, n_list), lambda i, *_: (i, 0, 0), memory_space=pltpu.SMEM),
                  pl.BlockSpec((1, 1, n_list), lambda i, *_: (jnp.minimum(i + 1, n_tiles - 1), 0, 0),
                               memory_space=pltpu.SMEM),
                  pl.BlockSpec(memory_space=pl.ANY)],
        out_specs=pl.BlockSpec((tile, d_model), lambda i, *_: (i, 0)),
        scratch_shapes=[pltpu.VMEM((2, n_local, d_model // 2), U32), pltpu.SemaphoreType.DMA((2,))],
    )
    return pl.pallas_call(
        functools.partial(_combine_kernel, tile=tile, n_local=n_local, final_norm=final_norm,
                          n_big_max=n_big_max, n_small_max=n_small_max),
        out_shape=jax.ShapeDtypeStruct((n_tok, d_model), F32),
        grid_spec=grid_spec,
        compiler_params=_cparams(1),
        name="combine",
    )(plan["n_big"], plan["n_small"], lp_cols, h2, norm_w, plan["copy_lists"], plan["copy_lists"], ys)


def _moe_plan(cnt, tile):
    n_experts = cnt.shape[2]
    blk = EXPERT_BLOCK
    counts = cnt[..., 0].reshape(-1, n_experts)
    n_tiles = counts.shape[0]
    run_units = (counts + ROW_UNIT - 1) // ROW_UNIT
    unit_end = jnp.cumsum(run_units, axis=1)
    expert_units = jnp.sum(run_units, axis=0)
    blk_units = blk // ROW_UNIT
    padded_units = (expert_units + blk_units - 1) // blk_units * blk_units
    expert_end = jnp.cumsum(padded_units)
    expert_start = expert_end - padded_units
    run_base = expert_start[None, :] + jnp.cumsum(run_units, axis=0) - run_units
    n_local = _round_up(TOP_K * tile + (ROW_UNIT - 1) * n_experts, PERM_CHUNK)
    n_units = n_local // ROW_UNIT

    def expand(ends, shift, n, stride):
        i = jnp.arange(n)
        step = shift[:, 1:] - shift[:, :-1]
        past = i[None, :, None] >= ends[:, None, :-1]
        value = stride * i[None, :] + shift[:, 0:1] + jnp.sum(jnp.where(past, step[:, None, :], 0), axis=-1)
        return jnp.where(i[None, :] < ends[:, -1:], value, 0)

    run_first = unit_end - run_units
    big = run_units // BIG_UNITS
    small = run_units - BIG_UNITS * big
    big_end = jnp.cumsum(big, axis=1)
    small_end = jnp.cumsum(small, axis=1)
    big_first = big_end - big
    small_first = small_end - small
    n_big_max = n_units // BIG_UNITS
    n_small_max = (BIG_UNITS - 1) * n_experts
    copy_lists = jnp.concatenate([
        expand(big_end, run_first - BIG_UNITS * big_first, n_big_max, BIG_UNITS),
        expand(big_end, run_base - BIG_UNITS * big_first, n_big_max, BIG_UNITS),
        expand(small_end, run_first + BIG_UNITS * big - small_first, n_small_max, 1),
        expand(small_end, run_base + BIG_UNITS * big - small_first, n_small_max, 1)], axis=1)
    tj = jnp.arange(n_experts * (blk_units - 1))
    tail_len = padded_units - expert_units
    tail_end = jnp.cumsum(tail_len)
    tshift = expert_start + expert_units - (tail_end - tail_len)
    tpast = tj[:, None] >= tail_end[None, :-1]
    tail_units = tj + tshift[0] + jnp.sum(jnp.where(tpast, (tshift[1:] - tshift[:-1])[None, :], 0), axis=-1)
    n_tail = tail_end[-1]
    tail_units = jnp.where(tj < n_tail, tail_units, 0)
    n_blocks = (n_tiles * tile * TOP_K + n_tiles * n_experts * (ROW_UNIT - 1) + n_experts * (blk - 1)) // blk
    block_e = jnp.minimum(jnp.sum((jnp.arange(n_blocks) * blk_units)[:, None] >= expert_end[None, :], axis=-1),
                          n_experts - 1)
    return {
        "tile": tile, "n_local": n_local, "n_rows": n_blocks * blk,
        "n_big_max": n_big_max, "n_small_max": n_small_max,
        "n_big": big_end[:, -1].astype(I32), "n_small": small_end[:, -1].astype(I32),
        "copy_lists": copy_lists.astype(I32).reshape(n_tiles, 1, 2 * (n_big_max + n_small_max)),
        "tail_units": tail_units.astype(I32), "n_tail": n_tail.astype(I32).reshape(1),
        "block_e": block_e.astype(I32), "n_used": (expert_end[-1] // blk_units).astype(I32).reshape(1),
    }


def _moe_tail(h2, xn, lp, gate, cnt, w, norm_w, final_norm):
    batch, length, d_model = h2.shape
    n_tok = batch * length
    tile = length // cnt.shape[1]
    plan = _moe_plan(cnt, tile)
    xs = _dispatch(lp, gate, xn, plan, plan["n_rows"])
    ys = _experts(xs, plan["block_e"], plan["n_used"], w)
    lp_cols = lp.transpose(0, 2, 1).reshape(n_tok, TOP_K)
    out = _combine(lp_cols, h2.reshape(n_tok, d_model), ys, norm_w, plan, final_norm)
    return out.reshape(batch, length, d_model)


def _layer(x, hist, s0, pos0, n_hist_valid, mk, mv, w, norm_w, final_norm):
    h1, new_hist, s_new = _mixer(x, hist, s0, pos0, n_hist_valid, w)
    h2, xn, lp, gate, cnt = _cross_router(h1, mk.astype(BF16), mv.astype(BF16), w)
    out = _moe_tail(h2, xn, lp, gate, cnt, w, norm_w, final_norm)
    return out, new_hist, s_new


def kernel(x_prompt, x_sample, mem_prompt, state_pool, state_ret, cache_mem_k, cache_mem_v, norm_mix_w, w_in, pool_w_grp, pool_scale, w_pool_out, ret_gn_w, w_ret_out, w_o, norm_mem_w, w_mk, w_mv, norm_cross_w, w_cq, w_co, norm_ffn_w, w_router, b_router, w_gu, b_gu, w_down, b_down, norm_final_w):
    depth = w_in.shape[0]
    batch_p = x_prompt.shape[0]
    m_len = mem_prompt.shape[1]
    d_model = x_prompt.shape[-1]
    hp, hs = x_prompt, x_sample
    norm_w = norm_final_w[None, :]
    mem_k_p, mem_v_p, pool_p, ret_p, pool_s, ret_s = [], [], [], [], [], []
    for l in range(depth):
        w = {
            "norm_mix_w": norm_mix_w[l][None, :], "w_in": w_in[l].astype(BF16),
            "pool_w_grp": pool_w_grp[l].astype(BF16), "pool_scale": pool_scale[l][None, :],
            "w_pool_out": w_pool_out[l].astype(BF16), "ret_gn_w": ret_gn_w[l][None, :],
            "w_ret_out": w_ret_out[l].astype(BF16), "w_o": w_o[l].astype(BF16),
            "norm_mem_w": norm_mem_w[l][None, :], "w_mk": w_mk[l].astype(BF16), "w_mv": w_mv[l].astype(BF16),
            "norm_cross_w": norm_cross_w[l][None, :], "w_cq": w_cq[l].astype(BF16), "w_co": w_co[l].astype(BF16),
            "norm_ffn_w": norm_ffn_w[l][None, :], "w_router_t": w_router[l].T.astype(BF16),
            "b_router_c": b_router[l][:, None],
            "w_gu": w_gu[l].astype(BF16), "b_gu": b_gu[l][:, None, :],
            "w_down": w_down[l].astype(BF16), "b_down": b_down[l][:, None, :],
        }
        last = l == depth - 1
        mk, mv = _mem_kv(mem_prompt, w)
        zero_hist = jnp.zeros((batch_p, POOL_HIST, POOL_WIDTH), F32)
        zero_state = jnp.zeros((batch_p, RET_HEADS, RET_DK, RET_DV), F32)
        hp, hist_p, s_p = _layer(hp, zero_hist, zero_state, 0, 0, mk, mv, w, norm_w, last)
        mem_k_p.append(mk.reshape(batch_p, m_len, MEM_HEADS, d_model // MEM_HEADS))
        mem_v_p.append(mv.reshape(batch_p, m_len, MEM_HEADS, d_model // MEM_HEADS))
        pool_p.append(hist_p)
        ret_p.append(s_p)
        ck = cache_mem_k[l].reshape(cache_mem_k.shape[1], m_len, d_model)
        cv = cache_mem_v[l].reshape(cache_mem_v.shape[1], m_len, d_model)
        hs, hist_s, s_s = _layer(hs, state_pool[l], state_ret[l], PAST_LEN, POOL_HIST, ck, cv, w, norm_w, last)
        pool_s.append(hist_s)
        ret_s.append(s_s)
    return (hp, hs, jnp.stack(mem_k_p), jnp.stack(mem_v_p), jnp.stack(pool_p), jnp.stack(ret_p),
            jnp.stack(pool_s), jnp.stack(ret_s))
```

```python
import functools

import jax
import jax.numpy as jnp
from jax import lax
from jax.experimental import pallas as pl
from jax.experimental.pallas import tpu as pltpu

F32 = jnp.float32
BF16 = jnp.bfloat16
I32 = jnp.int32
U32 = jnp.uint32
HIGH_HALF = 0xFFFF0000

EPS = 1e-6
PAST_LEN = 1024
POOL_WINDOWS = (2, 4, 8, 16)
POOL_GROUP_DIM = 128
POOL_WIDTH = POOL_GROUP_DIM * len(POOL_WINDOWS)
POOL_HIST = max(POOL_WINDOWS) - 1
RET_HEADS = 4
RET_DK = 128
RET_DV = 256
ROPE_BASE = 10000.0
MEM_HEADS = 4
TOP_K = 4
SWIGLU_LIMIT = 7.0
SWIGLU_ALPHA = 1.702

SUBLANES = 8
LANES = 128
VMEM_LIMIT_BYTES = 56 * 1024 * 1024

MIXER_TILE = 512
RET_CHUNK = 256
CROSS_TILE = 512
CROSS_CHUNK = 512
EXPERT_BLOCK = 512
MIN_EXPERT_BLOCK = 128
ROW_UNIT = SUBLANES
PERM_CHUNK = 256
BIG_UNITS = 4
DMA_UNROLL = 4


def _cparams(n_axes):
    return pltpu.CompilerParams(dimension_semantics=("arbitrary",) * n_axes,
                                vmem_limit_bytes=VMEM_LIMIT_BYTES)


def _const_spec(shape):
    nd = len(shape)
    return pl.BlockSpec(shape, lambda *_: (0,) * nd, pipeline_mode=pl.Buffered(1))


def _round_up(n, m):
    return (n + m - 1) // m * m


def _rms(x32, w_row):
    ms = jnp.mean(x32 * x32, axis=-1, keepdims=True)
    return x32 * lax.rsqrt(ms + EPS) * w_row


def _dot(a, b):
    return jnp.dot(a, b, preferred_element_type=F32)


def _dot_nt(a, b):
    return lax.dot_general(a, b, (((1,), (1,)), ((), ())), preferred_element_type=F32)


def _dot_tn(a, b):
    return lax.dot_general(a, b, (((0,), (0,)), ((), ())), preferred_element_type=F32)


def _rope_kernel(inv_ref, sign_ref, cos_ref, sin_ref, *, pos0, tile):
    i = pl.program_id(0)
    pos = (lax.broadcasted_iota(I32, (tile, RET_DK), 0) + (pos0 + i * tile)).astype(F32)
    ang = pos * inv_ref[...]
    cos_ref[...] = jnp.cos(ang)
    sin_ref[...] = jnp.sin(ang) * sign_ref[...]


def _rope_tables(length, pos0):
    half = RET_DK // 2
    inv = 1.0 / (ROPE_BASE ** (jnp.arange(half, dtype=F32) / half))
    inv2 = jnp.concatenate([inv, inv])[None, :]
    sign = jnp.concatenate([-jnp.ones((half,), F32), jnp.ones((half,), F32)])[None, :]
    tile = min(length, 512)
    assert length % tile == 0
    return pl.pallas_call(
        functools.partial(_rope_kernel, pos0=pos0, tile=tile),
        out_shape=(jax.ShapeDtypeStruct((length, RET_DK), F32),) * 2,
        grid=(length // tile,),
        in_specs=[pl.BlockSpec((1, RET_DK), lambda i: (0, 0))] * 2,
        out_specs=(pl.BlockSpec((tile, RET_DK), lambda i: (i, 0)),) * 2,
        compiler_params=_cparams(1),
        name="rope_tables",
    )(inv2, sign)


def _mixer_kernel(x_ref, hist_ref, s0_ref, cos_ref, sin_ref, nmw_ref, win_ref, pwg_ref, pscale_ref,
                  wpo_ref, gnw_ref, wro_ref, wo_ref, dec_ref, qd_ref, kd_ref, gc_ref,
                  h_ref, hist_out_ref, s_out_ref,
                  ubuf, s_scr, z_scr, r_scr, yp_scr, *, tile, chunk, n_hist_valid, d_model):
    t = pl.program_id(1)
    n_t = pl.num_programs(1)
    q_off = POOL_WIDTH
    k_off = q_off + RET_HEADS * RET_DK
    v_off = k_off + RET_HEADS * RET_DK
    g_off = v_off + RET_HEADS * RET_DV
    ap_off = g_off + RET_HEADS * RET_DV
    ar_off = ap_off + d_model

    @pl.when(t == 0)
    def _():
        ubuf[0:1, :] = jnp.zeros((1, POOL_WIDTH), F32)
        ubuf[1:1 + POOL_HIST, :] = hist_ref[0]
        s_scr[...] = s0_ref[0]

    n_chunks = tile // chunk
    hist_rows = POOL_HIST + 1
    for c in range(n_chunks):
        rows = slice(c * chunk, (c + 1) * chunk)
        xn = _rms(x_ref[0, rows, :], nmw_ref[...]).astype(BF16)
        z_scr[rows, :] = _dot(xn, win_ref[...])
        ubuf[hist_rows + c * chunk:hist_rows + (c + 1) * chunk, :] = z_scr[rows, 0:POOL_WIDTH]

    for c in range(n_chunks):
        rows = slice(c * chunk, (c + 1) * chunk)
        full = ubuf[c * chunk:c * chunk + hist_rows + chunk, :]
        pos = t * tile + c * chunk + lax.broadcasted_iota(I32, (chunk, 1), 0)
        ys = []
        for g, w in enumerate(POOL_WINDOWS):
            f = full[:, g * POOL_GROUP_DIM:(g + 1) * POOL_GROUP_DIM]
            s = f
            shift = 1
            while shift < w:
                s = s + pltpu.roll(s, shift, 0)
                shift *= 2
            cnt = jnp.minimum(w, pos + 1 + n_hist_valid).astype(F32)
            d = s[hist_rows:, :] / cnt - f[hist_rows:, :]
            ys.append(_dot(d.astype(BF16), pwg_ref[g]))
        yp_scr[rows, :] = (jnp.concatenate(ys, axis=-1) * pscale_ref[...]).astype(BF16)
    ubuf[1:1 + POOL_HIST, :] = ubuf[tile + 1:tile + 1 + POOL_HIST, :]

    for c in range(n_chunks):
        rows = slice(c * chunk, (c + 1) * chunk)
        cosc = cos_ref[rows, :]
        sinc = sin_ref[rows, :]
        for h in range(RET_HEADS):
            qh = z_scr[rows, q_off + h * RET_DK:q_off + (h + 1) * RET_DK]
            kh = z_scr[rows, k_off + h * RET_DK:k_off + (h + 1) * RET_DK]
            vb = z_scr[rows, v_off + h * RET_DV:v_off + (h + 1) * RET_DV].astype(BF16)
            qr = qh * cosc + pltpu.roll(qh, RET_DK // 2, 1) * sinc
            kr = (kh * cosc + pltpu.roll(kh, RET_DK // 2, 1) * sinc) * (RET_DK ** -0.5)
            qb = qr.astype(BF16)
            kb = kr.astype(BF16)
            scores = _dot_nt(qb, kb) * dec_ref[h]
            state = s_scr[h]
            o = _dot(scores.astype(BF16), vb) + _dot(qb, state.astype(BF16)) * qd_ref[h]
            s_scr[h] = gc_ref[h] * state + _dot_tn((kr * kd_ref[h]).astype(BF16), vb)
            mu = jnp.mean(o, axis=-1, keepdims=True)
            dlt = o - mu
            var = jnp.mean(dlt * dlt, axis=-1, keepdims=True)
            on = dlt * lax.rsqrt(var + EPS) * gnw_ref[:, h * RET_DV:(h + 1) * RET_DV]
            gate = z_scr[rows, g_off + h * RET_DV:g_off + (h + 1) * RET_DV]
            r_scr[rows, h * RET_DV:(h + 1) * RET_DV] = (on * (gate * jax.nn.sigmoid(gate))).astype(BF16)

    for c in range(n_chunks):
        rows = slice(c * chunk, (c + 1) * chunk)
        branch_pool = _dot(yp_scr[rows, :], wpo_ref[...])
        branch_ret = _dot(r_scr[rows, :], wro_ref[...])
        merged = (jax.nn.sigmoid(z_scr[rows, ap_off:ap_off + d_model]) * branch_pool
                  + jax.nn.sigmoid(z_scr[rows, ar_off:ar_off + d_model]) * branch_ret)
        h_ref[0, rows, :] = x_ref[0, rows, :] + _dot(merged.astype(BF16), wo_ref[...])

    @pl.when(t == n_t - 1)
    def _():
        hist_out_ref[0] = ubuf[1:1 + POOL_HIST, :]
        s_out_ref[0] = s_scr[...]


def _ret_tables(chunk):
    lg = jnp.log(1.0 - 2.0 ** (-5.0 - jnp.arange(RET_HEADS, dtype=F32)))
    idx = jnp.arange(chunk)
    rel = idx[:, None] - idx[None, :]
    decay = jnp.where(rel[None] >= 0, jnp.exp(jnp.maximum(rel, 0)[None].astype(F32) * lg[:, None, None]), 0.0)
    q_decay = jnp.exp((idx + 1).astype(F32)[None, :] * lg[:, None])[:, :, None]
    k_decay = jnp.exp((chunk - 1 - idx).astype(F32)[None, :] * lg[:, None])[:, :, None]
    g_chunk = jnp.exp(chunk * lg)
    return decay, q_decay, k_decay, g_chunk


def _mixer(x, hist, s0, pos0, n_hist_valid, w):
    batch, length, d_model = x.shape
    tile = min(MIXER_TILE, length)
    chunk = min(RET_CHUNK, tile)
    assert length % tile == 0 and tile % chunk == 0 and tile >= POOL_HIST + 1
    in_width = w["w_in"].shape[1]
    cos, sin = _rope_tables(length, pos0)
    decay, q_decay, k_decay, g_chunk = _ret_tables(chunk)
    kern = functools.partial(_mixer_kernel, tile=tile, chunk=chunk, n_hist_valid=n_hist_valid, d_model=d_model)
    return pl.pallas_call(
        kern,
        out_shape=(jax.ShapeDtypeStruct(x.shape, F32),
                   jax.ShapeDtypeStruct(hist.shape, F32),
                   jax.ShapeDtypeStruct(s0.shape, F32)),
        grid=(batch, length // tile),
        in_specs=[
            pl.BlockSpec((1, tile, d_model), lambda b, t: (b, t, 0)),
            pl.BlockSpec((1, POOL_HIST, POOL_WIDTH), lambda b, t: (b, 0, 0)),
            pl.BlockSpec((1, RET_HEADS, RET_DK, RET_DV), lambda b, t: (b, 0, 0, 0)),
            pl.BlockSpec((tile, RET_DK), lambda b, t: (t, 0)),
            pl.BlockSpec((tile, RET_DK), lambda b, t: (t, 0)),
            _const_spec((1, d_model)),
            _const_spec((d_model, in_width)),
            _const_spec(w["pool_w_grp"].shape),
            _const_spec((1, POOL_WIDTH)),
            _const_spec((POOL_WIDTH, d_model)),
            _const_spec((1, RET_HEADS * RET_DV)),
            _const_spec((RET_HEADS * RET_DV, d_model)),
            _const_spec((d_model, d_model)),
            _const_spec(decay.shape),
            _const_spec(q_decay.shape),
            _const_spec(k_decay.shape),
            pl.BlockSpec(memory_space=pltpu.SMEM),
        ],
        out_specs=(
            pl.BlockSpec((1, tile, d_model), lambda b, t: (b, t, 0)),
            pl.BlockSpec((1, POOL_HIST, POOL_WIDTH), lambda b, t: (b, 0, 0)),
            pl.BlockSpec((1, RET_HEADS, RET_DK, RET_DV), lambda b, t: (b, 0, 0, 0)),
        ),
        scratch_shapes=[
            pltpu.VMEM((POOL_HIST + 1 + tile, POOL_WIDTH), F32),
            pltpu.VMEM((RET_HEADS, RET_DK, RET_DV), F32),
            pltpu.VMEM((tile, in_width), F32),
            pltpu.VMEM((tile, RET_HEADS * RET_DV), BF16),
            pltpu.VMEM((tile, POOL_WIDTH), BF16),
        ],
        compiler_params=_cparams(2),
        name="mixer",
    )(x, hist, s0, cos, sin, w["norm_mix_w"], w["w_in"], w["pool_w_grp"], w["pool_scale"], w["w_pool_out"],
      w["ret_gn_w"], w["w_ret_out"], w["w_o"], decay, q_decay, k_decay, g_chunk)


def _mem_kv_kernel(mem_ref, nw_ref, wk_ref, wv_ref, k_ref, v_ref):
    mn = _rms(mem_ref[0], nw_ref[...]).astype(BF16)
    k_ref[0] = _dot(mn, wk_ref[...])
    v_ref[0] = _dot(mn, wv_ref[...])


def _mem_kv(mem, w):
    batch, m_len, d_model = mem.shape
    out = jax.ShapeDtypeStruct((batch, m_len, d_model), F32)
    blk = pl.BlockSpec((1, m_len, d_model), lambda b: (b, 0, 0))
    return pl.pallas_call(
        _mem_kv_kernel,
        out_shape=(out, out),
        grid=(batch,),
        in_specs=[blk, _const_spec((1, d_model)), _const_spec((d_model, d_model)), _const_spec((d_model, d_model))],
        out_specs=(blk, blk),
        compiler_params=_cparams(1),
        name="mem_kv",
    )(mem, w["norm_mem_w"], w["w_mk"], w["w_mv"])


def _cross_router_kernel(h_ref, mk_ref, mv_ref, ncw_ref, wcq_ref, wco_ref, nfw_ref, wrt_ref, brt_ref,
                         h2_ref, xn_ref, lp_ref, gate_ref, cnt_ref,
                         o_scr, *, tile, chunk, head_dim, n_experts):
    e_iota = lax.broadcasted_iota(I32, (n_experts, chunk), 0)
    r_iota = lax.broadcasted_iota(I32, (chunk, chunk), 0)
    c_iota = lax.broadcasted_iota(I32, (chunk, chunk), 1)
    earlier = (r_iota < c_iota).astype(BF16)
    counts = jnp.zeros((n_experts, 1), F32)
    chunk_sels, chunk_prior = [], []
    for c in range(tile // chunk):
        rows = slice(c * chunk, (c + 1) * chunk)
        h1 = h_ref[0, rows, :]
        hn = _rms(h1, ncw_ref[...]).astype(BF16)
        q = _dot(hn, wcq_ref[...]).astype(BF16)
        for hd in range(MEM_HEADS):
            cols = slice(hd * head_dim, (hd + 1) * head_dim)
            s = _dot_nt(q[:, cols], mk_ref[0, :, cols]) * (head_dim ** -0.5)
            e = jnp.exp(s - jnp.max(s, axis=-1, keepdims=True))
            p = e / jnp.sum(e, axis=-1, keepdims=True)
            o_scr[rows, cols] = _dot(p.astype(BF16), mv_ref[0, :, cols]).astype(BF16)
        h2 = h1 + _dot(o_scr[rows, :], wco_ref[...])
        h2_ref[0, rows, :] = h2
        xn_ref[0, rows, :] = _rms(h2, nfw_ref[...]).astype(BF16)

    for c in range(tile // chunk):
        rows = slice(c * chunk, (c + 1) * chunk)
        work = _dot_nt(wrt_ref[...], xn_ref[0, rows, :]) + brt_ref[...]
        vals, sels = [], []
        for _ in range(TOP_K):
            m = jnp.max(work, axis=0, keepdims=True)
            first = jnp.min(jnp.where(work == m, e_iota, n_experts), axis=0, keepdims=True)
            sel = e_iota == first
            vals.append(m)
            sels.append(sel)
            work = jnp.where(sel, -jnp.inf, work)
        exps = [jnp.exp(v - vals[0]) for v in vals]
        denom = exps[0] + exps[1] + exps[2] + exps[3]
        for k in range(TOP_K):
            gate_ref[0, k:k + 1, rows] = exps[k] / denom
        assigned = jnp.zeros((n_experts, chunk), F32)
        for sel in sels:
            assigned = assigned + sel.astype(F32)
        chunk_prior.append(counts + _dot(assigned.astype(BF16), earlier))
        chunk_sels.append(sels)
        counts = counts + jnp.sum(assigned, axis=1, keepdims=True)

    run = jnp.floor((counts + (ROW_UNIT - 1)) * (1.0 / ROW_UNIT)) * ROW_UNIT
    er = lax.broadcasted_iota(I32, (n_experts, n_experts), 0)
    ec = lax.broadcasted_iota(I32, (n_experts, n_experts), 1)
    run_start = _dot((ec < er).astype(BF16), jnp.broadcast_to(run, (n_experts, LANES)).astype(BF16))[:, 0:1]
    for c in range(tile // chunk):
        rows = slice(c * chunk, (c + 1) * chunk)
        place = run_start + chunk_prior[c]
        for k in range(TOP_K):
            lp_ref[0, k:k + 1, rows] = jnp.sum(jnp.where(chunk_sels[c][k], place, 0.0),
                                               axis=0, keepdims=True).astype(I32)
    cnt_ref[0, 0] = jnp.broadcast_to(counts, (n_experts, LANES)).astype(I32)


def _cross_router(h1, mk, mv, w):
    batch, length, d_model = h1.shape
    m_len = mk.shape[1]
    head_dim = d_model // MEM_HEADS
    n_experts = w["w_router_t"].shape[0]
    tile = min(CROSS_TILE, length)
    assert length % tile == 0
    n_t = length // tile
    chunk = min(CROSS_CHUNK, tile)
    assert tile % chunk == 0
    kern = functools.partial(_cross_router_kernel, tile=tile, chunk=chunk, head_dim=head_dim, n_experts=n_experts)
    tok = pl.BlockSpec((1, tile, d_model), lambda b, t: (b, t, 0))
    mem = pl.BlockSpec((1, m_len, d_model), lambda b, t: (b, 0, 0))
    small = pl.BlockSpec((1, TOP_K, tile), lambda b, t: (b, 0, t))
    return pl.pallas_call(
        kern,
        out_shape=(jax.ShapeDtypeStruct(h1.shape, F32),
                   jax.ShapeDtypeStruct(h1.shape, BF16),
                   jax.ShapeDtypeStruct((batch, TOP_K, length), I32),
                   jax.ShapeDtypeStruct((batch, TOP_K, length), F32),
                   jax.ShapeDtypeStruct((batch, n_t, n_experts, LANES), I32)),
        grid=(batch, n_t),
        in_specs=[tok, mem, mem,
                  _const_spec((1, d_model)), _const_spec((d_model, d_model)), _const_spec((d_model, d_model)),
                  _const_spec((1, d_model)), _const_spec((n_experts, d_model)), _const_spec((n_experts, 1))],
        out_specs=(tok, tok, small, small,
                   pl.BlockSpec((1, 1, n_experts, LANES), lambda b, t: (b, t, 0, 0))),
        scratch_shapes=[pltpu.VMEM((tile, d_model), BF16)],
        compiler_params=_cparams(2),
        name="cross_router",
    )(h1, mk, mv, w["norm_cross_w"], w["w_cq"], w["w_co"], w["norm_ffn_w"], w["w_router_t"], w["b_router_c"])


def _unit_copy(src_ref, dst_ref, src_unit, dst_unit, sem, n_units=1):
    def rows_of(unit):
        start = unit * ROW_UNIT
        return pl.ds(start if isinstance(start, int) else pl.multiple_of(start, ROW_UNIT), n_units * ROW_UNIT)

    return pltpu.make_async_copy(src_ref.at[rows_of(src_unit)], dst_ref.at[rows_of(dst_unit)], sem)


def _run_copies(lists_ref, local_ref, global_ref, sem, n_big_max, n_small_max, to_global):
    def make(local_at, global_at, n_units):
        def copy_of(j):
            loc, glo = lists_ref[0, 0, local_at + j], lists_ref[0, 0, global_at + j]
            if to_global:
                return _unit_copy(local_ref, global_ref, loc, glo, sem, n_units)
            return _unit_copy(global_ref, local_ref, glo, loc, sem, n_units)
        return copy_of

    return (make(0, n_big_max, BIG_UNITS),
            make(2 * n_big_max, 2 * n_big_max + n_small_max, 1))


def _for_units(n, body):
    def group(g, c):
        for lane in range(DMA_UNROLL):
            body(g * DMA_UNROLL + lane, lane)
        return c

    n_groups = lax.div(n, DMA_UNROLL)
    lax.fori_loop(0, n_groups, group, 0)

    def rest(j, c):
        body(j, 0)
        return c

    lax.fori_loop(n_groups * DMA_UNROLL, n, rest, 0)


def _start_units(n, copy_of):
    _for_units(n, lambda j, lane: copy_of(j).start(priority=lane % 2))


def _wait_units(n, copy_of):
    _for_units(n, lambda j, lane: copy_of(j).wait())


def _pack_pairs(a, b):
    ua = lax.bitcast_convert_type(a, U32)
    ub = lax.bitcast_convert_type(b, U32)
    return lax.shift_right_logical(ua, U32(16)) | (ub & U32(HIGH_HALF))


def _unpack_pairs(u):
    a = lax.bitcast_convert_type(lax.shift_left(u, U32(16)), F32)
    b = lax.bitcast_convert_type(u & U32(HIGH_HALF), F32)
    return a, b


def _dispatch_kernel(nbig_ref, nsmall_ref, tail_ref, ntail_ref, lp_ref, gate_ref, xn_ref, lists_ref, listsp_ref,
                     xs_ref, rows, zero_buf, sems, tail_sem, *, tile, n_local, d_model, n_big_max, n_small_max):
    i = pl.program_id(0)
    n_i = pl.num_programs(0)
    slot = lax.rem(i, 2)

    def for_tile(step, lists, at_slot, action):
        big, small = _run_copies(lists, rows.at[at_slot], xs_ref, sems.at[at_slot], n_big_max, n_small_max, True)
        action(nbig_ref[step], big)
        action(nsmall_ref[step], small)

    @pl.when(i == 0)
    def _():
        zero_buf[...] = jnp.zeros(zero_buf.shape, U32)

        def tail_copy(j):
            return _unit_copy(zero_buf, xs_ref, 0, tail_ref[j], tail_sem)

        _start_units(ntail_ref[0], tail_copy)
        _wait_units(ntail_ref[0], tail_copy)

    half = d_model // 2
    xn = xn_ref[0]
    lps = [lp_ref[0, k:k + 1, :] for k in range(TOP_K)]
    gates = [gate_ref[0, k:k + 1, :] for k in range(TOP_K)]
    for c in range(n_local // PERM_CHUNK):
        r_iota = lax.broadcasted_iota(I32, (PERM_CHUNK, tile), 0) + c * PERM_CHUNK
        gmat = jnp.full((PERM_CHUNK, tile), -1.0, F32)
        for k in range(TOP_K):
            gmat = jnp.where(r_iota == lps[k], gates[k], gmat)
        perm = jnp.where(gmat >= 0.0, 1.0, 0.0).astype(BF16)
        wmat = jnp.maximum(gmat, 0.0)
        rs = pl.ds(c * PERM_CHUNK, PERM_CHUNK)
        xr = _dot(perm, xn)
        rows[slot, rs, 0:half] = _pack_pairs(xr[:, 0:half], xr[:, half:d_model])
        rows[slot, rs, half:half + LANES] = lax.bitcast_convert_type(
            jnp.broadcast_to(jnp.sum(wmat, axis=1, keepdims=True), (PERM_CHUNK, LANES)), U32)

    for_tile(i, lists_ref, slot, _start_units)

    @pl.when(i > 0)
    def _():
        for_tile(i - 1, listsp_ref, 1 - slot, _wait_units)

    @pl.when(i == n_i - 1)
    def _():
        for_tile(i, lists_ref, slot, _wait_units)


def _dispatch(lp, gate, xn, plan, n_rows):
    batch, length, d_model = xn.shape
    tile, n_local = plan["tile"], plan["n_local"]
    n_big_max, n_small_max = plan["n_big_max"], plan["n_small_max"]
    n_list = 2 * (n_big_max + n_small_max)
    n_t = length // tile
    width = d_model // 2 + LANES
    small = pl.BlockSpec((1, TOP_K, tile), lambda i, *_: (i // n_t, 0, i % n_t))
    grid_spec = pltpu.PrefetchScalarGridSpec(
        num_scalar_prefetch=4,
        grid=(batch * n_t,),
        in_specs=[small, small,
                  pl.BlockSpec((1, tile, d_model), lambda i, *_: (i // n_t, i % n_t, 0)),
                  pl.BlockSpec((1, 1, n_list), lambda i, *_: (i, 0, 0), memory_space=pltpu.SMEM),
                  pl.BlockSpec((1, 1, n_list), lambda i, *_: (jnp.maximum(i - 1, 0), 0, 0),
                               memory_space=pltpu.SMEM)],
        out_specs=pl.BlockSpec(memory_space=pl.ANY),
        scratch_shapes=[pltpu.VMEM((2, n_local, width), U32), pltpu.VMEM((ROW_UNIT, width), U32),
                        pltpu.SemaphoreType.DMA((2,)), pltpu.SemaphoreType.DMA(())],
    )
    return pl.pallas_call(
        functools.partial(_dispatch_kernel, tile=tile, n_local=n_local, d_model=d_model,
                          n_big_max=n_big_max, n_small_max=n_small_max),
        out_shape=jax.ShapeDtypeStruct((n_rows, width), U32),
        grid_spec=grid_spec,
        compiler_params=_cparams(1),
        name="dispatch",
    )(plan["n_big"], plan["n_small"], plan["tail_units"], plan["n_tail"], lp, gate, xn,
      plan["copy_lists"], plan["copy_lists"])


def _expert_kernel(be_ref, nb_ref, xs_ref, wgu_ref, bgu_ref, wd_ref, bd_ref, ys_ref, wgu_bf, wd_bf,
                   *, d_model, d_expert):
    b = pl.program_id(0)

    @pl.when(b < nb_ref[0])
    def _():
        @pl.when((b == 0) | (be_ref[b] != be_ref[jnp.maximum(b - 1, 0)]))
        def _():
            wgu_bf[...] = wgu_ref[0].astype(BF16)
            wd_bf[...] = wd_ref[0].astype(BF16)

        half = d_model // 2
        xa, xb = _unpack_pairs(xs_ref[:, 0:half])
        row_gate = lax.bitcast_convert_type(xs_ref[:, half:half + 1], F32)
        gu = _dot(jnp.concatenate([xa, xb], axis=-1).astype(BF16), wgu_bf[...]) + bgu_ref[0]
        gl = jnp.minimum(gu[:, :d_expert], SWIGLU_LIMIT)
        up = jnp.clip(gu[:, d_expert:], -SWIGLU_LIMIT, SWIGLU_LIMIT)
        act = (gl * jax.nn.sigmoid(SWIGLU_ALPHA * gl) * (up + 1.0)).astype(BF16)
        y = ((_dot(act, wd_bf[...]) + bd_ref[0]) * row_gate).astype(BF16).astype(F32)
        ys_ref[...] = _pack_pairs(y[:, 0:half], y[:, half:d_model])


def _experts(xs, block_e, n_used, w, blk):
    n_rows, width = xs.shape
    d_model = 2 * (width - LANES)
    d_expert = w["w_down"].shape[1]

    def row_map(b, be, nb):
        return (jnp.minimum(b, nb[0] - 1), 0)

    def exp_map(b, be, nb):
        return (be[b], 0, 0)

    grid_spec = pltpu.PrefetchScalarGridSpec(
        num_scalar_prefetch=2,
        grid=(n_rows // blk,),
        in_specs=[pl.BlockSpec((blk, width), row_map),
                  pl.BlockSpec((1, d_model, 2 * d_expert), exp_map),
                  pl.BlockSpec((1, 1, 2 * d_expert), exp_map),
                  pl.BlockSpec((1, d_expert, d_model), exp_map),
                  pl.BlockSpec((1, 1, d_model), exp_map)],
        out_specs=pl.BlockSpec((blk, d_model // 2), row_map),
        scratch_shapes=[pltpu.VMEM((d_model, 2 * d_expert), BF16), pltpu.VMEM((d_expert, d_model), BF16)],
    )
    return pl.pallas_call(
        functools.partial(_expert_kernel, d_model=d_model, d_expert=d_expert),
        out_shape=jax.ShapeDtypeStruct((n_rows, d_model // 2), U32),
        grid_spec=grid_spec,
        compiler_params=_cparams(1),
        name="experts",
    )(block_e, n_used, xs, w["w_gu"], w["b_gu"], w["w_down"], w["b_down"])


def _combine_kernel(nbig_ref, nsmall_ref, lpc_ref, h2_ref, nw_ref, lists_ref, listsn_ref, ys_ref, out_ref,
                    rows, sems, *, tile, n_local, final_norm, n_big_max, n_small_max):
    i = pl.program_id(0)
    n_i = pl.num_programs(0)
    slot = lax.rem(i, 2)

    def for_tile(step, lists, at_slot, action):
        big, small = _run_copies(lists, rows.at[at_slot], ys_ref, sems.at[at_slot], n_big_max, n_small_max, False)
        action(nbig_ref[step], big)
        action(nsmall_ref[step], small)

    @pl.when(i == 0)
    def _():
        rows[...] = jnp.zeros(rows.shape, U32)
        for_tile(0, lists_ref, 0, _start_units)

    @pl.when(i + 1 < n_i)
    def _():
        for_tile(i + 1, listsn_ref, 1 - slot, _start_units)

    for_tile(i, lists_ref, slot, _wait_units)

    lps = [lpc_ref[:, k:k + 1] for k in range(TOP_K)]
    half = out_ref.shape[1] // 2
    ya = jnp.zeros((tile, half), F32)
    yb = jnp.zeros((tile, half), F32)
    for c in range(n_local // PERM_CHUNK):
        c_iota = lax.broadcasted_iota(I32, (tile, PERM_CHUNK), 1) + c * PERM_CHUNK
        sel = jnp.zeros((tile, PERM_CHUNK), F32)
        for k in range(TOP_K):
            sel = jnp.where(c_iota == lps[k], 1.0, sel)
        sel = sel.astype(BF16)
        ra, rb = _unpack_pairs(rows[slot, pl.ds(c * PERM_CHUNK, PERM_CHUNK), :])
        ya = ya + _dot(sel, ra.astype(BF16))
        yb = yb + _dot(sel, rb.astype(BF16))
    h3 = h2_ref[...] + jnp.concatenate([ya, yb], axis=-1)
    out_ref[...] = _rms(h3, nw_ref[...]) if final_norm else h3


def _combine(lp_cols, h2, ys, norm_w, plan, final_norm):
    n_tok, d_model = h2.shape
    tile, n_local = plan["tile"], plan["n_local"]
    n_big_max, n_small_max = plan["n_big_max"], plan["n_small_max"]
    n_list = 2 * (n_big_max + n_small_max)
    n_tiles = n_tok // tile
    grid_spec = pltpu.PrefetchScalarGridSpec(
        num_scalar_prefetch=2,
        grid=(n_tiles,),
        in_specs=[pl.BlockSpec((tile, TOP_K), lambda i, *_: (i, 0)),
                  pl.BlockSpec((tile, d_model), lambda i, *_: (i, 0)),
                  _const_spec((1, d_model)),
                  pl.BlockSpec((1, 1, n_list), lambda i, *_: (i, 0, 0), memory_space=pltpu.SMEM),
                  pl.BlockSpec((1, 1, n_list), lambda i, *_: (jnp.minimum(i + 1, n_tiles - 1), 0, 0),
                               memory_space=pltpu.SMEM),
                  pl.BlockSpec(memory_space=pl.ANY)],
        out_specs=pl.BlockSpec((tile, d_model), lambda i, *_: (i, 0)),
        scratch_shapes=[pltpu.VMEM((2, n_local, d_model // 2), U32), pltpu.SemaphoreType.DMA((2,))],
    )
    return pl.pallas_call(
        functools.partial(_combine_kernel, tile=tile, n_local=n_local, final_norm=final_norm,
                          n_big_max=n_big_max, n_small_max=n_small_max),
        out_shape=jax.ShapeDtypeStruct((n_tok, d_model), F32),
        grid_spec=grid_spec,
        compiler_params=_cparams(1),
        name="combine",
    )(plan["n_big"], plan["n_small"], lp_cols, h2, norm_w, plan["copy_lists"], plan["copy_lists"], ys)


def _moe_plan(cnt, tile):
    n_experts = cnt.shape[2]
    counts = cnt[..., 0].reshape(-1, n_experts)
    n_tiles = counts.shape[0]
    blk = EXPERT_BLOCK
    while blk > MIN_EXPERT_BLOCK and n_tiles * tile * TOP_K < n_experts * blk:
        blk //= 2
    run_units = (counts + ROW_UNIT - 1) // ROW_UNIT
    unit_end = jnp.cumsum(run_units, axis=1)
    expert_units = jnp.sum(run_units, axis=0)
    blk_units = blk // ROW_UNIT
    padded_units = (expert_units + blk_units - 1) // blk_units * blk_units
    expert_end = jnp.cumsum(padded_units)
    expert_start = expert_end - padded_units
    run_base = expert_start[None, :] + jnp.cumsum(run_units, axis=0) - run_units
    n_local = _round_up(TOP_K * tile + (ROW_UNIT - 1) * n_experts, PERM_CHUNK)
    n_units = n_local // ROW_UNIT

    def expand(ends, shift, n, stride):
        i = jnp.arange(n)
        step = shift[:, 1:] - shift[:, :-1]
        past = i[None, :, None] >= ends[:, None, :-1]
        value = stride * i[None, :] + shift[:, 0:1] + jnp.sum(jnp.where(past, step[:, None, :], 0), axis=-1)
        return jnp.where(i[None, :] < ends[:, -1:], value, 0)

    run_first = unit_end - run_units
    big = run_units // BIG_UNITS
    small = run_units - BIG_UNITS * big
    big_end = jnp.cumsum(big, axis=1)
    small_end = jnp.cumsum(small, axis=1)
    big_first = big_end - big
    small_first = small_end - small
    n_big_max = n_units // BIG_UNITS
    n_small_max = (BIG_UNITS - 1) * n_experts
    copy_lists = jnp.concatenate([
        expand(big_end, run_first - BIG_UNITS * big_first, n_big_max, BIG_UNITS),
        expand(big_end, run_base - BIG_UNITS * big_first, n_big_max, BIG_UNITS),
        expand(small_end, run_first + BIG_UNITS * big - small_first, n_small_max, 1),
        expand(small_end, run_base + BIG_UNITS * big - small_first, n_small_max, 1)], axis=1)
    tj = jnp.arange(n_experts * (blk_units - 1))
    tail_len = padded_units - expert_units
    tail_end = jnp.cumsum(tail_len)
    tshift = expert_start + expert_units - (tail_end - tail_len)
    tpast = tj[:, None] >= tail_end[None, :-1]
    tail_units = tj + tshift[0] + jnp.sum(jnp.where(tpast, (tshift[1:] - tshift[:-1])[None, :], 0), axis=-1)
    n_tail = tail_end[-1]
    tail_units = jnp.where(tj < n_tail, tail_units, 0)
    n_blocks = (n_tiles * tile * TOP_K + n_tiles * n_experts * (ROW_UNIT - 1) + n_experts * (blk - 1)) // blk
    block_e = jnp.minimum(jnp.sum((jnp.arange(n_blocks) * blk_units)[:, None] >= expert_end[None, :], axis=-1),
                          n_experts - 1)
    return {
        "tile": tile, "n_local": n_local, "n_rows": n_blocks * blk, "blk": blk,
        "n_big_max": n_big_max, "n_small_max": n_small_max,
        "n_big": big_end[:, -1].astype(I32), "n_small": small_end[:, -1].astype(I32),
        "copy_lists": copy_lists.astype(I32).reshape(n_tiles, 1, 2 * (n_big_max + n_small_max)),
        "tail_units": tail_units.astype(I32), "n_tail": n_tail.astype(I32).reshape(1),
        "block_e": block_e.astype(I32), "n_used": (expert_end[-1] // blk_units).astype(I32).reshape(1),
    }


def _moe_tail(h2, xn, lp, gate, cnt, w, norm_w, final_norm):
    batch, length, d_model = h2.shape
    n_tok = batch * length
    tile = length // cnt.shape[1]
    plan = _moe_plan(cnt, tile)
    xs = _dispatch(lp, gate, xn, plan, plan["n_rows"])
    ys = _experts(xs, plan["block_e"], plan["n_used"], w, plan["blk"])
    lp_cols = lp.transpose(0, 2, 1).reshape(n_tok, TOP_K)
    out = _combine(lp_cols, h2.reshape(n_tok, d_model), ys, norm_w, plan, final_norm)
    return out.reshape(batch, length, d_model)


def _layer(x, hist, s0, pos0, n_hist_valid, mk, mv, w, norm_w, final_norm):
    h1, new_hist, s_new = _mixer(x, hist, s0, pos0, n_hist_valid, w)
    h2, xn, lp, gate, cnt = _cross_router(h1, mk.astype(BF16), mv.astype(BF16), w)
    out = _moe_tail(h2, xn, lp, gate, cnt, w, norm_w, final_norm)
    return out, new_hist, s_new


def kernel(x_prompt, x_sample, mem_prompt, state_pool, state_ret, cache_mem_k, cache_mem_v, norm_mix_w, w_in, pool_w_grp, pool_scale, w_pool_out, ret_gn_w, w_ret_out, w_o, norm_mem_w, w_mk, w_mv, norm_cross_w, w_cq, w_co, norm_ffn_w, w_router, b_router, w_gu, b_gu, w_down, b_down, norm_final_w):
    depth = w_in.shape[0]
    batch_p = x_prompt.shape[0]
    m_len = mem_prompt.shape[1]
    d_model = x_prompt.shape[-1]
    hp, hs = x_prompt, x_sample
    norm_w = norm_final_w[None, :]
    mem_k_p, mem_v_p, pool_p, ret_p, pool_s, ret_s = [], [], [], [], [], []
    for l in range(depth):
        w = {
            "norm_mix_w": norm_mix_w[l][None, :], "w_in": w_in[l].astype(BF16),
            "pool_w_grp": pool_w_grp[l].astype(BF16), "pool_scale": pool_scale[l][None, :],
            "w_pool_out": w_pool_out[l].astype(BF16), "ret_gn_w": ret_gn_w[l][None, :],
            "w_ret_out": w_ret_out[l].astype(BF16), "w_o": w_o[l].astype(BF16),
            "norm_mem_w": norm_mem_w[l][None, :], "w_mk": w_mk[l].astype(BF16), "w_mv": w_mv[l].astype(BF16),
            "norm_cross_w": norm_cross_w[l][None, :], "w_cq": w_cq[l].astype(BF16), "w_co": w_co[l].astype(BF16),
            "norm_ffn_w": norm_ffn_w[l][None, :], "w_router_t": w_router[l].T.astype(BF16),
            "b_router_c": b_router[l][:, None],
            "w_gu": w_gu[l], "b_gu": b_gu[l][:, None, :],
            "w_down": w_down[l], "b_down": b_down[l][:, None, :],
        }
        last = l == depth - 1
        mk, mv = _mem_kv(mem_prompt, w)
        zero_hist = jnp.zeros((batch_p, POOL_HIST, POOL_WIDTH), F32)
        zero_state = jnp.zeros((batch_p, RET_HEADS, RET_DK, RET_DV), F32)
        hp, hist_p, s_p = _layer(hp, zero_hist, zero_state, 0, 0, mk, mv, w, norm_w, last)
        mem_k_p.append(mk.reshape(batch_p, m_len, MEM_HEADS, d_model // MEM_HEADS))
        mem_v_p.append(mv.reshape(batch_p, m_len, MEM_HEADS, d_model // MEM_HEADS))
        pool_p.append(hist_p)
        ret_p.append(s_p)
        ck = cache_mem_k[l].reshape(cache_mem_k.shape[1], m_len, d_model)
        cv = cache_mem_v[l].reshape(cache_mem_v.shape[1], m_len, d_model)
        hs, hist_s, s_s = _layer(hs, state_pool[l], state_ret[l], PAST_LEN, POOL_HIST, ck, cv, w, norm_w, last)
        pool_s.append(hist_s)
        ret_s.append(s_s)
    return (hp, hs, jnp.stack(mem_k_p), jnp.stack(mem_v_p), jnp.stack(pool_p), jnp.stack(ret_p),
            jnp.stack(pool_s), jnp.stack(ret_s))
```

```python
import functools

import jax
import jax.numpy as jnp
from jax import lax
from jax.experimental import pallas as pl
from jax.experimental.pallas import tpu as pltpu

F32 = jnp.float32
BF16 = jnp.bfloat16
I32 = jnp.int32
U32 = jnp.uint32
HIGH_HALF = 0xFFFF0000

EPS = 1e-6
PAST_LEN = 1024
POOL_WINDOWS = (2, 4, 8, 16)
POOL_GROUP_DIM = 128
POOL_WIDTH = POOL_GROUP_DIM * len(POOL_WINDOWS)
POOL_HIST = max(POOL_WINDOWS) - 1
RET_HEADS = 4
RET_DK = 128
RET_DV = 256
ROPE_BASE = 10000.0
MEM_HEADS = 4
TOP_K = 4
SWIGLU_LIMIT = 7.0
SWIGLU_ALPHA = 1.702

SUBLANES = 8
LANES = 128
VMEM_LIMIT_BYTES = 56 * 1024 * 1024

MIXER_TILE = 512
RET_CHUNK = 256
CROSS_TILE = 512
CROSS_CHUNK = 512
EXPERT_BLOCK = 512
MIN_EXPERT_BLOCK = 128
ROW_UNIT = SUBLANES
PERM_CHUNK = 256
BIG_UNITS = 4
DMA_UNROLL = 4


def _cparams(n_axes):
    return pltpu.CompilerParams(dimension_semantics=("arbitrary",) * n_axes,
                                vmem_limit_bytes=VMEM_LIMIT_BYTES)


def _const_spec(shape):
    nd = len(shape)
    return pl.BlockSpec(shape, lambda *_: (0,) * nd, pipeline_mode=pl.Buffered(1))


def _round_up(n, m):
    return (n + m - 1) // m * m


def _rms(x32, w_row):
    ms = jnp.mean(x32 * x32, axis=-1, keepdims=True)
    return x32 * lax.rsqrt(ms + EPS) * w_row


def _dot(a, b):
    return jnp.dot(a, b, preferred_element_type=F32)


def _dot_nt(a, b):
    return lax.dot_general(a, b, (((1,), (1,)), ((), ())), preferred_element_type=F32)


def _dot_tn(a, b):
    return lax.dot_general(a, b, (((0,), (0,)), ((), ())), preferred_element_type=F32)


def _rope_kernel(inv_ref, sign_ref, cos_ref, sin_ref, *, pos0, tile):
    i = pl.program_id(0)
    pos = (lax.broadcasted_iota(I32, (tile, RET_DK), 0) + (pos0 + i * tile)).astype(F32)
    ang = pos * inv_ref[...]
    cos_ref[...] = jnp.cos(ang)
    sin_ref[...] = jnp.sin(ang) * sign_ref[...]


def _rope_tables(length, pos0):
    half = RET_DK // 2
    inv = 1.0 / (ROPE_BASE ** (jnp.arange(half, dtype=F32) / half))
    inv2 = jnp.concatenate([inv, inv])[None, :]
    sign = jnp.concatenate([-jnp.ones((half,), F32), jnp.ones((half,), F32)])[None, :]
    tile = min(length, 512)
    assert length % tile == 0
    return pl.pallas_call(
        functools.partial(_rope_kernel, pos0=pos0, tile=tile),
        out_shape=(jax.ShapeDtypeStruct((length, RET_DK), F32),) * 2,
        grid=(length // tile,),
        in_specs=[pl.BlockSpec((1, RET_DK), lambda i: (0, 0))] * 2,
        out_specs=(pl.BlockSpec((tile, RET_DK), lambda i: (i, 0)),) * 2,
        compiler_params=_cparams(1),
        name="rope_tables",
    )(inv2, sign)


def _mixer_kernel(x_ref, hist_ref, s0_ref, cos_ref, sin_ref, nmw_ref, win_ref, pwg_ref, pscale_ref,
                  wpo_ref, gnw_ref, wro_ref, wo_ref, dec_ref, qd_ref, kd_ref, gc_ref,
                  h_ref, hist_out_ref, s_out_ref,
                  ubuf, s_scr, z_scr, r_scr, yp_scr, *, tile, chunk, n_hist_valid, d_model):
    t = pl.program_id(1)
    n_t = pl.num_programs(1)
    q_off = POOL_WIDTH
    k_off = q_off + RET_HEADS * RET_DK
    v_off = k_off + RET_HEADS * RET_DK
    g_off = v_off + RET_HEADS * RET_DV
    ap_off = g_off + RET_HEADS * RET_DV
    ar_off = ap_off + d_model

    @pl.when(t == 0)
    def _():
        ubuf[0:1, :] = jnp.zeros((1, POOL_WIDTH), F32)
        ubuf[1:1 + POOL_HIST, :] = hist_ref[0]
        s_scr[...] = s0_ref[0]

    n_chunks = tile // chunk
    hist_rows = POOL_HIST + 1
    for c in range(n_chunks):
        rows = slice(c * chunk, (c + 1) * chunk)
        xn = _rms(x_ref[0, rows, :], nmw_ref[...]).astype(BF16)
        z_scr[rows, :] = _dot(xn, win_ref[...])
        ubuf[hist_rows + c * chunk:hist_rows + (c + 1) * chunk, :] = z_scr[rows, 0:POOL_WIDTH]

    for c in range(n_chunks):
        rows = slice(c * chunk, (c + 1) * chunk)
        full = ubuf[c * chunk:c * chunk + hist_rows + chunk, :]
        pos = t * tile + c * chunk + lax.broadcasted_iota(I32, (chunk, 1), 0)
        ys = []
        for g, w in enumerate(POOL_WINDOWS):
            f = full[:, g * POOL_GROUP_DIM:(g + 1) * POOL_GROUP_DIM]
            s = f
            shift = 1
            while shift < w:
                s = s + pltpu.roll(s, shift, 0)
                shift *= 2
            cnt = jnp.minimum(w, pos + 1 + n_hist_valid).astype(F32)
            d = s[hist_rows:, :] / cnt - f[hist_rows:, :]
            ys.append(_dot(d.astype(BF16), pwg_ref[g]))
        yp_scr[rows, :] = (jnp.concatenate(ys, axis=-1) * pscale_ref[...]).astype(BF16)
    ubuf[1:1 + POOL_HIST, :] = ubuf[tile + 1:tile + 1 + POOL_HIST, :]

    for c in range(n_chunks):
        rows = slice(c * chunk, (c + 1) * chunk)
        cosc = cos_ref[rows, :]
        sinc = sin_ref[rows, :]
        for h in range(RET_HEADS):
            qh = z_scr[rows, q_off + h * RET_DK:q_off + (h + 1) * RET_DK]
            kh = z_scr[rows, k_off + h * RET_DK:k_off + (h + 1) * RET_DK]
            vb = z_scr[rows, v_off + h * RET_DV:v_off + (h + 1) * RET_DV].astype(BF16)
            qr = qh * cosc + pltpu.roll(qh, RET_DK // 2, 1) * sinc
            kr = (kh * cosc + pltpu.roll(kh, RET_DK // 2, 1) * sinc) * (RET_DK ** -0.5)
            qb = qr.astype(BF16)
            kb = kr.astype(BF16)
            scores = _dot_nt(qb, kb) * dec_ref[h]
            state = s_scr[h]
            o = _dot(scores.astype(BF16), vb) + _dot(qb, state.astype(BF16)) * qd_ref[h]
            s_scr[h] = gc_ref[h] * state + _dot_tn((kr * kd_ref[h]).astype(BF16), vb)
            mu = jnp.mean(o, axis=-1, keepdims=True)
            dlt = o - mu
            var = jnp.mean(dlt * dlt, axis=-1, keepdims=True)
            on = dlt * lax.rsqrt(var + EPS) * gnw_ref[:, h * RET_DV:(h + 1) * RET_DV]
            gate = z_scr[rows, g_off + h * RET_DV:g_off + (h + 1) * RET_DV]
            r_scr[rows, h * RET_DV:(h + 1) * RET_DV] = (on * (gate * jax.nn.sigmoid(gate))).astype(BF16)

    for c in range(n_chunks):
        rows = slice(c * chunk, (c + 1) * chunk)
        branch_pool = _dot(yp_scr[rows, :], wpo_ref[...])
        branch_ret = _dot(r_scr[rows, :], wro_ref[...])
        merged = (jax.nn.sigmoid(z_scr[rows, ap_off:ap_off + d_model]) * branch_pool
                  + jax.nn.sigmoid(z_scr[rows, ar_off:ar_off + d_model]) * branch_ret)
        h_ref[0, rows, :] = x_ref[0, rows, :] + _dot(merged.astype(BF16), wo_ref[...])

    @pl.when(t == n_t - 1)
    def _():
        hist_out_ref[0] = ubuf[1:1 + POOL_HIST, :]
        s_out_ref[0] = s_scr[...]


def _ret_tables(chunk):
    lg = jnp.log(1.0 - 2.0 ** (-5.0 - jnp.arange(RET_HEADS, dtype=F32)))
    idx = jnp.arange(chunk)
    rel = idx[:, None] - idx[None, :]
    decay = jnp.where(rel[None] >= 0, jnp.exp(jnp.maximum(rel, 0)[None].astype(F32) * lg[:, None, None]), 0.0)
    q_decay = jnp.exp((idx + 1).astype(F32)[None, :] * lg[:, None])[:, :, None]
    k_decay = jnp.exp((chunk - 1 - idx).astype(F32)[None, :] * lg[:, None])[:, :, None]
    g_chunk = jnp.exp(chunk * lg)
    return decay, q_decay, k_decay, g_chunk


def _mixer(x, hist, s0, pos0, n_hist_valid, w):
    batch, length, d_model = x.shape
    tile = min(MIXER_TILE, length)
    chunk = min(RET_CHUNK, tile)
    assert length % tile == 0 and tile % chunk == 0 and tile >= POOL_HIST + 1
    in_width = w["w_in"].shape[1]
    cos, sin = _rope_tables(length, pos0)
    decay, q_decay, k_decay, g_chunk = _ret_tables(chunk)
    kern = functools.partial(_mixer_kernel, tile=tile, chunk=chunk, n_hist_valid=n_hist_valid, d_model=d_model)
    return pl.pallas_call(
        kern,
        out_shape=(jax.ShapeDtypeStruct(x.shape, F32),
                   jax.ShapeDtypeStruct(hist.shape, F32),
                   jax.ShapeDtypeStruct(s0.shape, F32)),
        grid=(batch, length // tile),
        in_specs=[
            pl.BlockSpec((1, tile, d_model), lambda b, t: (b, t, 0)),
            pl.BlockSpec((1, POOL_HIST, POOL_WIDTH), lambda b, t: (b, 0, 0)),
            pl.BlockSpec((1, RET_HEADS, RET_DK, RET_DV), lambda b, t: (b, 0, 0, 0)),
            pl.BlockSpec((tile, RET_DK), lambda b, t: (t, 0)),
            pl.BlockSpec((tile, RET_DK), lambda b, t: (t, 0)),
            _const_spec((1, d_model)),
            _const_spec((d_model, in_width)),
            _const_spec(w["pool_w_grp"].shape),
            _const_spec((1, POOL_WIDTH)),
            _const_spec((POOL_WIDTH, d_model)),
            _const_spec((1, RET_HEADS * RET_DV)),
            _const_spec((RET_HEADS * RET_DV, d_model)),
            _const_spec((d_model, d_model)),
            _const_spec(decay.shape),
            _const_spec(q_decay.shape),
            _const_spec(k_decay.shape),
            pl.BlockSpec(memory_space=pltpu.SMEM),
        ],
        out_specs=(
            pl.BlockSpec((1, tile, d_model), lambda b, t: (b, t, 0)),
            pl.BlockSpec((1, POOL_HIST, POOL_WIDTH), lambda b, t: (b, 0, 0)),
            pl.BlockSpec((1, RET_HEADS, RET_DK, RET_DV), lambda b, t: (b, 0, 0, 0)),
        ),
        scratch_shapes=[
            pltpu.VMEM((POOL_HIST + 1 + tile, POOL_WIDTH), F32),
            pltpu.VMEM((RET_HEADS, RET_DK, RET_DV), F32),
            pltpu.VMEM((tile, in_width), F32),
            pltpu.VMEM((tile, RET_HEADS * RET_DV), BF16),
            pltpu.VMEM((tile, POOL_WIDTH), BF16),
        ],
        compiler_params=_cparams(2),
        name="mixer",
    )(x, hist, s0, cos, sin, w["norm_mix_w"], w["w_in"], w["pool_w_grp"], w["pool_scale"], w["w_pool_out"],
      w["ret_gn_w"], w["w_ret_out"], w["w_o"], decay, q_decay, k_decay, g_chunk)


def _mem_kv_kernel(mem_ref, nw_ref, wk_ref, wv_ref, k_ref, v_ref):
    mn = _rms(mem_ref[0], nw_ref[...]).astype(BF16)
    k_ref[0] = _dot(mn, wk_ref[...])
    v_ref[0] = _dot(mn, wv_ref[...])


def _mem_kv(mem, w):
    batch, m_len, d_model = mem.shape
    out = jax.ShapeDtypeStruct((batch, m_len, d_model), F32)
    blk = pl.BlockSpec((1, m_len, d_model), lambda b: (b, 0, 0))
    return pl.pallas_call(
        _mem_kv_kernel,
        out_shape=(out, out),
        grid=(batch,),
        in_specs=[blk, _const_spec((1, d_model)), _const_spec((d_model, d_model)), _const_spec((d_model, d_model))],
        out_specs=(blk, blk),
        compiler_params=_cparams(1),
        name="mem_kv",
    )(mem, w["norm_mem_w"], w["w_mk"], w["w_mv"])


def _cross_router_kernel(h_ref, mk_ref, mv_ref, ncw_ref, wcq_ref, wco_ref, nfw_ref, wrt_ref, brt_ref,
                         h2_ref, xn_ref, lp_ref, gate_ref, cnt_ref,
                         o_scr, *, tile, chunk, head_dim, n_experts):
    e_iota = lax.broadcasted_iota(I32, (n_experts, chunk), 0)
    r_iota = lax.broadcasted_iota(I32, (chunk, chunk), 0)
    c_iota = lax.broadcasted_iota(I32, (chunk, chunk), 1)
    earlier = (r_iota < c_iota).astype(BF16)
    counts = jnp.zeros((n_experts, 1), F32)
    chunk_sels, chunk_prior = [], []
    for c in range(tile // chunk):
        rows = slice(c * chunk, (c + 1) * chunk)
        h1 = h_ref[0, rows, :]
        hn = _rms(h1, ncw_ref[...]).astype(BF16)
        q = _dot(hn, wcq_ref[...]).astype(BF16)
        for hd in range(MEM_HEADS):
            cols = slice(hd * head_dim, (hd + 1) * head_dim)
            s = _dot_nt(q[:, cols], mk_ref[0, :, cols]) * (head_dim ** -0.5)
            e = jnp.exp(s - jnp.max(s, axis=-1, keepdims=True))
            p = e / jnp.sum(e, axis=-1, keepdims=True)
            o_scr[rows, cols] = _dot(p.astype(BF16), mv_ref[0, :, cols]).astype(BF16)
        h2 = h1 + _dot(o_scr[rows, :], wco_ref[...])
        h2_ref[0, rows, :] = h2
        xn_ref[0, rows, :] = _rms(h2, nfw_ref[...]).astype(BF16)

    for c in range(tile // chunk):
        rows = slice(c * chunk, (c + 1) * chunk)
        work = _dot_nt(wrt_ref[...], xn_ref[0, rows, :]) + brt_ref[...]
        vals, sels = [], []
        for _ in range(TOP_K):
            m = jnp.max(work, axis=0, keepdims=True)
            first = jnp.min(jnp.where(work == m, e_iota, n_experts), axis=0, keepdims=True)
            sel = e_iota == first
            vals.append(m)
            sels.append(sel)
            work = jnp.where(sel, -jnp.inf, work)
        exps = [jnp.exp(v - vals[0]) for v in vals]
        denom = exps[0] + exps[1] + exps[2] + exps[3]
        for k in range(TOP_K):
            gate_ref[0, k:k + 1, rows] = exps[k] / denom
        assigned = jnp.zeros((n_experts, chunk), F32)
        for sel in sels:
            assigned = assigned + sel.astype(F32)
        chunk_prior.append(counts + _dot(assigned.astype(BF16), earlier))
        chunk_sels.append(sels)
        counts = counts + jnp.sum(assigned, axis=1, keepdims=True)

    run = jnp.floor((counts + (ROW_UNIT - 1)) * (1.0 / ROW_UNIT)) * ROW_UNIT
    er = lax.broadcasted_iota(I32, (n_experts, n_experts), 0)
    ec = lax.broadcasted_iota(I32, (n_experts, n_experts), 1)
    run_start = _dot((ec < er).astype(BF16), jnp.broadcast_to(run, (n_experts, LANES)).astype(BF16))[:, 0:1]
    for c in range(tile // chunk):
        rows = slice(c * chunk, (c + 1) * chunk)
        place = run_start + chunk_prior[c]
        for k in range(TOP_K):
            lp_ref[0, k:k + 1, rows] = jnp.sum(jnp.where(chunk_sels[c][k], place, 0.0),
                                               axis=0, keepdims=True).astype(I32)
    cnt_ref[0, 0] = jnp.broadcast_to(counts, (n_experts, LANES)).astype(I32)


def _cross_router(h1, mk, mv, w):
    batch, length, d_model = h1.shape
    m_len = mk.shape[1]
    head_dim = d_model // MEM_HEADS
    n_experts = w["w_router_t"].shape[0]
    tile = min(CROSS_TILE, length)
    assert length % tile == 0
    n_t = length // tile
    chunk = min(CROSS_CHUNK, tile)
    assert tile % chunk == 0
    kern = functools.partial(_cross_router_kernel, tile=tile, chunk=chunk, head_dim=head_dim, n_experts=n_experts)
    tok = pl.BlockSpec((1, tile, d_model), lambda b, t: (b, t, 0))
    mem = pl.BlockSpec((1, m_len, d_model), lambda b, t: (b, 0, 0))
    small = pl.BlockSpec((1, TOP_K, tile), lambda b, t: (b, 0, t))
    return pl.pallas_call(
        kern,
        out_shape=(jax.ShapeDtypeStruct(h1.shape, F32),
                   jax.ShapeDtypeStruct(h1.shape, BF16),
                   jax.ShapeDtypeStruct((batch, TOP_K, length), I32),
                   jax.ShapeDtypeStruct((batch, TOP_K, length), F32),
                   jax.ShapeDtypeStruct((batch, n_t, n_experts, LANES), I32)),
        grid=(batch, n_t),
        in_specs=[tok, mem, mem,
                  _const_spec((1, d_model)), _const_spec((d_model, d_model)), _const_spec((d_model, d_model)),
                  _const_spec((1, d_model)), _const_spec((n_experts, d_model)), _const_spec((n_experts, 1))],
        out_specs=(tok, tok, small, small,
                   pl.BlockSpec((1, 1, n_experts, LANES), lambda b, t: (b, t, 0, 0))),
        scratch_shapes=[pltpu.VMEM((tile, d_model), BF16)],
        compiler_params=_cparams(2),
        name="cross_router",
    )(h1, mk, mv, w["norm_cross_w"], w["w_cq"], w["w_co"], w["norm_ffn_w"], w["w_router_t"], w["b_router_c"])


def _unit_copy(src_ref, dst_ref, src_unit, dst_unit, sem, n_units=1):
    def rows_of(unit):
        start = unit * ROW_UNIT
        return pl.ds(start if isinstance(start, int) else pl.multiple_of(start, ROW_UNIT), n_units * ROW_UNIT)

    return pltpu.make_async_copy(src_ref.at[rows_of(src_unit)], dst_ref.at[rows_of(dst_unit)], sem)


def _run_copies(lists_ref, local_ref, global_ref, sem, n_big_max, n_small_max, to_global):
    def make(local_at, global_at, n_units):
        def copy_of(j):
            loc, glo = lists_ref[0, 0, local_at + j], lists_ref[0, 0, global_at + j]
            if to_global:
                return _unit_copy(local_ref, global_ref, loc, glo, sem, n_units)
            return _unit_copy(global_ref, local_ref, glo, loc, sem, n_units)
        return copy_of

    return (make(0, n_big_max, BIG_UNITS),
            make(2 * n_big_max, 2 * n_big_max + n_small_max, 1))


def _for_units(n, body):
    def group(g, c):
        for lane in range(DMA_UNROLL):
            body(g * DMA_UNROLL + lane, lane)
        return c

    n_groups = lax.div(n, DMA_UNROLL)
    lax.fori_loop(0, n_groups, group, 0)

    def rest(j, c):
        body(j, 0)
        return c

    lax.fori_loop(n_groups * DMA_UNROLL, n, rest, 0)


def _start_units(n, copy_of):
    _for_units(n, lambda j, lane: copy_of(j).start(priority=lane % 2))


def _wait_units(n, copy_of):
    _for_units(n, lambda j, lane: copy_of(j).wait())


def _pack_pairs(a, b):
    ua = lax.bitcast_convert_type(a, U32)
    ub = lax.bitcast_convert_type(b, U32)
    return lax.shift_right_logical(ua, U32(16)) | (ub & U32(HIGH_HALF))


def _unpack_pairs(u):
    a = lax.bitcast_convert_type(lax.shift_left(u, U32(16)), F32)
    b = lax.bitcast_convert_type(u & U32(HIGH_HALF), F32)
    return a, b


def _dispatch_kernel(nbig_ref, nsmall_ref, tail_ref, ntail_ref, lp_ref, gate_ref, xn_ref, lists_ref, listsp_ref,
                     *refs, tile, n_local, d_model, n_big_max, n_small_max, first_group):
    xs_ref, rows, zero_buf, sems, tail_sem = refs if first_group else refs[1:]
    i = pl.program_id(0)
    n_i = pl.num_programs(0)
    slot = lax.rem(i, 2)

    def for_tile(step, lists, at_slot, action):
        big, small = _run_copies(lists, rows.at[at_slot], xs_ref, sems.at[at_slot], n_big_max, n_small_max, True)
        action(nbig_ref[step], big)
        action(nsmall_ref[step], small)

    if first_group:
        @pl.when(i == 0)
        def _():
            zero_buf[...] = jnp.zeros(zero_buf.shape, U32)

            def tail_copy(j):
                return _unit_copy(zero_buf, xs_ref, 0, tail_ref[j], tail_sem)

            _start_units(ntail_ref[0], tail_copy)
            _wait_units(ntail_ref[0], tail_copy)

    half = d_model // 2
    xn = xn_ref[0]
    lps = [lp_ref[0, k:k + 1, :] for k in range(TOP_K)]
    gates = [gate_ref[0, k:k + 1, :] for k in range(TOP_K)]
    for c in range(n_local // PERM_CHUNK):
        r_iota = lax.broadcasted_iota(I32, (PERM_CHUNK, tile), 0) + c * PERM_CHUNK
        gmat = jnp.full((PERM_CHUNK, tile), -1.0, F32)
        for k in range(TOP_K):
            gmat = jnp.where(r_iota == lps[k], gates[k], gmat)
        perm = jnp.where(gmat >= 0.0, 1.0, 0.0).astype(BF16)
        wmat = jnp.maximum(gmat, 0.0)
        rs = pl.ds(c * PERM_CHUNK, PERM_CHUNK)
        xr = _dot(perm, xn)
        rows[slot, rs, 0:half] = _pack_pairs(xr[:, 0:half], xr[:, half:d_model])
        rows[slot, rs, half:half + LANES] = lax.bitcast_convert_type(
            jnp.broadcast_to(jnp.sum(wmat, axis=1, keepdims=True), (PERM_CHUNK, LANES)), U32)

    for_tile(i, lists_ref, slot, _start_units)

    @pl.when(i > 0)
    def _():
        for_tile(i - 1, listsp_ref, 1 - slot, _wait_units)

    @pl.when(i == n_i - 1)
    def _():
        for_tile(i, lists_ref, slot, _wait_units)


def _dispatch(lp, gate, xn, layout, plan, xs_prev):
    batch, length, d_model = xn.shape
    tile, n_local = plan["tile"], plan["n_local"]
    n_big_max, n_small_max = plan["n_big_max"], plan["n_small_max"]
    n_list = 2 * (n_big_max + n_small_max)
    n_t = length // tile
    width = d_model // 2 + LANES
    first_group = xs_prev is None
    small = pl.BlockSpec((1, TOP_K, tile), lambda i, *_: (i // n_t, 0, i % n_t))
    n_prefetch = 4
    in_specs = [small, small,
                pl.BlockSpec((1, tile, d_model), lambda i, *_: (i // n_t, i % n_t, 0)),
                pl.BlockSpec((1, 1, n_list), lambda i, *_: (i, 0, 0), memory_space=pltpu.SMEM),
                pl.BlockSpec((1, 1, n_list), lambda i, *_: (jnp.maximum(i - 1, 0), 0, 0),
                             memory_space=pltpu.SMEM)]
    args = [plan["n_big"], plan["n_small"], layout["tail_units"], layout["n_tail"], lp, gate, xn,
            plan["copy_lists"], plan["copy_lists"]]
    aliases = {}
    if not first_group:
        aliases = {len(args): 0}
        in_specs.append(pl.BlockSpec(memory_space=pl.ANY))
        args.append(xs_prev)
    grid_spec = pltpu.PrefetchScalarGridSpec(
        num_scalar_prefetch=n_prefetch,
        grid=(batch * n_t,),
        in_specs=in_specs,
        out_specs=pl.BlockSpec(memory_space=pl.ANY),
        scratch_shapes=[pltpu.VMEM((2, n_local, width), U32), pltpu.VMEM((ROW_UNIT, width), U32),
                        pltpu.SemaphoreType.DMA((2,)), pltpu.SemaphoreType.DMA(())],
    )
    return pl.pallas_call(
        functools.partial(_dispatch_kernel, tile=tile, n_local=n_local, d_model=d_model,
                          n_big_max=n_big_max, n_small_max=n_small_max, first_group=first_group),
        out_shape=jax.ShapeDtypeStruct((layout["n_rows"], width), U32),
        grid_spec=grid_spec,
        input_output_aliases=aliases,
        compiler_params=_cparams(1),
        name="dispatch",
    )(*args)


def _expert_kernel(be_ref, nb_ref, xs_ref, wgu_ref, bgu_ref, wd_ref, bd_ref, ys_ref, wgu_bf, wd_bf,
                   *, d_model, d_expert):
    b = pl.program_id(0)

    @pl.when(b < nb_ref[0])
    def _():
        @pl.when((b == 0) | (be_ref[b] != be_ref[jnp.maximum(b - 1, 0)]))
        def _():
            wgu_bf[...] = wgu_ref[0].astype(BF16)
            wd_bf[...] = wd_ref[0].astype(BF16)

        half = d_model // 2
        xa, xb = _unpack_pairs(xs_ref[:, 0:half])
        row_gate = lax.bitcast_convert_type(xs_ref[:, half:half + 1], F32)
        gu = _dot(jnp.concatenate([xa, xb], axis=-1).astype(BF16), wgu_bf[...]) + bgu_ref[0]
        gl = jnp.minimum(gu[:, :d_expert], SWIGLU_LIMIT)
        up = jnp.clip(gu[:, d_expert:], -SWIGLU_LIMIT, SWIGLU_LIMIT)
        act = (gl * jax.nn.sigmoid(SWIGLU_ALPHA * gl) * (up + 1.0)).astype(BF16)
        y = ((_dot(act, wd_bf[...]) + bd_ref[0]) * row_gate).astype(BF16).astype(F32)
        ys_ref[...] = _pack_pairs(y[:, 0:half], y[:, half:d_model])


def _experts(xs, block_e, n_used, w, blk):
    n_rows, width = xs.shape
    d_model = 2 * (width - LANES)
    d_expert = w["w_down"].shape[1]

    def row_map(b, be, nb):
        return (jnp.minimum(b, nb[0] - 1), 0)

    def exp_map(b, be, nb):
        return (be[b], 0, 0)

    grid_spec = pltpu.PrefetchScalarGridSpec(
        num_scalar_prefetch=2,
        grid=(n_rows // blk,),
        in_specs=[pl.BlockSpec((blk, width), row_map),
                  pl.BlockSpec((1, d_model, 2 * d_expert), exp_map),
                  pl.BlockSpec((1, 1, 2 * d_expert), exp_map),
                  pl.BlockSpec((1, d_expert, d_model), exp_map),
                  pl.BlockSpec((1, 1, d_model), exp_map)],
        out_specs=pl.BlockSpec((blk, d_model // 2), row_map),
        scratch_shapes=[pltpu.VMEM((d_model, 2 * d_expert), BF16), pltpu.VMEM((d_expert, d_model), BF16)],
    )
    return pl.pallas_call(
        functools.partial(_expert_kernel, d_model=d_model, d_expert=d_expert),
        out_shape=jax.ShapeDtypeStruct((n_rows, d_model // 2), U32),
        grid_spec=grid_spec,
        compiler_params=_cparams(1),
        name="experts",
    )(block_e, n_used, xs, w["w_gu"], w["b_gu"], w["w_down"], w["b_down"])


def _combine_kernel(nbig_ref, nsmall_ref, lpc_ref, h2_ref, nw_ref, lists_ref, listsn_ref, ys_ref, out_ref,
                    rows, sems, *, tile, n_local, final_norm, n_big_max, n_small_max):
    i = pl.program_id(0)
    n_i = pl.num_programs(0)
    slot = lax.rem(i, 2)

    def for_tile(step, lists, at_slot, action):
        big, small = _run_copies(lists, rows.at[at_slot], ys_ref, sems.at[at_slot], n_big_max, n_small_max, False)
        action(nbig_ref[step], big)
        action(nsmall_ref[step], small)

    @pl.when(i == 0)
    def _():
        rows[...] = jnp.zeros(rows.shape, U32)
        for_tile(0, lists_ref, 0, _start_units)

    @pl.when(i + 1 < n_i)
    def _():
        for_tile(i + 1, listsn_ref, 1 - slot, _start_units)

    for_tile(i, lists_ref, slot, _wait_units)

    lps = [lpc_ref[:, k:k + 1] for k in range(TOP_K)]
    half = out_ref.shape[1] // 2
    ya = jnp.zeros((tile, half), F32)
    yb = jnp.zeros((tile, half), F32)
    for c in range(n_local // PERM_CHUNK):
        c_iota = lax.broadcasted_iota(I32, (tile, PERM_CHUNK), 1) + c * PERM_CHUNK
        sel = jnp.zeros((tile, PERM_CHUNK), F32)
        for k in range(TOP_K):
            sel = jnp.where(c_iota == lps[k], 1.0, sel)
        sel = sel.astype(BF16)
        ra, rb = _unpack_pairs(rows[slot, pl.ds(c * PERM_CHUNK, PERM_CHUNK), :])
        ya = ya + _dot(sel, ra.astype(BF16))
        yb = yb + _dot(sel, rb.astype(BF16))
    h3 = h2_ref[...] + jnp.concatenate([ya, yb], axis=-1)
    out_ref[...] = _rms(h3, nw_ref[...]) if final_norm else h3


def _combine(lp_cols, h2, ys, norm_w, plan, final_norm):
    n_tok, d_model = h2.shape
    tile, n_local = plan["tile"], plan["n_local"]
    n_big_max, n_small_max = plan["n_big_max"], plan["n_small_max"]
    n_list = 2 * (n_big_max + n_small_max)
    n_tiles = n_tok // tile
    grid_spec = pltpu.PrefetchScalarGridSpec(
        num_scalar_prefetch=2,
        grid=(n_tiles,),
        in_specs=[pl.BlockSpec((tile, TOP_K), lambda i, *_: (i, 0)),
                  pl.BlockSpec((tile, d_model), lambda i, *_: (i, 0)),
                  _const_spec((1, d_model)),
                  pl.BlockSpec((1, 1, n_list), lambda i, *_: (i, 0, 0), memory_space=pltpu.SMEM),
                  pl.BlockSpec((1, 1, n_list), lambda i, *_: (jnp.minimum(i + 1, n_tiles - 1), 0, 0),
                               memory_space=pltpu.SMEM),
                  pl.BlockSpec(memory_space=pl.ANY)],
        out_specs=pl.BlockSpec((tile, d_model), lambda i, *_: (i, 0)),
        scratch_shapes=[pltpu.VMEM((2, n_local, d_model // 2), U32), pltpu.SemaphoreType.DMA((2,))],
    )
    return pl.pallas_call(
        functools.partial(_combine_kernel, tile=tile, n_local=n_local, final_norm=final_norm,
                          n_big_max=n_big_max, n_small_max=n_small_max),
        out_shape=jax.ShapeDtypeStruct((n_tok, d_model), F32),
        grid_spec=grid_spec,
        compiler_params=_cparams(1),
        name="combine",
    )(plan["n_big"], plan["n_small"], lp_cols, h2, norm_w, plan["copy_lists"], plan["copy_lists"], ys)


def _moe_plan(cnts, tiles):
    n_experts = cnts[0].shape[2]
    group_counts = [c[..., 0].reshape(-1, n_experts) for c in cnts]
    counts = jnp.concatenate(group_counts, axis=0)
    n_tiles = counts.shape[0]
    n_assigned = sum(g.shape[0] * t for g, t in zip(group_counts, tiles)) * TOP_K
    blk = EXPERT_BLOCK
    while blk > MIN_EXPERT_BLOCK and n_assigned < n_experts * blk:
        blk //= 2
    run_units = (counts + ROW_UNIT - 1) // ROW_UNIT
    expert_units = jnp.sum(run_units, axis=0)
    blk_units = blk // ROW_UNIT
    padded_units = (expert_units + blk_units - 1) // blk_units * blk_units
    expert_end = jnp.cumsum(padded_units)
    expert_start = expert_end - padded_units
    run_base = expert_start[None, :] + jnp.cumsum(run_units, axis=0) - run_units
    tj = jnp.arange(n_experts * (blk_units - 1))
    tail_len = padded_units - expert_units
    tail_end = jnp.cumsum(tail_len)
    tshift = expert_start + expert_units - (tail_end - tail_len)
    tpast = tj[:, None] >= tail_end[None, :-1]
    tail_units = tj + tshift[0] + jnp.sum(jnp.where(tpast, (tshift[1:] - tshift[:-1])[None, :], 0), axis=-1)
    n_tail = tail_end[-1]
    tail_units = jnp.where(tj < n_tail, tail_units, 0)
    n_blocks = (n_assigned + n_tiles * n_experts * (ROW_UNIT - 1) + n_experts * (blk - 1)) // blk
    block_e = jnp.minimum(jnp.sum((jnp.arange(n_blocks) * blk_units)[:, None] >= expert_end[None, :], axis=-1),
                          n_experts - 1)
    layout = {
        "n_rows": n_blocks * blk, "blk": blk,
        "tail_units": tail_units.astype(I32), "n_tail": n_tail.astype(I32).reshape(1),
        "block_e": block_e.astype(I32), "n_used": (expert_end[-1] // blk_units).astype(I32).reshape(1),
    }
    plans, first = [], 0
    for g, tile in zip(group_counts, tiles):
        last = first + g.shape[0]
        plans.append(_group_plan(run_units[first:last], run_base[first:last], tile))
        first = last
    return layout, plans


def _group_plan(run_units, run_base, tile):
    n_tiles, n_experts = run_units.shape
    unit_end = jnp.cumsum(run_units, axis=1)
    n_local = _round_up(TOP_K * tile + (ROW_UNIT - 1) * n_experts, PERM_CHUNK)
    n_units = n_local // ROW_UNIT

    def expand(ends, shift, n, stride):
        i = jnp.arange(n)
        step = shift[:, 1:] - shift[:, :-1]
        past = i[None, :, None] >= ends[:, None, :-1]
        value = stride * i[None, :] + shift[:, 0:1] + jnp.sum(jnp.where(past, step[:, None, :], 0), axis=-1)
        return jnp.where(i[None, :] < ends[:, -1:], value, 0)

    run_first = unit_end - run_units
    big = run_units // BIG_UNITS
    small = run_units - BIG_UNITS * big
    big_end = jnp.cumsum(big, axis=1)
    small_end = jnp.cumsum(small, axis=1)
    big_first = big_end - big
    small_first = small_end - small
    n_big_max = n_units // BIG_UNITS
    n_small_max = (BIG_UNITS - 1) * n_experts
    copy_lists = jnp.concatenate([
        expand(big_end, run_first - BIG_UNITS * big_first, n_big_max, BIG_UNITS),
        expand(big_end, run_base - BIG_UNITS * big_first, n_big_max, BIG_UNITS),
        expand(small_end, run_first + BIG_UNITS * big - small_first, n_small_max, 1),
        expand(small_end, run_base + BIG_UNITS * big - small_first, n_small_max, 1)], axis=1)
    return {
        "tile": tile, "n_local": n_local, "n_big_max": n_big_max, "n_small_max": n_small_max,
        "n_big": big_end[:, -1].astype(I32), "n_small": small_end[:, -1].astype(I32),
        "copy_lists": copy_lists.astype(I32).reshape(n_tiles, 1, 2 * (n_big_max + n_small_max)),
    }


def _moe(groups, w, norm_w, final_norm):
    cnts = [g[4] for g in groups]
    tiles = [g[0].shape[1] // g[4].shape[1] for g in groups]
    layout, plans = _moe_plan(cnts, tiles)
    xs = None
    for (h2, xn, lp, gate, cnt), plan in zip(groups, plans):
        xs = _dispatch(lp, gate, xn, layout, plan, xs)
    ys = _experts(xs, layout["block_e"], layout["n_used"], w, layout["blk"])
    outs = []
    for (h2, xn, lp, gate, cnt), plan in zip(groups, plans):
        batch, length, d_model = h2.shape
        lp_cols = lp.transpose(0, 2, 1).reshape(batch * length, TOP_K)
        out = _combine(lp_cols, h2.reshape(batch * length, d_model), ys, norm_w, plan, final_norm)
        outs.append(out.reshape(batch, length, d_model))
    return outs


def _pre_moe(x, hist, s0, pos0, n_hist_valid, mk, mv, w):
    h1, new_hist, s_new = _mixer(x, hist, s0, pos0, n_hist_valid, w)
    return _cross_router(h1, mk.astype(BF16), mv.astype(BF16), w), new_hist, s_new


def kernel(x_prompt, x_sample, mem_prompt, state_pool, state_ret, cache_mem_k, cache_mem_v, norm_mix_w, w_in, pool_w_grp, pool_scale, w_pool_out, ret_gn_w, w_ret_out, w_o, norm_mem_w, w_mk, w_mv, norm_cross_w, w_cq, w_co, norm_ffn_w, w_router, b_router, w_gu, b_gu, w_down, b_down, norm_final_w):
    depth = w_in.shape[0]
    batch_p = x_prompt.shape[0]
    m_len = mem_prompt.shape[1]
    d_model = x_prompt.shape[-1]
    hp, hs = x_prompt, x_sample
    norm_w = norm_final_w[None, :]
    mem_k_p, mem_v_p, pool_p, ret_p, pool_s, ret_s = [], [], [], [], [], []
    for l in range(depth):
        w = {
            "norm_mix_w": norm_mix_w[l][None, :], "w_in": w_in[l].astype(BF16),
            "pool_w_grp": pool_w_grp[l].astype(BF16), "pool_scale": pool_scale[l][None, :],
            "w_pool_out": w_pool_out[l].astype(BF16), "ret_gn_w": ret_gn_w[l][None, :],
            "w_ret_out": w_ret_out[l].astype(BF16), "w_o": w_o[l].astype(BF16),
            "norm_mem_w": norm_mem_w[l][None, :], "w_mk": w_mk[l].astype(BF16), "w_mv": w_mv[l].astype(BF16),
            "norm_cross_w": norm_cross_w[l][None, :], "w_cq": w_cq[l].astype(BF16), "w_co": w_co[l].astype(BF16),
            "norm_ffn_w": norm_ffn_w[l][None, :], "w_router_t": w_router[l].T.astype(BF16),
            "b_router_c": b_router[l][:, None],
            "w_gu": w_gu[l], "b_gu": b_gu[l][:, None, :],
            "w_down": w_down[l], "b_down": b_down[l][:, None, :],
        }
        last = l == depth - 1
        mk, mv = _mem_kv(mem_prompt, w)
        zero_hist = jnp.zeros((batch_p, POOL_HIST, POOL_WIDTH), F32)
        zero_state = jnp.zeros((batch_p, RET_HEADS, RET_DK, RET_DV), F32)
        routed_p, hist_p, s_p = _pre_moe(hp, zero_hist, zero_state, 0, 0, mk, mv, w)
        mem_k_p.append(mk.reshape(batch_p, m_len, MEM_HEADS, d_model // MEM_HEADS))
        mem_v_p.append(mv.reshape(batch_p, m_len, MEM_HEADS, d_model // MEM_HEADS))
        pool_p.append(hist_p)
        ret_p.append(s_p)
        ck = cache_mem_k[l].reshape(cache_mem_k.shape[1], m_len, d_model)
        cv = cache_mem_v[l].reshape(cache_mem_v.shape[1], m_len, d_model)
        routed_s, hist_s, s_s = _pre_moe(hs, state_pool[l], state_ret[l], PAST_LEN, POOL_HIST, ck, cv, w)
        pool_s.append(hist_s)
        ret_s.append(s_s)
        hp, hs = _moe([routed_p, routed_s], w, norm_w, last)
    return (hp, hs, jnp.stack(mem_k_p), jnp.stack(mem_v_p), jnp.stack(pool_p), jnp.stack(ret_p),
            jnp.stack(pool_s), jnp.stack(ret_s))
```

```python
import functools

import jax
import jax.numpy as jnp
from jax import lax
from jax.experimental import pallas as pl
from jax.experimental.pallas import tpu as pltpu

F32 = jnp.float32
BF16 = jnp.bfloat16
I32 = jnp.int32
U32 = jnp.uint32
HIGH_HALF = 0xFFFF0000

EPS = 1e-6
PAST_LEN = 1024
POOL_WINDOWS = (2, 4, 8, 16)
POOL_GROUP_DIM = 128
POOL_WIDTH = POOL_GROUP_DIM * len(POOL_WINDOWS)
POOL_HIST = max(POOL_WINDOWS) - 1
RET_HEADS = 4
RET_DK = 128
RET_DV = 256
ROPE_BASE = 10000.0
MEM_HEADS = 4
TOP_K = 4
SWIGLU_LIMIT = 7.0
SWIGLU_ALPHA = 1.702

SUBLANES = 8
LANES = 128
VMEM_LIMIT_BYTES = 56 * 1024 * 1024

MIXER_TILE = 512
RET_CHUNK = 256
CROSS_TILE = 512
CROSS_CHUNK = 512
EXPERT_BLOCK = 512
MIN_EXPERT_BLOCK = 128
ROW_UNIT = SUBLANES
PERM_CHUNK = 256
BIG_UNITS = 4
DMA_UNROLL = 4


def _cparams(n_axes):
    return pltpu.CompilerParams(dimension_semantics=("arbitrary",) * n_axes,
                                vmem_limit_bytes=VMEM_LIMIT_BYTES)


def _const_spec(shape):
    nd = len(shape)
    return pl.BlockSpec(shape, lambda *_: (0,) * nd, pipeline_mode=pl.Buffered(1))


def _round_up(n, m):
    return (n + m - 1) // m * m


def _rms(x32, w_row):
    ms = jnp.mean(x32 * x32, axis=-1, keepdims=True)
    return x32 * lax.rsqrt(ms + EPS) * w_row


def _dot(a, b):
    return jnp.dot(a, b, preferred_element_type=F32)


def _dot_nt(a, b):
    return lax.dot_general(a, b, (((1,), (1,)), ((), ())), preferred_element_type=F32)


def _dot_tn(a, b):
    return lax.dot_general(a, b, (((0,), (0,)), ((), ())), preferred_element_type=F32)


def _rope_kernel(inv_ref, sign_ref, cos_ref, sin_ref, *, pos0, tile):
    i = pl.program_id(0)
    pos = (lax.broadcasted_iota(I32, (tile, RET_DK), 0) + (pos0 + i * tile)).astype(F32)
    ang = pos * inv_ref[...]
    cos_ref[...] = jnp.cos(ang)
    sin_ref[...] = jnp.sin(ang) * sign_ref[...]


def _rope_tables(length, pos0):
    half = RET_DK // 2
    inv = 1.0 / (ROPE_BASE ** (jnp.arange(half, dtype=F32) / half))
    inv2 = jnp.concatenate([inv, inv])[None, :]
    sign = jnp.concatenate([-jnp.ones((half,), F32), jnp.ones((half,), F32)])[None, :]
    tile = min(length, 512)
    assert length % tile == 0
    return pl.pallas_call(
        functools.partial(_rope_kernel, pos0=pos0, tile=tile),
        out_shape=(jax.ShapeDtypeStruct((length, RET_DK), F32),) * 2,
        grid=(length // tile,),
        in_specs=[pl.BlockSpec((1, RET_DK), lambda i: (0, 0))] * 2,
        out_specs=(pl.BlockSpec((tile, RET_DK), lambda i: (i, 0)),) * 2,
        compiler_params=_cparams(1),
        name="rope_tables",
    )(inv2, sign)


def _mixer_kernel(x_ref, hist_ref, s0_ref, cos_ref, sin_ref, nmw_ref, win_ref, pwg_ref, pscale_ref,
                  wpo_ref, gnw_ref, wro_ref, wo_ref, dec_ref, qd_ref, kd_ref, gc_ref,
                  h_ref, hist_out_ref, s_out_ref,
                  ubuf, s_scr, z_scr, r_scr, yp_scr, *, tile, chunk, n_hist_valid, d_model):
    t = pl.program_id(1)
    n_t = pl.num_programs(1)
    q_off = POOL_WIDTH
    k_off = q_off + RET_HEADS * RET_DK
    v_off = k_off + RET_HEADS * RET_DK
    g_off = v_off + RET_HEADS * RET_DV
    ap_off = g_off + RET_HEADS * RET_DV
    ar_off = ap_off + d_model

    @pl.when(t == 0)
    def _():
        ubuf[0:1, :] = jnp.zeros((1, POOL_WIDTH), F32)
        ubuf[1:1 + POOL_HIST, :] = hist_ref[0]
        s_scr[...] = s0_ref[0]

    n_chunks = tile // chunk
    hist_rows = POOL_HIST + 1
    for c in range(n_chunks):
        rows = slice(c * chunk, (c + 1) * chunk)
        xn = _rms(x_ref[0, rows, :], nmw_ref[...]).astype(BF16)
        z_scr[rows, :] = _dot(xn, win_ref[...])
        ubuf[hist_rows + c * chunk:hist_rows + (c + 1) * chunk, :] = z_scr[rows, 0:POOL_WIDTH]

    for c in range(n_chunks):
        rows = slice(c * chunk, (c + 1) * chunk)
        full = ubuf[c * chunk:c * chunk + hist_rows + chunk, :]
        pos = t * tile + c * chunk + lax.broadcasted_iota(I32, (chunk, 1), 0)
        ys = []
        for g, w in enumerate(POOL_WINDOWS):
            f = full[:, g * POOL_GROUP_DIM:(g + 1) * POOL_GROUP_DIM]
            s = f
            shift = 1
            while shift < w:
                s = s + pltpu.roll(s, shift, 0)
                shift *= 2
            cnt = jnp.minimum(w, pos + 1 + n_hist_valid).astype(F32)
            d = s[hist_rows:, :] / cnt - f[hist_rows:, :]
            ys.append(_dot(d.astype(BF16), pwg_ref[g]))
        yp_scr[rows, :] = (jnp.concatenate(ys, axis=-1) * pscale_ref[...]).astype(BF16)
    ubuf[1:1 + POOL_HIST, :] = ubuf[tile + 1:tile + 1 + POOL_HIST, :]

    for c in range(n_chunks):
        rows = slice(c * chunk, (c + 1) * chunk)
        cosc = cos_ref[rows, :]
        sinc = sin_ref[rows, :]
        for h in range(RET_HEADS):
            qh = z_scr[rows, q_off + h * RET_DK:q_off + (h + 1) * RET_DK]
            kh = z_scr[rows, k_off + h * RET_DK:k_off + (h + 1) * RET_DK]
            vb = z_scr[rows, v_off + h * RET_DV:v_off + (h + 1) * RET_DV].astype(BF16)
            qr = qh * cosc + pltpu.roll(qh, RET_DK // 2, 1) * sinc
            kr = (kh * cosc + pltpu.roll(kh, RET_DK // 2, 1) * sinc) * (RET_DK ** -0.5)
            qb = qr.astype(BF16)
            kb = kr.astype(BF16)
            scores = _dot_nt(qb, kb) * dec_ref[h]
            state = s_scr[h]
            o = _dot(scores.astype(BF16), vb) + _dot(qb, state.astype(BF16)) * qd_ref[h]
            s_scr[h] = gc_ref[h] * state + _dot_tn((kr * kd_ref[h]).astype(BF16), vb)
            mu = jnp.mean(o, axis=-1, keepdims=True)
            dlt = o - mu
            var = jnp.mean(dlt * dlt, axis=-1, keepdims=True)
            on = dlt * lax.rsqrt(var + EPS) * gnw_ref[:, h * RET_DV:(h + 1) * RET_DV]
            gate = z_scr[rows, g_off + h * RET_DV:g_off + (h + 1) * RET_DV]
            r_scr[rows, h * RET_DV:(h + 1) * RET_DV] = (on * (gate * jax.nn.sigmoid(gate))).astype(BF16)

    for c in range(n_chunks):
        rows = slice(c * chunk, (c + 1) * chunk)
        branch_pool = _dot(yp_scr[rows, :], wpo_ref[...])
        branch_ret = _dot(r_scr[rows, :], wro_ref[...])
        merged = (jax.nn.sigmoid(z_scr[rows, ap_off:ap_off + d_model]) * branch_pool
                  + jax.nn.sigmoid(z_scr[rows, ar_off:ar_off + d_model]) * branch_ret)
        h_ref[0, rows, :] = x_ref[0, rows, :] + _dot(merged.astype(BF16), wo_ref[...])

    @pl.when(t == n_t - 1)
    def _():
        hist_out_ref[0] = ubuf[1:1 + POOL_HIST, :]
        s_out_ref[0] = s_scr[...]


def _ret_tables(chunk):
    lg = jnp.log(1.0 - 2.0 ** (-5.0 - jnp.arange(RET_HEADS, dtype=F32)))
    idx = jnp.arange(chunk)
    rel = idx[:, None] - idx[None, :]
    decay = jnp.where(rel[None] >= 0, jnp.exp(jnp.maximum(rel, 0)[None].astype(F32) * lg[:, None, None]), 0.0)
    q_decay = jnp.exp((idx + 1).astype(F32)[None, :] * lg[:, None])[:, :, None]
    k_decay = jnp.exp((chunk - 1 - idx).astype(F32)[None, :] * lg[:, None])[:, :, None]
    g_chunk = jnp.exp(chunk * lg)
    return decay, q_decay, k_decay, g_chunk


def _mixer(x, hist, s0, pos0, n_hist_valid, w):
    batch, length, d_model = x.shape
    tile = min(MIXER_TILE, length)
    chunk = min(RET_CHUNK, tile)
    assert length % tile == 0 and tile % chunk == 0 and tile >= POOL_HIST + 1
    in_width = w["w_in"].shape[1]
    cos, sin = _rope_tables(length, pos0)
    decay, q_decay, k_decay, g_chunk = _ret_tables(chunk)
    kern = functools.partial(_mixer_kernel, tile=tile, chunk=chunk, n_hist_valid=n_hist_valid, d_model=d_model)
    return pl.pallas_call(
        kern,
        out_shape=(jax.ShapeDtypeStruct(x.shape, F32),
                   jax.ShapeDtypeStruct(hist.shape, F32),
                   jax.ShapeDtypeStruct(s0.shape, F32)),
        grid=(batch, length // tile),
        in_specs=[
            pl.BlockSpec((1, tile, d_model), lambda b, t: (b, t, 0)),
            pl.BlockSpec((1, POOL_HIST, POOL_WIDTH), lambda b, t: (b, 0, 0)),
            pl.BlockSpec((1, RET_HEADS, RET_DK, RET_DV), lambda b, t: (b, 0, 0, 0)),
            pl.BlockSpec((tile, RET_DK), lambda b, t: (t, 0)),
            pl.BlockSpec((tile, RET_DK), lambda b, t: (t, 0)),
            _const_spec((1, d_model)),
            _const_spec((d_model, in_width)),
            _const_spec(w["pool_w_grp"].shape),
            _const_spec((1, POOL_WIDTH)),
            _const_spec((POOL_WIDTH, d_model)),
            _const_spec((1, RET_HEADS * RET_DV)),
            _const_spec((RET_HEADS * RET_DV, d_model)),
            _const_spec((d_model, d_model)),
            _const_spec(decay.shape),
            _const_spec(q_decay.shape),
            _const_spec(k_decay.shape),
            pl.BlockSpec(memory_space=pltpu.SMEM),
        ],
        out_specs=(
            pl.BlockSpec((1, tile, d_model), lambda b, t: (b, t, 0)),
            pl.BlockSpec((1, POOL_HIST, POOL_WIDTH), lambda b, t: (b, 0, 0)),
            pl.BlockSpec((1, RET_HEADS, RET_DK, RET_DV), lambda b, t: (b, 0, 0, 0)),
        ),
        scratch_shapes=[
            pltpu.VMEM((POOL_HIST + 1 + tile, POOL_WIDTH), F32),
            pltpu.VMEM((RET_HEADS, RET_DK, RET_DV), F32),
            pltpu.VMEM((tile, in_width), F32),
            pltpu.VMEM((tile, RET_HEADS * RET_DV), BF16),
            pltpu.VMEM((tile, POOL_WIDTH), BF16),
        ],
        compiler_params=_cparams(2),
        name="mixer",
    )(x, hist, s0, cos, sin, w["norm_mix_w"], w["w_in"], w["pool_w_grp"], w["pool_scale"], w["w_pool_out"],
      w["ret_gn_w"], w["w_ret_out"], w["w_o"], decay, q_decay, k_decay, g_chunk)


def _mem_kv_kernel(mem_ref, nw_ref, wk_ref, wv_ref, k_ref, v_ref):
    mn = _rms(mem_ref[0], nw_ref[...]).astype(BF16)
    k_ref[0] = _dot(mn, wk_ref[...])
    v_ref[0] = _dot(mn, wv_ref[...])


def _mem_kv(mem, w):
    batch, m_len, d_model = mem.shape
    out = jax.ShapeDtypeStruct((batch, m_len, d_model), F32)
    blk = pl.BlockSpec((1, m_len, d_model), lambda b: (b, 0, 0))
    return pl.pallas_call(
        _mem_kv_kernel,
        out_shape=(out, out),
        grid=(batch,),
        in_specs=[blk, _const_spec((1, d_model)), _const_spec((d_model, d_model)), _const_spec((d_model, d_model))],
        out_specs=(blk, blk),
        compiler_params=_cparams(1),
        name="mem_kv",
    )(mem, w["norm_mem_w"], w["w_mk"], w["w_mv"])


def _cross_router_kernel(h_ref, mk_ref, mv_ref, ncw_ref, wcq_ref, wco_ref, nfw_ref, wrt_ref, brt_ref,
                         h2_ref, xn_ref, lp_ref, gate_ref, cnt_ref,
                         o_scr, h2_prev, *, tile, head_dim, n_experts):
    @pl.when(pl.program_id(0) == 0)
    def _():
        h2_prev[...] = jnp.zeros(h2_prev.shape, F32)

    d_model = h_ref.shape[2]
    half = d_model // 2
    h1 = h_ref[0]
    hn = _rms(h1, ncw_ref[...]).astype(BF16)
    q = _dot(hn, wcq_ref[...]).astype(BF16)
    xn = _rms(h2_prev[...], nfw_ref[...]).astype(BF16)
    xn_ref[0] = xn
    for hd in range(MEM_HEADS):
        cols = slice(hd * head_dim, (hd + 1) * head_dim)
        s = _dot_nt(q[:, cols], mk_ref[0, :, cols]) * (head_dim ** -0.5)
        e = jnp.exp(s - jnp.max(s, axis=-1, keepdims=True))
        p = e / jnp.sum(e, axis=-1, keepdims=True)
        o_scr[:, cols] = _dot(p.astype(BF16), mv_ref[0, :, cols]).astype(BF16)

    work = _dot_nt(wrt_ref[...], xn) + brt_ref[...]
    h2_a = h1[:, 0:half] + _dot(o_scr[...], wco_ref[:, 0:half])

    e_iota = lax.broadcasted_iota(I32, (n_experts, tile), 0)
    vals, sels = [], []
    for _ in range(TOP_K):
        m = jnp.max(work, axis=0, keepdims=True)
        first = jnp.min(jnp.where(work == m, e_iota, n_experts), axis=0, keepdims=True)
        sel = e_iota == first
        vals.append(m)
        sels.append(sel)
        work = jnp.where(sel, -jnp.inf, work)
    exps = [jnp.exp(v - vals[0]) for v in vals]
    denom = exps[0] + exps[1] + exps[2] + exps[3]
    assigned = jnp.zeros((n_experts, tile), F32)
    for sel in sels:
        assigned = assigned + sel.astype(F32)
    r_iota = lax.broadcasted_iota(I32, (tile, tile), 0)
    c_iota = lax.broadcasted_iota(I32, (tile, tile), 1)
    prior = _dot(assigned.astype(BF16), (r_iota < c_iota).astype(BF16))
    counts = jnp.sum(assigned, axis=1, keepdims=True)
    run = jnp.floor((counts + (ROW_UNIT - 1)) * (1.0 / ROW_UNIT)) * ROW_UNIT
    er = lax.broadcasted_iota(I32, (n_experts, n_experts), 0)
    ec = lax.broadcasted_iota(I32, (n_experts, n_experts), 1)
    run_start = _dot((ec < er).astype(BF16), jnp.broadcast_to(run, (n_experts, LANES)).astype(BF16))[:, 0:1]

    h2_b = h1[:, half:d_model] + _dot(o_scr[...], wco_ref[:, half:d_model])
    h2 = jnp.concatenate([h2_a, h2_b], axis=-1)
    h2_ref[0] = h2
    h2_prev[...] = h2

    for k in range(TOP_K):
        lp_ref[0, k:k + 1, :] = jnp.sum(jnp.where(sels[k], run_start + prior, 0.0), axis=0, keepdims=True).astype(I32)
        gate_ref[0, k:k + 1, :] = exps[k] / denom
    cnt_ref[0, 0] = jnp.broadcast_to(counts, (n_experts, LANES)).astype(I32)


def _cross_router(h1, mk, mv, w):
    batch, length, d_model = h1.shape
    m_len = mk.shape[1]
    head_dim = d_model // MEM_HEADS
    n_experts = w["w_router_t"].shape[0]
    tile = min(CROSS_TILE, length)
    assert length % tile == 0
    n_t = length // tile
    n_tiles = batch * n_t
    kern = functools.partial(_cross_router_kernel, tile=tile, head_dim=head_dim, n_experts=n_experts)

    def attended(i):
        return jnp.minimum(i, n_tiles - 1)

    def routed(i):
        return jnp.maximum(i - 1, 0)

    tok = pl.BlockSpec((1, tile, d_model), lambda i: (attended(i) // n_t, attended(i) % n_t, 0))
    tok_routed = pl.BlockSpec((1, tile, d_model), lambda i: (routed(i) // n_t, routed(i) % n_t, 0))
    mem = pl.BlockSpec((1, m_len, d_model), lambda i: (attended(i) // n_t, 0, 0))
    small = pl.BlockSpec((1, TOP_K, tile), lambda i: (routed(i) // n_t, 0, routed(i) % n_t))
    return pl.pallas_call(
        kern,
        out_shape=(jax.ShapeDtypeStruct(h1.shape, F32),
                   jax.ShapeDtypeStruct(h1.shape, BF16),
                   jax.ShapeDtypeStruct((batch, TOP_K, length), I32),
                   jax.ShapeDtypeStruct((batch, TOP_K, length), F32),
                   jax.ShapeDtypeStruct((batch, n_t, n_experts, LANES), I32)),
        grid=(n_tiles + 1,),
        in_specs=[tok, mem, mem,
                  _const_spec((1, d_model)), _const_spec((d_model, d_model)), _const_spec((d_model, d_model)),
                  _const_spec((1, d_model)), _const_spec((n_experts, d_model)), _const_spec((n_experts, 1))],
        out_specs=(tok, tok_routed, small, small,
                   pl.BlockSpec((1, 1, n_experts, LANES), lambda i: (routed(i) // n_t, routed(i) % n_t, 0, 0))),
        scratch_shapes=[pltpu.VMEM((tile, d_model), BF16), pltpu.VMEM((tile, d_model), F32)],
        compiler_params=_cparams(1),
        name="cross_router",
    )(h1, mk, mv, w["norm_cross_w"], w["w_cq"], w["w_co"], w["norm_ffn_w"], w["w_router_t"], w["b_router_c"])


def _unit_copy(src_ref, dst_ref, src_unit, dst_unit, sem, n_units=1):
    def rows_of(unit):
        start = unit * ROW_UNIT
        return pl.ds(start if isinstance(start, int) else pl.multiple_of(start, ROW_UNIT), n_units * ROW_UNIT)

    return pltpu.make_async_copy(src_ref.at[rows_of(src_unit)], dst_ref.at[rows_of(dst_unit)], sem)


def _run_copies(lists_ref, local_ref, global_ref, sem, n_big_max, n_small_max, to_global):
    def make(local_at, global_at, n_units):
        def copy_of(j):
            loc, glo = lists_ref[0, 0, local_at + j], lists_ref[0, 0, global_at + j]
            if to_global:
                return _unit_copy(local_ref, global_ref, loc, glo, sem, n_units)
            return _unit_copy(global_ref, local_ref, glo, loc, sem, n_units)
        return copy_of

    return (make(0, n_big_max, BIG_UNITS),
            make(2 * n_big_max, 2 * n_big_max + n_small_max, 1))


def _for_units(n, body):
    def group(g, c):
        for lane in range(DMA_UNROLL):
            body(g * DMA_UNROLL + lane, lane)
        return c

    n_groups = lax.div(n, DMA_UNROLL)
    lax.fori_loop(0, n_groups, group, 0)

    def rest(j, c):
        body(j, 0)
        return c

    lax.fori_loop(n_groups * DMA_UNROLL, n, rest, 0)


def _start_units(n, copy_of):
    _for_units(n, lambda j, lane: copy_of(j).start(priority=lane % 2))


def _wait_units(n, copy_of):
    _for_units(n, lambda j, lane: copy_of(j).wait())


def _pack_pairs(a, b):
    ua = lax.bitcast_convert_type(a, U32)
    ub = lax.bitcast_convert_type(b, U32)
    return lax.shift_right_logical(ua, U32(16)) | (ub & U32(HIGH_HALF))


def _unpack_pairs(u):
    a = lax.bitcast_convert_type(lax.shift_left(u, U32(16)), F32)
    b = lax.bitcast_convert_type(u & U32(HIGH_HALF), F32)
    return a, b


def _dispatch_kernel(nbig_ref, nsmall_ref, tail_ref, ntail_ref, lp_ref, gate_ref, xn_ref, lists_ref, listsp_ref,
                     *refs, tile, n_local, d_model, n_big_max, n_small_max, first_group):
    xs_ref, rows, zero_buf, sems, tail_sem = refs if first_group else refs[1:]
    i = pl.program_id(0)
    n_i = pl.num_programs(0)
    slot = lax.rem(i, 2)

    def for_tile(step, lists, at_slot, action):
        big, small = _run_copies(lists, rows.at[at_slot], xs_ref, sems.at[at_slot], n_big_max, n_small_max, True)
        action(nbig_ref[step], big)
        action(nsmall_ref[step], small)

    if first_group:
        @pl.when(i == 0)
        def _():
            zero_buf[...] = jnp.zeros(zero_buf.shape, U32)

            def tail_copy(j):
                return _unit_copy(zero_buf, xs_ref, 0, tail_ref[j], tail_sem)

            _start_units(ntail_ref[0], tail_copy)
            _wait_units(ntail_ref[0], tail_copy)

    half = d_model // 2
    xn = xn_ref[0]
    lps = [lp_ref[0, k:k + 1, :] for k in range(TOP_K)]
    gates = [gate_ref[0, k:k + 1, :] for k in range(TOP_K)]
    for c in range(n_local // PERM_CHUNK):
        r_iota = lax.broadcasted_iota(I32, (PERM_CHUNK, tile), 0) + c * PERM_CHUNK
        gmat = jnp.full((PERM_CHUNK, tile), -1.0, F32)
        for k in range(TOP_K):
            gmat = jnp.where(r_iota == lps[k], gates[k], gmat)
        perm = jnp.where(gmat >= 0.0, 1.0, 0.0).astype(BF16)
        wmat = jnp.maximum(gmat, 0.0)
        rs = pl.ds(c * PERM_CHUNK, PERM_CHUNK)
        xr = _dot(perm, xn)
        rows[slot, rs, 0:half] = _pack_pairs(xr[:, 0:half], xr[:, half:d_model])
        rows[slot, rs, half:half + LANES] = lax.bitcast_convert_type(
            jnp.broadcast_to(jnp.sum(wmat, axis=1, keepdims=True), (PERM_CHUNK, LANES)), U32)

    for_tile(i, lists_ref, slot, _start_units)

    @pl.when(i > 0)
    def _():
        for_tile(i - 1, listsp_ref, 1 - slot, _wait_units)

    @pl.when(i == n_i - 1)
    def _():
        for_tile(i, lists_ref, slot, _wait_units)


def _dispatch(lp, gate, xn, layout, plan, xs_prev):
    batch, length, d_model = xn.shape
    tile, n_local = plan["tile"], plan["n_local"]
    n_big_max, n_small_max = plan["n_big_max"], plan["n_small_max"]
    n_list = 2 * (n_big_max + n_small_max)
    n_t = length // tile
    width = d_model // 2 + LANES
    first_group = xs_prev is None
    small = pl.BlockSpec((1, TOP_K, tile), lambda i, *_: (i // n_t, 0, i % n_t))
    n_prefetch = 4
    in_specs = [small, small,
                pl.BlockSpec((1, tile, d_model), lambda i, *_: (i // n_t, i % n_t, 0)),
                pl.BlockSpec((1, 1, n_list), lambda i, *_: (i, 0, 0), memory_space=pltpu.SMEM),
                pl.BlockSpec((1, 1, n_list), lambda i, *_: (jnp.maximum(i - 1, 0), 0, 0),
                             memory_space=pltpu.SMEM)]
    args = [plan["n_big"], plan["n_small"], layout["tail_units"], layout["n_tail"], lp, gate, xn,
            plan["copy_lists"], plan["copy_lists"]]
    aliases = {}
    if not first_group:
        aliases = {len(args): 0}
        in_specs.append(pl.BlockSpec(memory_space=pl.ANY))
        args.append(xs_prev)
    grid_spec = pltpu.PrefetchScalarGridSpec(
        num_scalar_prefetch=n_prefetch,
        grid=(batch * n_t,),
        in_specs=in_specs,
        out_specs=pl.BlockSpec(memory_space=pl.ANY),
        scratch_shapes=[pltpu.VMEM((2, n_local, width), U32), pltpu.VMEM((ROW_UNIT, width), U32),
                        pltpu.SemaphoreType.DMA((2,)), pltpu.SemaphoreType.DMA(())],
    )
    return pl.pallas_call(
        functools.partial(_dispatch_kernel, tile=tile, n_local=n_local, d_model=d_model,
                          n_big_max=n_big_max, n_small_max=n_small_max, first_group=first_group),
        out_shape=jax.ShapeDtypeStruct((layout["n_rows"], width), U32),
        grid_spec=grid_spec,
        input_output_aliases=aliases,
        compiler_params=_cparams(1),
        name="dispatch",
    )(*args)


def _expert_kernel(be_ref, nb_ref, xs_ref, wgu_ref, bgu_ref, wd_ref, bd_ref, ys_ref, wgu_bf, wd_bf,
                   *, d_model, d_expert):
    b = pl.program_id(0)

    @pl.when(b < nb_ref[0])
    def _():
        @pl.when((b == 0) | (be_ref[b] != be_ref[jnp.maximum(b - 1, 0)]))
        def _():
            wgu_bf[...] = wgu_ref[0].astype(BF16)
            wd_bf[...] = wd_ref[0].astype(BF16)

        half = d_model // 2
        xa, xb = _unpack_pairs(xs_ref[:, 0:half])
        row_gate = lax.bitcast_convert_type(xs_ref[:, half:half + 1], F32)
        gu = _dot(jnp.concatenate([xa, xb], axis=-1).astype(BF16), wgu_bf[...]) + bgu_ref[0]
        gl = jnp.minimum(gu[:, :d_expert], SWIGLU_LIMIT)
        up = jnp.clip(gu[:, d_expert:], -SWIGLU_LIMIT, SWIGLU_LIMIT)
        act = (gl * jax.nn.sigmoid(SWIGLU_ALPHA * gl) * (up + 1.0)).astype(BF16)
        y = ((_dot(act, wd_bf[...]) + bd_ref[0]) * row_gate).astype(BF16).astype(F32)
        ys_ref[...] = _pack_pairs(y[:, 0:half], y[:, half:d_model])


def _experts(xs, block_e, n_used, w, blk):
    n_rows, width = xs.shape
    d_model = 2 * (width - LANES)
    d_expert = w["w_down"].shape[1]

    def row_map(b, be, nb):
        return (jnp.minimum(b, nb[0] - 1), 0)

    def exp_map(b, be, nb):
        return (be[b], 0, 0)

    grid_spec = pltpu.PrefetchScalarGridSpec(
        num_scalar_prefetch=2,
        grid=(n_rows // blk,),
        in_specs=[pl.BlockSpec((blk, width), row_map),
                  pl.BlockSpec((1, d_model, 2 * d_expert), exp_map),
                  pl.BlockSpec((1, 1, 2 * d_expert), exp_map),
                  pl.BlockSpec((1, d_expert, d_model), exp_map),
                  pl.BlockSpec((1, 1, d_model), exp_map)],
        out_specs=pl.BlockSpec((blk, d_model // 2), row_map),
        scratch_shapes=[pltpu.VMEM((d_model, 2 * d_expert), BF16), pltpu.VMEM((d_expert, d_model), BF16)],
    )
    return pl.pallas_call(
        functools.partial(_expert_kernel, d_model=d_model, d_expert=d_expert),
        out_shape=jax.ShapeDtypeStruct((n_rows, d_model // 2), U32),
        grid_spec=grid_spec,
        compiler_params=_cparams(1),
        name="experts",
    )(block_e, n_used, xs, w["w_gu"], w["b_gu"], w["w_down"], w["b_down"])


def _combine_kernel(nbig_ref, nsmall_ref, lpc_ref, h2_ref, nw_ref, lists_ref, listsn_ref, ys_ref, out_ref,
                    rows, sems, *, tile, n_local, final_norm, n_big_max, n_small_max):
    i = pl.program_id(0)
    n_i = pl.num_programs(0)
    slot = lax.rem(i, 2)

    def for_tile(step, lists, at_slot, action):
        big, small = _run_copies(lists, rows.at[at_slot], ys_ref, sems.at[at_slot], n_big_max, n_small_max, False)
        action(nbig_ref[step], big)
        action(nsmall_ref[step], small)

    @pl.when(i == 0)
    def _():
        rows[...] = jnp.zeros(rows.shape, U32)
        for_tile(0, lists_ref, 0, _start_units)

    @pl.when(i + 1 < n_i)
    def _():
        for_tile(i + 1, listsn_ref, 1 - slot, _start_units)

    for_tile(i, lists_ref, slot, _wait_units)

    lps = [lpc_ref[:, k:k + 1] for k in range(TOP_K)]
    half = out_ref.shape[1] // 2
    ya = jnp.zeros((tile, half), F32)
    yb = jnp.zeros((tile, half), F32)
    for c in range(n_local // PERM_CHUNK):
        c_iota = lax.broadcasted_iota(I32, (tile, PERM_CHUNK), 1) + c * PERM_CHUNK
        sel = jnp.zeros((tile, PERM_CHUNK), F32)
        for k in range(TOP_K):
            sel = jnp.where(c_iota == lps[k], 1.0, sel)
        sel = sel.astype(BF16)
        ra, rb = _unpack_pairs(rows[slot, pl.ds(c * PERM_CHUNK, PERM_CHUNK), :])
        ya = ya + _dot(sel, ra.astype(BF16))
        yb = yb + _dot(sel, rb.astype(BF16))
    h3 = h2_ref[...] + jnp.concatenate([ya, yb], axis=-1)
    out_ref[...] = _rms(h3, nw_ref[...]) if final_norm else h3


def _combine(lp_cols, h2, ys, norm_w, plan, final_norm):
    n_tok, d_model = h2.shape
    tile, n_local = plan["tile"], plan["n_local"]
    n_big_max, n_small_max = plan["n_big_max"], plan["n_small_max"]
    n_list = 2 * (n_big_max + n_small_max)
    n_tiles = n_tok // tile
    grid_spec = pltpu.PrefetchScalarGridSpec(
        num_scalar_prefetch=2,
        grid=(n_tiles,),
        in_specs=[pl.BlockSpec((tile, TOP_K), lambda i, *_: (i, 0)),
                  pl.BlockSpec((tile, d_model), lambda i, *_: (i, 0)),
                  _const_spec((1, d_model)),
                  pl.BlockSpec((1, 1, n_list), lambda i, *_: (i, 0, 0), memory_space=pltpu.SMEM),
                  pl.BlockSpec((1, 1, n_list), lambda i, *_: (jnp.minimum(i + 1, n_tiles - 1), 0, 0),
                               memory_space=pltpu.SMEM),
                  pl.BlockSpec(memory_space=pl.ANY)],
        out_specs=pl.BlockSpec((tile, d_model), lambda i, *_: (i, 0)),
        scratch_shapes=[pltpu.VMEM((2, n_local, d_model // 2), U32), pltpu.SemaphoreType.DMA((2,))],
    )
    return pl.pallas_call(
        functools.partial(_combine_kernel, tile=tile, n_local=n_local, final_norm=final_norm,
                          n_big_max=n_big_max, n_small_max=n_small_max),
        out_shape=jax.ShapeDtypeStruct((n_tok, d_model), F32),
        grid_spec=grid_spec,
        compiler_params=_cparams(1),
        name="combine",
    )(plan["n_big"], plan["n_small"], lp_cols, h2, norm_w, plan["copy_lists"], plan["copy_lists"], ys)


def _moe_plan(cnts, tiles):
    n_experts = cnts[0].shape[2]
    group_counts = [c[..., 0].reshape(-1, n_experts) for c in cnts]
    counts = jnp.concatenate(group_counts, axis=0)
    n_tiles = counts.shape[0]
    n_assigned = sum(g.shape[0] * t for g, t in zip(group_counts, tiles)) * TOP_K
    blk = EXPERT_BLOCK
    while blk > MIN_EXPERT_BLOCK and n_assigned < n_experts * blk:
        blk //= 2
    run_units = (counts + ROW_UNIT - 1) // ROW_UNIT
    expert_units = jnp.sum(run_units, axis=0)
    blk_units = blk // ROW_UNIT
    padded_units = (expert_units + blk_units - 1) // blk_units * blk_units
    expert_end = jnp.cumsum(padded_units)
    expert_start = expert_end - padded_units
    run_base = expert_start[None, :] + jnp.cumsum(run_units, axis=0) - run_units
    tj = jnp.arange(n_experts * (blk_units - 1))
    tail_len = padded_units - expert_units
    tail_end = jnp.cumsum(tail_len)
    tshift = expert_start + expert_units - (tail_end - tail_len)
    tpast = tj[:, None] >= tail_end[None, :-1]
    tail_units = tj + tshift[0] + jnp.sum(jnp.where(tpast, (tshift[1:] - tshift[:-1])[None, :], 0), axis=-1)
    n_tail = tail_end[-1]
    tail_units = jnp.where(tj < n_tail, tail_units, 0)
    n_blocks = (n_assigned + n_tiles * n_experts * (ROW_UNIT - 1) + n_experts * (blk - 1)) // blk
    block_e = jnp.minimum(jnp.sum((jnp.arange(n_blocks) * blk_units)[:, None] >= expert_end[None, :], axis=-1),
                          n_experts - 1)
    layout = {
        "n_rows": n_blocks * blk, "blk": blk,
        "tail_units": tail_units.astype(I32), "n_tail": n_tail.astype(I32).reshape(1),
        "block_e": block_e.astype(I32), "n_used": (expert_end[-1] // blk_units).astype(I32).reshape(1),
    }
    plans, first = [], 0
    for g, tile in zip(group_counts, tiles):
        last = first + g.shape[0]
        plans.append(_group_plan(run_units[first:last], run_base[first:last], tile))
        first = last
    return layout, plans


def _group_plan(run_units, run_base, tile):
    n_tiles, n_experts = run_units.shape
    unit_end = jnp.cumsum(run_units, axis=1)
    n_local = _round_up(TOP_K * tile + (ROW_UNIT - 1) * n_experts, PERM_CHUNK)
    n_units = n_local // ROW_UNIT

    def expand(ends, shift, n, stride):
        i = jnp.arange(n)
        step = shift[:, 1:] - shift[:, :-1]
        past = i[None, :, None] >= ends[:, None, :-1]
        value = stride * i[None, :] + shift[:, 0:1] + jnp.sum(jnp.where(past, step[:, None, :], 0), axis=-1)
        return jnp.where(i[None, :] < ends[:, -1:], value, 0)

    run_first = unit_end - run_units
    big = run_units // BIG_UNITS
    small = run_units - BIG_UNITS * big
    big_end = jnp.cumsum(big, axis=1)
    small_end = jnp.cumsum(small, axis=1)
    big_first = big_end - big
    small_first = small_end - small
    n_big_max = n_units // BIG_UNITS
    n_small_max = (BIG_UNITS - 1) * n_experts
    copy_lists = jnp.concatenate([
        expand(big_end, run_first - BIG_UNITS * big_first, n_big_max, BIG_UNITS),
        expand(big_end, run_base - BIG_UNITS * big_first, n_big_max, BIG_UNITS),
        expand(small_end, run_first + BIG_UNITS * big - small_first, n_small_max, 1),
        expand(small_end, run_base + BIG_UNITS * big - small_first, n_small_max, 1)], axis=1)
    return {
        "tile": tile, "n_local": n_local, "n_big_max": n_big_max, "n_small_max": n_small_max,
        "n_big": big_end[:, -1].astype(I32), "n_small": small_end[:, -1].astype(I32),
        "copy_lists": copy_lists.astype(I32).reshape(n_tiles, 1, 2 * (n_big_max + n_small_max)),
    }


def _moe(groups, w, norm_w, final_norm):
    cnts = [g[4] for g in groups]
    tiles = [g[0].shape[1] // g[4].shape[1] for g in groups]
    layout, plans = _moe_plan(cnts, tiles)
    xs = None
    for (h2, xn, lp, gate, cnt), plan in zip(groups, plans):
        xs = _dispatch(lp, gate, xn, layout, plan, xs)
    ys = _experts(xs, layout["block_e"], layout["n_used"], w, layout["blk"])
    outs = []
    for (h2, xn, lp, gate, cnt), plan in zip(groups, plans):
        batch, length, d_model = h2.shape
        lp_cols = lp.transpose(0, 2, 1).reshape(batch * length, TOP_K)
        out = _combine(lp_cols, h2.reshape(batch * length, d_model), ys, norm_w, plan, final_norm)
        outs.append(out.reshape(batch, length, d_model))
    return outs


def _pre_moe(x, hist, s0, pos0, n_hist_valid, mk, mv, w):
    h1, new_hist, s_new = _mixer(x, hist, s0, pos0, n_hist_valid, w)
    return _cross_router(h1, mk.astype(BF16), mv.astype(BF16), w), new_hist, s_new


def kernel(x_prompt, x_sample, mem_prompt, state_pool, state_ret, cache_mem_k, cache_mem_v, norm_mix_w, w_in, pool_w_grp, pool_scale, w_pool_out, ret_gn_w, w_ret_out, w_o, norm_mem_w, w_mk, w_mv, norm_cross_w, w_cq, w_co, norm_ffn_w, w_router, b_router, w_gu, b_gu, w_down, b_down, norm_final_w):
    depth = w_in.shape[0]
    batch_p = x_prompt.shape[0]
    m_len = mem_prompt.shape[1]
    d_model = x_prompt.shape[-1]
    hp, hs = x_prompt, x_sample
    norm_w = norm_final_w[None, :]
    mem_k_p, mem_v_p, pool_p, ret_p, pool_s, ret_s = [], [], [], [], [], []
    for l in range(depth):
        w = {
            "norm_mix_w": norm_mix_w[l][None, :], "w_in": w_in[l].astype(BF16),
            "pool_w_grp": pool_w_grp[l].astype(BF16), "pool_scale": pool_scale[l][None, :],
            "w_pool_out": w_pool_out[l].astype(BF16), "ret_gn_w": ret_gn_w[l][None, :],
            "w_ret_out": w_ret_out[l].astype(BF16), "w_o": w_o[l].astype(BF16),
            "norm_mem_w": norm_mem_w[l][None, :], "w_mk": w_mk[l].astype(BF16), "w_mv": w_mv[l].astype(BF16),
            "norm_cross_w": norm_cross_w[l][None, :], "w_cq": w_cq[l].astype(BF16), "w_co": w_co[l].astype(BF16),
            "norm_ffn_w": norm_ffn_w[l][None, :], "w_router_t": w_router[l].T.astype(BF16),
            "b_router_c": b_router[l][:, None],
            "w_gu": w_gu[l], "b_gu": b_gu[l][:, None, :],
            "w_down": w_down[l], "b_down": b_down[l][:, None, :],
        }
        last = l == depth - 1
        mk, mv = _mem_kv(mem_prompt, w)
        zero_hist = jnp.zeros((batch_p, POOL_HIST, POOL_WIDTH), F32)
        zero_state = jnp.zeros((batch_p, RET_HEADS, RET_DK, RET_DV), F32)
        routed_p, hist_p, s_p = _pre_moe(hp, zero_hist, zero_state, 0, 0, mk, mv, w)
        mem_k_p.append(mk.reshape(batch_p, m_len, MEM_HEADS, d_model // MEM_HEADS))
        mem_v_p.append(mv.reshape(batch_p, m_len, MEM_HEADS, d_model // MEM_HEADS))
        pool_p.append(hist_p)
        ret_p.append(s_p)
        ck = cache_mem_k[l].reshape(cache_mem_k.shape[1], m_len, d_model)
        cv = cache_mem_v[l].reshape(cache_mem_v.shape[1], m_len, d_model)
        routed_s, hist_s, s_s = _pre_moe(hs, state_pool[l], state_ret[l], PAST_LEN, POOL_HIST, ck, cv, w)
        pool_s.append(hist_s)
        ret_s.append(s_s)
        hp, hs = _moe([routed_p, routed_s], w, norm_w, last)
    return (hp, hs, jnp.stack(mem_k_p), jnp.stack(mem_v_p), jnp.stack(pool_p), jnp.stack(ret_p),
            jnp.stack(pool_s), jnp.stack(ret_s))
```

```python
import functools

import jax
import jax.numpy as jnp
from jax import lax
from jax.experimental import pallas as pl
from jax.experimental.pallas import tpu as pltpu

F32 = jnp.float32
BF16 = jnp.bfloat16
I32 = jnp.int32
U32 = jnp.uint32
HIGH_HALF = 0xFFFF0000

EPS = 1e-6
PAST_LEN = 1024
POOL_WINDOWS = (2, 4, 8, 16)
POOL_GROUP_DIM = 128
POOL_WIDTH = POOL_GROUP_DIM * len(POOL_WINDOWS)
POOL_HIST = max(POOL_WINDOWS) - 1
RET_HEADS = 4
RET_DK = 128
RET_DV = 256
ROPE_BASE = 10000.0
MEM_HEADS = 4
TOP_K = 4
SWIGLU_LIMIT = 7.0
SWIGLU_ALPHA = 1.702

SUBLANES = 8
LANES = 128
VMEM_LIMIT_BYTES = 56 * 1024 * 1024

MIXER_TILE = 512
RET_CHUNK = 256
CROSS_TILE = 512
CROSS_CHUNK = 512
EXPERT_BLOCK = 1024
MIN_EXPERT_BLOCK = 128
ROW_UNIT = SUBLANES
PERM_CHUNK = 256
BIG_UNITS = 4
DMA_UNROLL = 4


def _cparams(n_axes):
    return pltpu.CompilerParams(dimension_semantics=("arbitrary",) * n_axes,
                                vmem_limit_bytes=VMEM_LIMIT_BYTES)


def _const_spec(shape):
    nd = len(shape)
    return pl.BlockSpec(shape, lambda *_: (0,) * nd, pipeline_mode=pl.Buffered(1))


def _round_up(n, m):
    return (n + m - 1) // m * m


def _rms(x32, w_row):
    ms = jnp.mean(x32 * x32, axis=-1, keepdims=True)
    return x32 * lax.rsqrt(ms + EPS) * w_row


def _dot(a, b):
    return jnp.dot(a, b, preferred_element_type=F32)


def _dot_nt(a, b):
    return lax.dot_general(a, b, (((1,), (1,)), ((), ())), preferred_element_type=F32)


def _dot_tn(a, b):
    return lax.dot_general(a, b, (((0,), (0,)), ((), ())), preferred_element_type=F32)


def _rope_kernel(inv_ref, sign_ref, cos_ref, sin_ref, *, pos0, tile):
    i = pl.program_id(0)
    pos = (lax.broadcasted_iota(I32, (tile, RET_DK), 0) + (pos0 + i * tile)).astype(F32)
    ang = pos * inv_ref[...]
    cos_ref[...] = jnp.cos(ang)
    sin_ref[...] = jnp.sin(ang) * sign_ref[...]


def _rope_tables(length, pos0):
    half = RET_DK // 2
    inv = 1.0 / (ROPE_BASE ** (jnp.arange(half, dtype=F32) / half))
    inv2 = jnp.concatenate([inv, inv])[None, :]
    sign = jnp.concatenate([-jnp.ones((half,), F32), jnp.ones((half,), F32)])[None, :]
    tile = min(length, 512)
    assert length % tile == 0
    return pl.pallas_call(
        functools.partial(_rope_kernel, pos0=pos0, tile=tile),
        out_shape=(jax.ShapeDtypeStruct((length, RET_DK), F32),) * 2,
        grid=(length // tile,),
        in_specs=[pl.BlockSpec((1, RET_DK), lambda i: (0, 0))] * 2,
        out_specs=(pl.BlockSpec((tile, RET_DK), lambda i: (i, 0)),) * 2,
        compiler_params=_cparams(1),
        name="rope_tables",
    )(inv2, sign)


def _mixer_kernel(x_ref, hist_ref, s0_ref, cos_ref, sin_ref, nmw_ref, win_ref, pwg_ref, pscale_ref,
                  wpo_ref, gnw_ref, wro_ref, wo_ref, dec_ref, qd_ref, kd_ref, gc_ref,
                  h_ref, hist_out_ref, s_out_ref,
                  ubuf, s_scr, z_scr, r_scr, yp_scr, *, tile, chunk, n_hist_valid, d_model):
    t = pl.program_id(1)
    n_t = pl.num_programs(1)
    q_off = POOL_WIDTH
    k_off = q_off + RET_HEADS * RET_DK
    v_off = k_off + RET_HEADS * RET_DK
    g_off = v_off + RET_HEADS * RET_DV
    ap_off = g_off + RET_HEADS * RET_DV
    ar_off = ap_off + d_model

    @pl.when(t == 0)
    def _():
        ubuf[0:1, :] = jnp.zeros((1, POOL_WIDTH), F32)
        ubuf[1:1 + POOL_HIST, :] = hist_ref[0]
        s_scr[...] = s0_ref[0]

    n_chunks = tile // chunk
    hist_rows = POOL_HIST + 1
    for c in range(n_chunks):
        rows = slice(c * chunk, (c + 1) * chunk)
        xn = _rms(x_ref[0, rows, :], nmw_ref[...]).astype(BF16)
        z_scr[rows, :] = _dot(xn, win_ref[...])
        ubuf[hist_rows + c * chunk:hist_rows + (c + 1) * chunk, :] = z_scr[rows, 0:POOL_WIDTH]

    for c in range(n_chunks):
        rows = slice(c * chunk, (c + 1) * chunk)
        full = ubuf[c * chunk:c * chunk + hist_rows + chunk, :]
        pos = t * tile + c * chunk + lax.broadcasted_iota(I32, (chunk, 1), 0)
        ys = []
        for g, w in enumerate(POOL_WINDOWS):
            f = full[:, g * POOL_GROUP_DIM:(g + 1) * POOL_GROUP_DIM]
            s = f
            shift = 1
            while shift < w:
                s = s + pltpu.roll(s, shift, 0)
                shift *= 2
            cnt = jnp.minimum(w, pos + 1 + n_hist_valid).astype(F32)
            d = s[hist_rows:, :] / cnt - f[hist_rows:, :]
            ys.append(_dot(d.astype(BF16), pwg_ref[g]))
        yp_scr[rows, :] = (jnp.concatenate(ys, axis=-1) * pscale_ref[...]).astype(BF16)
    ubuf[1:1 + POOL_HIST, :] = ubuf[tile + 1:tile + 1 + POOL_HIST, :]

    for c in range(n_chunks):
        rows = slice(c * chunk, (c + 1) * chunk)
        cosc = cos_ref[rows, :]
        sinc = sin_ref[rows, :]
        for h in range(RET_HEADS):
            qh = z_scr[rows, q_off + h * RET_DK:q_off + (h + 1) * RET_DK]
            kh = z_scr[rows, k_off + h * RET_DK:k_off + (h + 1) * RET_DK]
            vb = z_scr[rows, v_off + h * RET_DV:v_off + (h + 1) * RET_DV].astype(BF16)
            qr = qh * cosc + pltpu.roll(qh, RET_DK // 2, 1) * sinc
            kr = (kh * cosc + pltpu.roll(kh, RET_DK // 2, 1) * sinc) * (RET_DK ** -0.5)
            qb = qr.astype(BF16)
            kb = kr.astype(BF16)
            scores = _dot_nt(qb, kb) * dec_ref[h]
            state = s_scr[h]
            o = _dot(scores.astype(BF16), vb) + _dot(qb, state.astype(BF16)) * qd_ref[h]
            s_scr[h] = gc_ref[h] * state + _dot_tn((kr * kd_ref[h]).astype(BF16), vb)
            mu = jnp.mean(o, axis=-1, keepdims=True)
            dlt = o - mu
            var = jnp.mean(dlt * dlt, axis=-1, keepdims=True)
            on = dlt * lax.rsqrt(var + EPS) * gnw_ref[:, h * RET_DV:(h + 1) * RET_DV]
            gate = z_scr[rows, g_off + h * RET_DV:g_off + (h + 1) * RET_DV]
            r_scr[rows, h * RET_DV:(h + 1) * RET_DV] = (on * (gate * jax.nn.sigmoid(gate))).astype(BF16)

    for c in range(n_chunks):
        rows = slice(c * chunk, (c + 1) * chunk)
        branch_pool = _dot(yp_scr[rows, :], wpo_ref[...])
        branch_ret = _dot(r_scr[rows, :], wro_ref[...])
        merged = (jax.nn.sigmoid(z_scr[rows, ap_off:ap_off + d_model]) * branch_pool
                  + jax.nn.sigmoid(z_scr[rows, ar_off:ar_off + d_model]) * branch_ret)
        h_ref[0, rows, :] = x_ref[0, rows, :] + _dot(merged.astype(BF16), wo_ref[...])

    @pl.when(t == n_t - 1)
    def _():
        hist_out_ref[0] = ubuf[1:1 + POOL_HIST, :]
        s_out_ref[0] = s_scr[...]


def _ret_tables(chunk):
    lg = jnp.log(1.0 - 2.0 ** (-5.0 - jnp.arange(RET_HEADS, dtype=F32)))
    idx = jnp.arange(chunk)
    rel = idx[:, None] - idx[None, :]
    decay = jnp.where(rel[None] >= 0, jnp.exp(jnp.maximum(rel, 0)[None].astype(F32) * lg[:, None, None]), 0.0)
    q_decay = jnp.exp((idx + 1).astype(F32)[None, :] * lg[:, None])[:, :, None]
    k_decay = jnp.exp((chunk - 1 - idx).astype(F32)[None, :] * lg[:, None])[:, :, None]
    g_chunk = jnp.exp(chunk * lg)
    return decay, q_decay, k_decay, g_chunk


def _mixer(x, hist, s0, pos0, n_hist_valid, w):
    batch, length, d_model = x.shape
    tile = min(MIXER_TILE, length)
    chunk = min(RET_CHUNK, tile)
    assert length % tile == 0 and tile % chunk == 0 and tile >= POOL_HIST + 1
    in_width = w["w_in"].shape[1]
    cos, sin = _rope_tables(length, pos0)
    decay, q_decay, k_decay, g_chunk = _ret_tables(chunk)
    kern = functools.partial(_mixer_kernel, tile=tile, chunk=chunk, n_hist_valid=n_hist_valid, d_model=d_model)
    return pl.pallas_call(
        kern,
        out_shape=(jax.ShapeDtypeStruct(x.shape, F32),
                   jax.ShapeDtypeStruct(hist.shape, F32),
                   jax.ShapeDtypeStruct(s0.shape, F32)),
        grid=(batch, length // tile),
        in_specs=[
            pl.BlockSpec((1, tile, d_model), lambda b, t: (b, t, 0)),
            pl.BlockSpec((1, POOL_HIST, POOL_WIDTH), lambda b, t: (b, 0, 0)),
            pl.BlockSpec((1, RET_HEADS, RET_DK, RET_DV), lambda b, t: (b, 0, 0, 0)),
            pl.BlockSpec((tile, RET_DK), lambda b, t: (t, 0)),
            pl.BlockSpec((tile, RET_DK), lambda b, t: (t, 0)),
            _const_spec((1, d_model)),
            _const_spec((d_model, in_width)),
            _const_spec(w["pool_w_grp"].shape),
            _const_spec((1, POOL_WIDTH)),
            _const_spec((POOL_WIDTH, d_model)),
            _const_spec((1, RET_HEADS * RET_DV)),
            _const_spec((RET_HEADS * RET_DV, d_model)),
            _const_spec((d_model, d_model)),
            _const_spec(decay.shape),
            _const_spec(q_decay.shape),
            _const_spec(k_decay.shape),
            pl.BlockSpec(memory_space=pltpu.SMEM),
        ],
        out_specs=(
            pl.BlockSpec((1, tile, d_model), lambda b, t: (b, t, 0)),
            pl.BlockSpec((1, POOL_HIST, POOL_WIDTH), lambda b, t: (b, 0, 0)),
            pl.BlockSpec((1, RET_HEADS, RET_DK, RET_DV), lambda b, t: (b, 0, 0, 0)),
        ),
        scratch_shapes=[
            pltpu.VMEM((POOL_HIST + 1 + tile, POOL_WIDTH), F32),
            pltpu.VMEM((RET_HEADS, RET_DK, RET_DV), F32),
            pltpu.VMEM((tile, in_width), F32),
            pltpu.VMEM((tile, RET_HEADS * RET_DV), BF16),
            pltpu.VMEM((tile, POOL_WIDTH), BF16),
        ],
        compiler_params=_cparams(2),
        name="mixer",
    )(x, hist, s0, cos, sin, w["norm_mix_w"], w["w_in"], w["pool_w_grp"], w["pool_scale"], w["w_pool_out"],
      w["ret_gn_w"], w["w_ret_out"], w["w_o"], decay, q_decay, k_decay, g_chunk)


def _mem_kv_kernel(mem_ref, nw_ref, wk_ref, wv_ref, k_ref, v_ref):
    mn = _rms(mem_ref[0], nw_ref[...]).astype(BF16)
    k_ref[0] = _dot(mn, wk_ref[...])
    v_ref[0] = _dot(mn, wv_ref[...])


def _mem_kv(mem, w):
    batch, m_len, d_model = mem.shape
    out = jax.ShapeDtypeStruct((batch, m_len, d_model), F32)
    blk = pl.BlockSpec((1, m_len, d_model), lambda b: (b, 0, 0))
    return pl.pallas_call(
        _mem_kv_kernel,
        out_shape=(out, out),
        grid=(batch,),
        in_specs=[blk, _const_spec((1, d_model)), _const_spec((d_model, d_model)), _const_spec((d_model, d_model))],
        out_specs=(blk, blk),
        compiler_params=_cparams(1),
        name="mem_kv",
    )(mem, w["norm_mem_w"], w["w_mk"], w["w_mv"])


def _cross_router_kernel(h_ref, mk_ref, mv_ref, ncw_ref, wcq_ref, wco_ref, nfw_ref, wrt_ref, brt_ref,
                         h2_ref, xn_ref, lp_ref, gate_ref, cnt_ref,
                         o_scr, h2_prev, *, tile, head_dim, n_experts):
    @pl.when(pl.program_id(0) == 0)
    def _():
        h2_prev[...] = jnp.zeros(h2_prev.shape, F32)

    d_model = h_ref.shape[2]
    half = d_model // 2
    h1 = h_ref[0]
    hn = _rms(h1, ncw_ref[...]).astype(BF16)
    q = _dot(hn, wcq_ref[...]).astype(BF16)
    xn = _rms(h2_prev[...], nfw_ref[...]).astype(BF16)
    xn_ref[0] = xn
    for hd in range(MEM_HEADS):
        cols = slice(hd * head_dim, (hd + 1) * head_dim)
        s = _dot_nt(q[:, cols], mk_ref[0, :, cols]) * (head_dim ** -0.5)
        e = jnp.exp(s - jnp.max(s, axis=-1, keepdims=True))
        p = e / jnp.sum(e, axis=-1, keepdims=True)
        o_scr[:, cols] = _dot(p.astype(BF16), mv_ref[0, :, cols]).astype(BF16)

    work = _dot_nt(wrt_ref[...], xn) + brt_ref[...]
    h2_a = h1[:, 0:half] + _dot(o_scr[...], wco_ref[:, 0:half])

    e_iota = lax.broadcasted_iota(I32, (n_experts, tile), 0)
    vals, sels = [], []
    for _ in range(TOP_K):
        m = jnp.max(work, axis=0, keepdims=True)
        first = jnp.min(jnp.where(work == m, e_iota, n_experts), axis=0, keepdims=True)
        sel = e_iota == first
        vals.append(m)
        sels.append(sel)
        work = jnp.where(sel, -jnp.inf, work)
    exps = [jnp.exp(v - vals[0]) for v in vals]
    denom = exps[0] + exps[1] + exps[2] + exps[3]
    assigned = jnp.zeros((n_experts, tile), F32)
    for sel in sels:
        assigned = assigned + sel.astype(F32)
    r_iota = lax.broadcasted_iota(I32, (tile, tile), 0)
    c_iota = lax.broadcasted_iota(I32, (tile, tile), 1)
    prior = _dot(assigned.astype(BF16), (r_iota < c_iota).astype(BF16))
    counts = jnp.sum(assigned, axis=1, keepdims=True)
    run = jnp.floor((counts + (ROW_UNIT - 1)) * (1.0 / ROW_UNIT)) * ROW_UNIT
    er = lax.broadcasted_iota(I32, (n_experts, n_experts), 0)
    ec = lax.broadcasted_iota(I32, (n_experts, n_experts), 1)
    run_start = _dot((ec < er).astype(BF16), jnp.broadcast_to(run, (n_experts, LANES)).astype(BF16))[:, 0:1]

    h2_b = h1[:, half:d_model] + _dot(o_scr[...], wco_ref[:, half:d_model])
    h2 = jnp.concatenate([h2_a, h2_b], axis=-1)
    h2_ref[0] = h2
    h2_prev[...] = h2

    for k in range(TOP_K):
        lp_ref[0, k:k + 1, :] = jnp.sum(jnp.where(sels[k], run_start + prior, 0.0), axis=0, keepdims=True).astype(I32)
        gate_ref[0, k:k + 1, :] = exps[k] / denom
    cnt_ref[0, 0] = jnp.broadcast_to(counts, (n_experts, LANES)).astype(I32)


def _cross_router(h1, mk, mv, w):
    batch, length, d_model = h1.shape
    m_len = mk.shape[1]
    head_dim = d_model // MEM_HEADS
    n_experts = w["w_router_t"].shape[0]
    tile = min(CROSS_TILE, length)
    assert length % tile == 0
    n_t = length // tile
    n_tiles = batch * n_t
    kern = functools.partial(_cross_router_kernel, tile=tile, head_dim=head_dim, n_experts=n_experts)

    def attended(i):
        return jnp.minimum(i, n_tiles - 1)

    def routed(i):
        return jnp.maximum(i - 1, 0)

    tok = pl.BlockSpec((1, tile, d_model), lambda i: (attended(i) // n_t, attended(i) % n_t, 0))
    tok_routed = pl.BlockSpec((1, tile, d_model), lambda i: (routed(i) // n_t, routed(i) % n_t, 0))
    mem = pl.BlockSpec((1, m_len, d_model), lambda i: (attended(i) // n_t, 0, 0))
    small = pl.BlockSpec((1, TOP_K, tile), lambda i: (routed(i) // n_t, 0, routed(i) % n_t))
    return pl.pallas_call(
        kern,
        out_shape=(jax.ShapeDtypeStruct(h1.shape, F32),
                   jax.ShapeDtypeStruct(h1.shape, BF16),
                   jax.ShapeDtypeStruct((batch, TOP_K, length), I32),
                   jax.ShapeDtypeStruct((batch, TOP_K, length), F32),
                   jax.ShapeDtypeStruct((batch, n_t, n_experts, LANES), I32)),
        grid=(n_tiles + 1,),
        in_specs=[tok, mem, mem,
                  _const_spec((1, d_model)), _const_spec((d_model, d_model)), _const_spec((d_model, d_model)),
                  _const_spec((1, d_model)), _const_spec((n_experts, d_model)), _const_spec((n_experts, 1))],
        out_specs=(tok, tok_routed, small, small,
                   pl.BlockSpec((1, 1, n_experts, LANES), lambda i: (routed(i) // n_t, routed(i) % n_t, 0, 0))),
        scratch_shapes=[pltpu.VMEM((tile, d_model), BF16), pltpu.VMEM((tile, d_model), F32)],
        compiler_params=_cparams(1),
        name="cross_router",
    )(h1, mk, mv, w["norm_cross_w"], w["w_cq"], w["w_co"], w["norm_ffn_w"], w["w_router_t"], w["b_router_c"])


def _unit_copy(src_ref, dst_ref, src_unit, dst_unit, sem, n_units=1):
    def rows_of(unit):
        start = unit * ROW_UNIT
        return pl.ds(start if isinstance(start, int) else pl.multiple_of(start, ROW_UNIT), n_units * ROW_UNIT)

    return pltpu.make_async_copy(src_ref.at[rows_of(src_unit)], dst_ref.at[rows_of(dst_unit)], sem)


def _run_copies(lists_ref, local_ref, global_ref, sem, n_big_max, n_small_max, to_global):
    def make(local_at, global_at, n_units):
        def copy_of(j):
            loc, glo = lists_ref[0, 0, local_at + j], lists_ref[0, 0, global_at + j]
            if to_global:
                return _unit_copy(local_ref, global_ref, loc, glo, sem, n_units)
            return _unit_copy(global_ref, local_ref, glo, loc, sem, n_units)
        return copy_of

    return (make(0, n_big_max, BIG_UNITS),
            make(2 * n_big_max, 2 * n_big_max + n_small_max, 1))


def _for_units(n, body):
    def group(g, c):
        for lane in range(DMA_UNROLL):
            body(g * DMA_UNROLL + lane, lane)
        return c

    n_groups = lax.div(n, DMA_UNROLL)
    lax.fori_loop(0, n_groups, group, 0)

    def rest(j, c):
        body(j, 0)
        return c

    lax.fori_loop(n_groups * DMA_UNROLL, n, rest, 0)


def _start_units(n, copy_of):
    _for_units(n, lambda j, lane: copy_of(j).start(priority=lane % 2))


def _wait_units(n, copy_of):
    _for_units(n, lambda j, lane: copy_of(j).wait())


def _pack_pairs(a, b):
    ua = lax.bitcast_convert_type(a, U32)
    ub = lax.bitcast_convert_type(b, U32)
    return lax.shift_right_logical(ua, U32(16)) | (ub & U32(HIGH_HALF))


def _unpack_pairs(u):
    a = lax.bitcast_convert_type(lax.shift_left(u, U32(16)), F32)
    b = lax.bitcast_convert_type(u & U32(HIGH_HALF), F32)
    return a, b


def _dispatch_kernel(nbig_ref, nsmall_ref, tail_ref, ntail_ref, lp_ref, gate_ref, xn_ref, lists_ref, listsp_ref,
                     *refs, tile, n_local, d_model, n_big_max, n_small_max, first_group):
    xs_ref, rows, zero_buf, sems, tail_sem = refs if first_group else refs[1:]
    i = pl.program_id(0)
    n_i = pl.num_programs(0)
    slot = lax.rem(i, 2)

    def for_tile(step, lists, at_slot, action):
        big, small = _run_copies(lists, rows.at[at_slot], xs_ref, sems.at[at_slot], n_big_max, n_small_max, True)
        action(nbig_ref[step], big)
        action(nsmall_ref[step], small)

    if first_group:
        @pl.when(i == 0)
        def _():
            zero_buf[...] = jnp.zeros(zero_buf.shape, U32)

            def tail_copy(j):
                return _unit_copy(zero_buf, xs_ref, 0, tail_ref[j], tail_sem)

            _start_units(ntail_ref[0], tail_copy)
            _wait_units(ntail_ref[0], tail_copy)

    half = d_model // 2
    xn = xn_ref[0]
    lps = [lp_ref[0, k:k + 1, :] for k in range(TOP_K)]
    gates = [gate_ref[0, k:k + 1, :] for k in range(TOP_K)]
    for c in range(n_local // PERM_CHUNK):
        r_iota = lax.broadcasted_iota(I32, (PERM_CHUNK, tile), 0) + c * PERM_CHUNK
        gmat = jnp.full((PERM_CHUNK, tile), -1.0, F32)
        for k in range(TOP_K):
            gmat = jnp.where(r_iota == lps[k], gates[k], gmat)
        perm = jnp.where(gmat >= 0.0, 1.0, 0.0).astype(BF16)
        wmat = jnp.maximum(gmat, 0.0)
        rs = pl.ds(c * PERM_CHUNK, PERM_CHUNK)
        xr = _dot(perm, xn)
        rows[slot, rs, 0:half] = _pack_pairs(xr[:, 0:half], xr[:, half:d_model])
        rows[slot, rs, half:half + LANES] = lax.bitcast_convert_type(
            jnp.broadcast_to(jnp.sum(wmat, axis=1, keepdims=True), (PERM_CHUNK, LANES)), U32)

    for_tile(i, lists_ref, slot, _start_units)

    @pl.when(i > 0)
    def _():
        for_tile(i - 1, listsp_ref, 1 - slot, _wait_units)

    @pl.when(i == n_i - 1)
    def _():
        for_tile(i, lists_ref, slot, _wait_units)


def _dispatch(lp, gate, xn, layout, plan, xs_prev):
    batch, length, d_model = xn.shape
    tile, n_local = plan["tile"], plan["n_local"]
    n_big_max, n_small_max = plan["n_big_max"], plan["n_small_max"]
    n_list = 2 * (n_big_max + n_small_max)
    n_t = length // tile
    width = d_model // 2 + LANES
    first_group = xs_prev is None
    small = pl.BlockSpec((1, TOP_K, tile), lambda i, *_: (i // n_t, 0, i % n_t))
    n_prefetch = 4
    in_specs = [small, small,
                pl.BlockSpec((1, tile, d_model), lambda i, *_: (i // n_t, i % n_t, 0)),
                pl.BlockSpec((1, 1, n_list), lambda i, *_: (i, 0, 0), memory_space=pltpu.SMEM),
                pl.BlockSpec((1, 1, n_list), lambda i, *_: (jnp.maximum(i - 1, 0), 0, 0),
                             memory_space=pltpu.SMEM)]
    args = [plan["n_big"], plan["n_small"], layout["tail_units"], layout["n_tail"], lp, gate, xn,
            plan["copy_lists"], plan["copy_lists"]]
    aliases = {}
    if not first_group:
        aliases = {len(args): 0}
        in_specs.append(pl.BlockSpec(memory_space=pl.ANY))
        args.append(xs_prev)
    grid_spec = pltpu.PrefetchScalarGridSpec(
        num_scalar_prefetch=n_prefetch,
        grid=(batch * n_t,),
        in_specs=in_specs,
        out_specs=pl.BlockSpec(memory_space=pl.ANY),
        scratch_shapes=[pltpu.VMEM((2, n_local, width), U32), pltpu.VMEM((ROW_UNIT, width), U32),
                        pltpu.SemaphoreType.DMA((2,)), pltpu.SemaphoreType.DMA(())],
    )
    return pl.pallas_call(
        functools.partial(_dispatch_kernel, tile=tile, n_local=n_local, d_model=d_model,
                          n_big_max=n_big_max, n_small_max=n_small_max, first_group=first_group),
        out_shape=jax.ShapeDtypeStruct((layout["n_rows"], width), U32),
        grid_spec=grid_spec,
        input_output_aliases=aliases,
        compiler_params=_cparams(1),
        name="dispatch",
    )(*args)


def _expert_kernel(be_ref, nb_ref, xs_ref, wgu_ref, bgu_ref, wd_ref, bd_ref, ys_ref, wgu_bf, wd_bf,
                   *, d_model, d_expert):
    b = pl.program_id(0)

    @pl.when(b < nb_ref[0])
    def _():
        @pl.when((b == 0) | (be_ref[b] != be_ref[jnp.maximum(b - 1, 0)]))
        def _():
            wgu_bf[...] = wgu_ref[0].astype(BF16)
            wd_bf[...] = wd_ref[0].astype(BF16)

        half = d_model // 2
        xa, xb = _unpack_pairs(xs_ref[:, 0:half])
        row_gate = lax.bitcast_convert_type(xs_ref[:, half:half + 1], F32)
        gu = _dot(jnp.concatenate([xa, xb], axis=-1).astype(BF16), wgu_bf[...]) + bgu_ref[0]
        gl = jnp.minimum(gu[:, :d_expert], SWIGLU_LIMIT)
        up = jnp.clip(gu[:, d_expert:], -SWIGLU_LIMIT, SWIGLU_LIMIT)
        act = (gl * jax.nn.sigmoid(SWIGLU_ALPHA * gl) * (up + 1.0)).astype(BF16)
        y = ((_dot(act, wd_bf[...]) + bd_ref[0]) * row_gate).astype(BF16).astype(F32)
        ys_ref[...] = _pack_pairs(y[:, 0:half], y[:, half:d_model])


def _experts(xs, block_e, n_used, w, blk):
    n_rows, width = xs.shape
    d_model = 2 * (width - LANES)
    d_expert = w["w_down"].shape[1]

    def row_map(b, be, nb):
        return (jnp.minimum(b, nb[0] - 1), 0)

    def exp_map(b, be, nb):
        return (be[b], 0, 0)

    grid_spec = pltpu.PrefetchScalarGridSpec(
        num_scalar_prefetch=2,
        grid=(n_rows // blk,),
        in_specs=[pl.BlockSpec((blk, width), row_map),
                  pl.BlockSpec((1, d_model, 2 * d_expert), exp_map),
                  pl.BlockSpec((1, 1, 2 * d_expert), exp_map),
                  pl.BlockSpec((1, d_expert, d_model), exp_map),
                  pl.BlockSpec((1, 1, d_model), exp_map)],
        out_specs=pl.BlockSpec((blk, d_model // 2), row_map),
        scratch_shapes=[pltpu.VMEM((d_model, 2 * d_expert), BF16), pltpu.VMEM((d_expert, d_model), BF16)],
    )
    return pl.pallas_call(
        functools.partial(_expert_kernel, d_model=d_model, d_expert=d_expert),
        out_shape=jax.ShapeDtypeStruct((n_rows, d_model // 2), U32),
        grid_spec=grid_spec,
        compiler_params=_cparams(1),
        name="experts",
    )(block_e, n_used, xs, w["w_gu"], w["b_gu"], w["w_down"], w["b_down"])


def _combine_kernel(nbig_ref, nsmall_ref, lpc_ref, h2_ref, nw_ref, lists_ref, listsn_ref, ys_ref, out_ref,
                    rows, sems, *, tile, n_local, final_norm, n_big_max, n_small_max):
    i = pl.program_id(0)
    n_i = pl.num_programs(0)
    slot = lax.rem(i, 2)

    def for_tile(step, lists, at_slot, action):
        big, small = _run_copies(lists, rows.at[at_slot], ys_ref, sems.at[at_slot], n_big_max, n_small_max, False)
        action(nbig_ref[step], big)
        action(nsmall_ref[step], small)

    @pl.when(i == 0)
    def _():
        rows[...] = jnp.zeros(rows.shape, U32)
        for_tile(0, lists_ref, 0, _start_units)

    @pl.when(i + 1 < n_i)
    def _():
        for_tile(i + 1, listsn_ref, 1 - slot, _start_units)

    for_tile(i, lists_ref, slot, _wait_units)

    lps = [lpc_ref[:, k:k + 1] for k in range(TOP_K)]
    half = out_ref.shape[1] // 2
    ya = jnp.zeros((tile, half), F32)
    yb = jnp.zeros((tile, half), F32)
    for c in range(n_local // PERM_CHUNK):
        c_iota = lax.broadcasted_iota(I32, (tile, PERM_CHUNK), 1) + c * PERM_CHUNK
        sel = jnp.zeros((tile, PERM_CHUNK), F32)
        for k in range(TOP_K):
            sel = jnp.where(c_iota == lps[k], 1.0, sel)
        sel = sel.astype(BF16)
        ra, rb = _unpack_pairs(rows[slot, pl.ds(c * PERM_CHUNK, PERM_CHUNK), :])
        ya = ya + _dot(sel, ra.astype(BF16))
        yb = yb + _dot(sel, rb.astype(BF16))
    h3 = h2_ref[...] + jnp.concatenate([ya, yb], axis=-1)
    out_ref[...] = _rms(h3, nw_ref[...]) if final_norm else h3


def _combine(lp_cols, h2, ys, norm_w, plan, final_norm):
    n_tok, d_model = h2.shape
    tile, n_local = plan["tile"], plan["n_local"]
    n_big_max, n_small_max = plan["n_big_max"], plan["n_small_max"]
    n_list = 2 * (n_big_max + n_small_max)
    n_tiles = n_tok // tile
    grid_spec = pltpu.PrefetchScalarGridSpec(
        num_scalar_prefetch=2,
        grid=(n_tiles,),
        in_specs=[pl.BlockSpec((tile, TOP_K), lambda i, *_: (i, 0)),
                  pl.BlockSpec((tile, d_model), lambda i, *_: (i, 0)),
                  _const_spec((1, d_model)),
                  pl.BlockSpec((1, 1, n_list), lambda i, *_: (i, 0, 0), memory_space=pltpu.SMEM),
                  pl.BlockSpec((1, 1, n_list), lambda i, *_: (jnp.minimum(i + 1, n_tiles - 1), 0, 0),
                               memory_space=pltpu.SMEM),
                  pl.BlockSpec(memory_space=pl.ANY)],
        out_specs=pl.BlockSpec((tile, d_model), lambda i, *_: (i, 0)),
        scratch_shapes=[pltpu.VMEM((2, n_local, d_model // 2), U32), pltpu.SemaphoreType.DMA((2,))],
    )
    return pl.pallas_call(
        functools.partial(_combine_kernel, tile=tile, n_local=n_local, final_norm=final_norm,
                          n_big_max=n_big_max, n_small_max=n_small_max),
        out_shape=jax.ShapeDtypeStruct((n_tok, d_model), F32),
        grid_spec=grid_spec,
        compiler_params=_cparams(1),
        name="combine",
    )(plan["n_big"], plan["n_small"], lp_cols, h2, norm_w, plan["copy_lists"], plan["copy_lists"], ys)


def _moe_plan(cnts, tiles):
    n_experts = cnts[0].shape[2]
    group_counts = [c[..., 0].reshape(-1, n_experts) for c in cnts]
    counts = jnp.concatenate(group_counts, axis=0)
    n_tiles = counts.shape[0]
    n_assigned = sum(g.shape[0] * t for g, t in zip(group_counts, tiles)) * TOP_K
    blk = EXPERT_BLOCK
    while blk > MIN_EXPERT_BLOCK and n_assigned < n_experts * blk:
        blk //= 2
    run_units = (counts + ROW_UNIT - 1) // ROW_UNIT
    expert_units = jnp.sum(run_units, axis=0)
    blk_units = blk // ROW_UNIT
    padded_units = (expert_units + blk_units - 1) // blk_units * blk_units
    expert_end = jnp.cumsum(padded_units)
    expert_start = expert_end - padded_units
    run_base = expert_start[None, :] + jnp.cumsum(run_units, axis=0) - run_units
    tj = jnp.arange(n_experts * (blk_units - 1))
    tail_len = padded_units - expert_units
    tail_end = jnp.cumsum(tail_len)
    tshift = expert_start + expert_units - (tail_end - tail_len)
    tpast = tj[:, None] >= tail_end[None, :-1]
    tail_units = tj + tshift[0] + jnp.sum(jnp.where(tpast, (tshift[1:] - tshift[:-1])[None, :], 0), axis=-1)
    n_tail = tail_end[-1]
    tail_units = jnp.where(tj < n_tail, tail_units, 0)
    n_blocks = (n_assigned + n_tiles * n_experts * (ROW_UNIT - 1) + n_experts * (blk - 1)) // blk
    block_e = jnp.minimum(jnp.sum((jnp.arange(n_blocks) * blk_units)[:, None] >= expert_end[None, :], axis=-1),
                          n_experts - 1)
    layout = {
        "n_rows": n_blocks * blk, "blk": blk,
        "tail_units": tail_units.astype(I32), "n_tail": n_tail.astype(I32).reshape(1),
        "block_e": block_e.astype(I32), "n_used": (expert_end[-1] // blk_units).astype(I32).reshape(1),
    }
    plans, first = [], 0
    for g, tile in zip(group_counts, tiles):
        last = first + g.shape[0]
        plans.append(_group_plan(run_units[first:last], run_base[first:last], tile))
        first = last
    return layout, plans


def _group_plan(run_units, run_base, tile):
    n_tiles, n_experts = run_units.shape
    unit_end = jnp.cumsum(run_units, axis=1)
    n_local = _round_up(TOP_K * tile + (ROW_UNIT - 1) * n_experts, PERM_CHUNK)
    n_units = n_local // ROW_UNIT

    def expand(ends, shift, n, stride):
        i = jnp.arange(n)
        step = shift[:, 1:] - shift[:, :-1]
        past = i[None, :, None] >= ends[:, None, :-1]
        value = stride * i[None, :] + shift[:, 0:1] + jnp.sum(jnp.where(past, step[:, None, :], 0), axis=-1)
        return jnp.where(i[None, :] < ends[:, -1:], value, 0)

    run_first = unit_end - run_units
    big = run_units // BIG_UNITS
    small = run_units - BIG_UNITS * big
    big_end = jnp.cumsum(big, axis=1)
    small_end = jnp.cumsum(small, axis=1)
    big_first = big_end - big
    small_first = small_end - small
    n_big_max = n_units // BIG_UNITS
    n_small_max = (BIG_UNITS - 1) * n_experts
    copy_lists = jnp.concatenate([
        expand(big_end, run_first - BIG_UNITS * big_first, n_big_max, BIG_UNITS),
        expand(big_end, run_base - BIG_UNITS * big_first, n_big_max, BIG_UNITS),
        expand(small_end, run_first + BIG_UNITS * big - small_first, n_small_max, 1),
        expand(small_end, run_base + BIG_UNITS * big - small_first, n_small_max, 1)], axis=1)
    return {
        "tile": tile, "n_local": n_local, "n_big_max": n_big_max, "n_small_max": n_small_max,
        "n_big": big_end[:, -1].astype(I32), "n_small": small_end[:, -1].astype(I32),
        "copy_lists": copy_lists.astype(I32).reshape(n_tiles, 1, 2 * (n_big_max + n_small_max)),
    }


def _moe(groups, w, norm_w, final_norm):
    cnts = [g[4] for g in groups]
    tiles = [g[0].shape[1] // g[4].shape[1] for g in groups]
    layout, plans = _moe_plan(cnts, tiles)
    xs = None
    for (h2, xn, lp, gate, cnt), plan in zip(groups, plans):
        xs = _dispatch(lp, gate, xn, layout, plan, xs)
    ys = _experts(xs, layout["block_e"], layout["n_used"], w, layout["blk"])
    outs = []
    for (h2, xn, lp, gate, cnt), plan in zip(groups, plans):
        batch, length, d_model = h2.shape
        lp_cols = lp.transpose(0, 2, 1).reshape(batch * length, TOP_K)
        out = _combine(lp_cols, h2.reshape(batch * length, d_model), ys, norm_w, plan, final_norm)
        outs.append(out.reshape(batch, length, d_model))
    return outs


def _pre_moe(x, hist, s0, pos0, n_hist_valid, mk, mv, w):
    h1, new_hist, s_new = _mixer(x, hist, s0, pos0, n_hist_valid, w)
    return _cross_router(h1, mk.astype(BF16), mv.astype(BF16), w), new_hist, s_new


def kernel(x_prompt, x_sample, mem_prompt, state_pool, state_ret, cache_mem_k, cache_mem_v, norm_mix_w, w_in, pool_w_grp, pool_scale, w_pool_out, ret_gn_w, w_ret_out, w_o, norm_mem_w, w_mk, w_mv, norm_cross_w, w_cq, w_co, norm_ffn_w, w_router, b_router, w_gu, b_gu, w_down, b_down, norm_final_w):
    depth = w_in.shape[0]
    batch_p = x_prompt.shape[0]
    m_len = mem_prompt.shape[1]
    d_model = x_prompt.shape[-1]
    hp, hs = x_prompt, x_sample
    norm_w = norm_final_w[None, :]
    mem_k_p, mem_v_p, pool_p, ret_p, pool_s, ret_s = [], [], [], [], [], []
    for l in range(depth):
        w = {
            "norm_mix_w": norm_mix_w[l][None, :], "w_in": w_in[l].astype(BF16),
            "pool_w_grp": pool_w_grp[l].astype(BF16), "pool_scale": pool_scale[l][None, :],
            "w_pool_out": w_pool_out[l].astype(BF16), "ret_gn_w": ret_gn_w[l][None, :],
            "w_ret_out": w_ret_out[l].astype(BF16), "w_o": w_o[l].astype(BF16),
            "norm_mem_w": norm_mem_w[l][None, :], "w_mk": w_mk[l].astype(BF16), "w_mv": w_mv[l].astype(BF16),
            "norm_cross_w": norm_cross_w[l][None, :], "w_cq": w_cq[l].astype(BF16), "w_co": w_co[l].astype(BF16),
            "norm_ffn_w": norm_ffn_w[l][None, :], "w_router_t": w_router[l].T.astype(BF16),
            "b_router_c": b_router[l][:, None],
            "w_gu": w_gu[l], "b_gu": b_gu[l][:, None, :],
            "w_down": w_down[l], "b_down": b_down[l][:, None, :],
        }
        last = l == depth - 1
        mk, mv = _mem_kv(mem_prompt, w)
        zero_hist = jnp.zeros((batch_p, POOL_HIST, POOL_WIDTH), F32)
        zero_state = jnp.zeros((batch_p, RET_HEADS, RET_DK, RET_DV), F32)
        routed_p, hist_p, s_p = _pre_moe(hp, zero_hist, zero_state, 0, 0, mk, mv, w)
        mem_k_p.append(mk.reshape(batch_p, m_len, MEM_HEADS, d_model // MEM_HEADS))
        mem_v_p.append(mv.reshape(batch_p, m_len, MEM_HEADS, d_model // MEM_HEADS))
        pool_p.append(hist_p)
        ret_p.append(s_p)
        ck = cache_mem_k[l].reshape(cache_mem_k.shape[1], m_len, d_model)
        cv = cache_mem_v[l].reshape(cache_mem_v.shape[1], m_len, d_model)
        routed_s, hist_s, s_s = _pre_moe(hs, state_pool[l], state_ret[l], PAST_LEN, POOL_HIST, ck, cv, w)
        pool_s.append(hist_s)
        ret_s.append(s_s)
        hp, hs = _moe([routed_p, routed_s], w, norm_w, last)
    return (hp, hs, jnp.stack(mem_k_p), jnp.stack(mem_v_p), jnp.stack(pool_p), jnp.stack(ret_p),
            jnp.stack(pool_s), jnp.stack(ret_s))
```

```python
import functools

import jax
import jax.numpy as jnp
from jax import lax
from jax.experimental import pallas as pl
from jax.experimental.pallas import tpu as pltpu

F32 = jnp.float32
BF16 = jnp.bfloat16
I32 = jnp.int32
U32 = jnp.uint32
HIGH_HALF = 0xFFFF0000

EPS = 1e-6
PAST_LEN = 1024
POOL_WINDOWS = (2, 4, 8, 16)
POOL_GROUP_DIM = 128
POOL_WIDTH = POOL_GROUP_DIM * len(POOL_WINDOWS)
POOL_HIST = max(POOL_WINDOWS) - 1
RET_HEADS = 4
RET_DK = 128
RET_DV = 256
ROPE_BASE = 10000.0
MEM_HEADS = 4
TOP_K = 4
SWIGLU_LIMIT = 7.0
SWIGLU_ALPHA = 1.702

SUBLANES = 8
LANES = 128
VMEM_LIMIT_BYTES = 56 * 1024 * 1024

MIXER_TILE = 512
RET_CHUNK = 256
CROSS_TILE = 512
ROPE_TILE = 512
EXPERT_BLOCK = 1024
MIN_EXPERT_BLOCK = 128
ROW_UNIT = SUBLANES
DISPATCH_CHUNK = 128
COMBINE_CHUNK = 256
assert COMBINE_CHUNK % DISPATCH_CHUNK == 0
BIG_UNITS = 4
DMA_UNROLL = 4


def _cparams(n_axes):
    return pltpu.CompilerParams(dimension_semantics=("arbitrary",) * n_axes,
                                vmem_limit_bytes=VMEM_LIMIT_BYTES)


def _const_spec(shape):
    nd = len(shape)
    return pl.BlockSpec(shape, lambda *_: (0,) * nd, pipeline_mode=pl.Buffered(1))


def _round_up(n, m):
    return (n + m - 1) // m * m


def _rms(x32, w_row):
    ms = jnp.mean(x32 * x32, axis=-1, keepdims=True)
    return x32 * lax.rsqrt(ms + EPS) * w_row


def _dot(a, b):
    return jnp.dot(a, b, preferred_element_type=F32)


def _dot_nt(a, b):
    return lax.dot_general(a, b, (((1,), (1,)), ((), ())), preferred_element_type=F32)


def _dot_tn(a, b):
    return lax.dot_general(a, b, (((0,), (0,)), ((), ())), preferred_element_type=F32)


def _rope_kernel(inv_ref, sign_ref, cos_ref, sin_ref, *, pos0, tile):
    i = pl.program_id(0)
    pos = (lax.broadcasted_iota(I32, (tile, RET_DK), 0) + (pos0 + i * tile)).astype(F32)
    ang = pos * inv_ref[...]
    cos_ref[...] = jnp.cos(ang)
    sin_ref[...] = jnp.sin(ang) * sign_ref[...]


def _rope_tables(length, pos0):
    half = RET_DK // 2
    inv = 1.0 / (ROPE_BASE ** (jnp.arange(half, dtype=F32) / half))
    inv2 = jnp.concatenate([inv, inv])[None, :]
    sign = jnp.concatenate([-jnp.ones((half,), F32), jnp.ones((half,), F32)])[None, :]
    tile = min(length, ROPE_TILE)
    assert length % tile == 0
    return pl.pallas_call(
        functools.partial(_rope_kernel, pos0=pos0, tile=tile),
        out_shape=(jax.ShapeDtypeStruct((length, RET_DK), F32),) * 2,
        grid=(length // tile,),
        in_specs=[pl.BlockSpec((1, RET_DK), lambda i: (0, 0))] * 2,
        out_specs=(pl.BlockSpec((tile, RET_DK), lambda i: (i, 0)),) * 2,
        compiler_params=_cparams(1),
        name="rope_tables",
    )(inv2, sign)


def _mixer_kernel(x_ref, hist_ref, s0_ref, cos_ref, sin_ref, nmw_ref, win_ref, pwg_ref, pscale_ref,
                  wpo_ref, gnw_ref, wro_ref, wo_ref, dec_ref, qd_ref, kd_ref, gc_ref,
                  h_ref, hist_out_ref, s_out_ref,
                  ubuf, s_scr, z_scr, r_scr, yp_scr, *, tile, chunk, n_hist_valid, d_model):
    t = pl.program_id(1)
    n_t = pl.num_programs(1)
    q_off = POOL_WIDTH
    k_off = q_off + RET_HEADS * RET_DK
    v_off = k_off + RET_HEADS * RET_DK
    g_off = v_off + RET_HEADS * RET_DV
    ap_off = g_off + RET_HEADS * RET_DV
    ar_off = ap_off + d_model

    @pl.when(t == 0)
    def _():
        ubuf[0:1, :] = jnp.zeros((1, POOL_WIDTH), F32)
        ubuf[1:1 + POOL_HIST, :] = hist_ref[0]
        s_scr[...] = s0_ref[0]

    n_chunks = tile // chunk
    hist_rows = POOL_HIST + 1
    for c in range(n_chunks):
        rows = slice(c * chunk, (c + 1) * chunk)
        xn = _rms(x_ref[0, rows, :], nmw_ref[...]).astype(BF16)
        z_scr[rows, :] = _dot(xn, win_ref[...])
        ubuf[hist_rows + c * chunk:hist_rows + (c + 1) * chunk, :] = z_scr[rows, 0:POOL_WIDTH]

    for c in range(n_chunks):
        rows = slice(c * chunk, (c + 1) * chunk)
        full = ubuf[c * chunk:c * chunk + hist_rows + chunk, :]
        pos = t * tile + c * chunk + lax.broadcasted_iota(I32, (chunk, 1), 0)
        ys = []
        for g, w in enumerate(POOL_WINDOWS):
            f = full[:, g * POOL_GROUP_DIM:(g + 1) * POOL_GROUP_DIM]
            s = f
            shift = 1
            while shift < w:
                s = s + pltpu.roll(s, shift, 0)
                shift *= 2
            cnt = jnp.minimum(w, pos + 1 + n_hist_valid).astype(F32)
            d = s[hist_rows:, :] / cnt - f[hist_rows:, :]
            ys.append(_dot(d.astype(BF16), pwg_ref[g]))
        yp_scr[rows, :] = (jnp.concatenate(ys, axis=-1) * pscale_ref[...]).astype(BF16)
    ubuf[1:1 + POOL_HIST, :] = ubuf[tile + 1:tile + 1 + POOL_HIST, :]

    for c in range(n_chunks):
        rows = slice(c * chunk, (c + 1) * chunk)
        cosc = cos_ref[rows, :]
        sinc = sin_ref[rows, :]
        for h in range(RET_HEADS):
            qh = z_scr[rows, q_off + h * RET_DK:q_off + (h + 1) * RET_DK]
            kh = z_scr[rows, k_off + h * RET_DK:k_off + (h + 1) * RET_DK]
            vb = z_scr[rows, v_off + h * RET_DV:v_off + (h + 1) * RET_DV].astype(BF16)
            qr = qh * cosc + pltpu.roll(qh, RET_DK // 2, 1) * sinc
            kr = (kh * cosc + pltpu.roll(kh, RET_DK // 2, 1) * sinc) * (RET_DK ** -0.5)
            qb = qr.astype(BF16)
            kb = kr.astype(BF16)
            scores = _dot_nt(qb, kb) * dec_ref[h]
            state = s_scr[h]
            o = _dot(scores.astype(BF16), vb) + _dot(qb, state.astype(BF16)) * qd_ref[h]
            s_scr[h] = gc_ref[h] * state + _dot_tn((kr * kd_ref[h]).astype(BF16), vb)
            mu = jnp.mean(o, axis=-1, keepdims=True)
            dlt = o - mu
            var = jnp.mean(dlt * dlt, axis=-1, keepdims=True)
            on = dlt * lax.rsqrt(var + EPS) * gnw_ref[:, h * RET_DV:(h + 1) * RET_DV]
            gate = z_scr[rows, g_off + h * RET_DV:g_off + (h + 1) * RET_DV]
            r_scr[rows, h * RET_DV:(h + 1) * RET_DV] = (on * (gate * jax.nn.sigmoid(gate))).astype(BF16)

    for c in range(n_chunks):
        rows = slice(c * chunk, (c + 1) * chunk)
        branch_pool = _dot(yp_scr[rows, :], wpo_ref[...])
        branch_ret = _dot(r_scr[rows, :], wro_ref[...])
        merged = (jax.nn.sigmoid(z_scr[rows, ap_off:ap_off + d_model]) * branch_pool
                  + jax.nn.sigmoid(z_scr[rows, ar_off:ar_off + d_model]) * branch_ret)
        h_ref[0, rows, :] = x_ref[0, rows, :] + _dot(merged.astype(BF16), wo_ref[...])

    @pl.when(t == n_t - 1)
    def _():
        hist_out_ref[0] = ubuf[1:1 + POOL_HIST, :]
        s_out_ref[0] = s_scr[...]


def _ret_tables(chunk):
    lg = jnp.log(1.0 - 2.0 ** (-5.0 - jnp.arange(RET_HEADS, dtype=F32)))
    idx = jnp.arange(chunk)
    rel = idx[:, None] - idx[None, :]
    decay = jnp.where(rel[None] >= 0, jnp.exp(jnp.maximum(rel, 0)[None].astype(F32) * lg[:, None, None]), 0.0)
    q_decay = jnp.exp((idx + 1).astype(F32)[None, :] * lg[:, None])[:, :, None]
    k_decay = jnp.exp((chunk - 1 - idx).astype(F32)[None, :] * lg[:, None])[:, :, None]
    g_chunk = jnp.exp(chunk * lg)
    return decay, q_decay, k_decay, g_chunk


def _mixer(x, hist, s0, pos0, n_hist_valid, w):
    batch, length, d_model = x.shape
    tile = min(MIXER_TILE, length)
    chunk = min(RET_CHUNK, tile)
    assert length % tile == 0 and tile % chunk == 0 and tile >= POOL_HIST + 1
    in_width = w["w_in"].shape[1]
    cos, sin = _rope_tables(length, pos0)
    decay, q_decay, k_decay, g_chunk = _ret_tables(chunk)
    kern = functools.partial(_mixer_kernel, tile=tile, chunk=chunk, n_hist_valid=n_hist_valid, d_model=d_model)
    return pl.pallas_call(
        kern,
        out_shape=(jax.ShapeDtypeStruct(x.shape, F32),
                   jax.ShapeDtypeStruct(hist.shape, F32),
                   jax.ShapeDtypeStruct(s0.shape, F32)),
        grid=(batch, length // tile),
        in_specs=[
            pl.BlockSpec((1, tile, d_model), lambda b, t: (b, t, 0)),
            pl.BlockSpec((1, POOL_HIST, POOL_WIDTH), lambda b, t: (b, 0, 0)),
            pl.BlockSpec((1, RET_HEADS, RET_DK, RET_DV), lambda b, t: (b, 0, 0, 0)),
            pl.BlockSpec((tile, RET_DK), lambda b, t: (t, 0)),
            pl.BlockSpec((tile, RET_DK), lambda b, t: (t, 0)),
            _const_spec((1, d_model)),
            _const_spec((d_model, in_width)),
            _const_spec(w["pool_w_grp"].shape),
            _const_spec((1, POOL_WIDTH)),
            _const_spec((POOL_WIDTH, d_model)),
            _const_spec((1, RET_HEADS * RET_DV)),
            _const_spec((RET_HEADS * RET_DV, d_model)),
            _const_spec((d_model, d_model)),
            _const_spec(decay.shape),
            _const_spec(q_decay.shape),
            _const_spec(k_decay.shape),
            pl.BlockSpec(memory_space=pltpu.SMEM),
        ],
        out_specs=(
            pl.BlockSpec((1, tile, d_model), lambda b, t: (b, t, 0)),
            pl.BlockSpec((1, POOL_HIST, POOL_WIDTH), lambda b, t: (b, 0, 0)),
            pl.BlockSpec((1, RET_HEADS, RET_DK, RET_DV), lambda b, t: (b, 0, 0, 0)),
        ),
        scratch_shapes=[
            pltpu.VMEM((POOL_HIST + 1 + tile, POOL_WIDTH), F32),
            pltpu.VMEM((RET_HEADS, RET_DK, RET_DV), F32),
            pltpu.VMEM((tile, in_width), F32),
            pltpu.VMEM((tile, RET_HEADS * RET_DV), BF16),
            pltpu.VMEM((tile, POOL_WIDTH), BF16),
        ],
        compiler_params=_cparams(2),
        name="mixer",
    )(x, hist, s0, cos, sin, w["norm_mix_w"], w["w_in"], w["pool_w_grp"], w["pool_scale"], w["w_pool_out"],
      w["ret_gn_w"], w["w_ret_out"], w["w_o"], decay, q_decay, k_decay, g_chunk)


def _mem_kv_kernel(mem_ref, nw_ref, wk_ref, wv_ref, k_ref, v_ref):
    mn = _rms(mem_ref[0], nw_ref[...]).astype(BF16)
    k_ref[0] = _dot(mn, wk_ref[...])
    v_ref[0] = _dot(mn, wv_ref[...])


def _mem_kv(mem, w):
    batch, m_len, d_model = mem.shape
    out = jax.ShapeDtypeStruct((batch, m_len, d_model), F32)
    blk = pl.BlockSpec((1, m_len, d_model), lambda b: (b, 0, 0))
    return pl.pallas_call(
        _mem_kv_kernel,
        out_shape=(out, out),
        grid=(batch,),
        in_specs=[blk, _const_spec((1, d_model)), _const_spec((d_model, d_model)), _const_spec((d_model, d_model))],
        out_specs=(blk, blk),
        compiler_params=_cparams(1),
        name="mem_kv",
    )(mem, w["norm_mem_w"], w["w_mk"], w["w_mv"])


def _cross_router_kernel(h_ref, mk_ref, mv_ref, ncw_ref, wcq_ref, wco_ref, nfw_ref, wrt_ref, brt_ref,
                         h2_ref, xn_ref, lp_ref, gate_ref, cnt_ref,
                         o_scr, h2_prev, *, tile, head_dim, n_experts):
    @pl.when(pl.program_id(0) == 0)
    def _():
        h2_prev[...] = jnp.zeros(h2_prev.shape, F32)

    d_model = h_ref.shape[2]
    half = d_model // 2
    h1 = h_ref[0]
    hn = _rms(h1, ncw_ref[...]).astype(BF16)
    q = _dot(hn, wcq_ref[...]).astype(BF16)
    xn = _rms(h2_prev[...], nfw_ref[...]).astype(BF16)
    xn_ref[0] = xn
    for hd in range(MEM_HEADS):
        cols = slice(hd * head_dim, (hd + 1) * head_dim)
        s = _dot_nt(q[:, cols], mk_ref[0, :, cols]) * (head_dim ** -0.5)
        e = jnp.exp(s - jnp.max(s, axis=-1, keepdims=True))
        p = e / jnp.sum(e, axis=-1, keepdims=True)
        o_scr[:, cols] = _dot(p.astype(BF16), mv_ref[0, :, cols]).astype(BF16)

    work = _dot_nt(wrt_ref[...], xn) + brt_ref[...]
    h2_a = h1[:, 0:half] + _dot(o_scr[...], wco_ref[:, 0:half])

    e_iota = lax.broadcasted_iota(I32, (n_experts, tile), 0)
    vals, sels = [], []
    for _ in range(TOP_K):
        m = jnp.max(work, axis=0, keepdims=True)
        first = jnp.min(jnp.where(work == m, e_iota, n_experts), axis=0, keepdims=True)
        sel = e_iota == first
        vals.append(m)
        sels.append(sel)
        work = jnp.where(sel, -jnp.inf, work)
    exps = [jnp.exp(v - vals[0]) for v in vals]
    denom = exps[0] + exps[1] + exps[2] + exps[3]
    assigned = jnp.zeros((n_experts, tile), F32)
    for sel in sels:
        assigned = assigned + sel.astype(F32)
    r_iota = lax.broadcasted_iota(I32, (tile, tile), 0)
    c_iota = lax.broadcasted_iota(I32, (tile, tile), 1)
    prior = _dot(assigned.astype(BF16), (r_iota < c_iota).astype(BF16))
    counts = jnp.sum(assigned, axis=1, keepdims=True)
    run = jnp.floor((counts + (ROW_UNIT - 1)) * (1.0 / ROW_UNIT)) * ROW_UNIT
    er = lax.broadcasted_iota(I32, (n_experts, n_experts), 0)
    ec = lax.broadcasted_iota(I32, (n_experts, n_experts), 1)
    run_start = _dot((ec < er).astype(BF16), jnp.broadcast_to(run, (n_experts, LANES)).astype(BF16))[:, 0:1]

    h2_b = h1[:, half:d_model] + _dot(o_scr[...], wco_ref[:, half:d_model])
    h2 = jnp.concatenate([h2_a, h2_b], axis=-1)
    h2_ref[0] = h2
    h2_prev[...] = h2

    for k in range(TOP_K):
        lp_ref[0, k:k + 1, :] = jnp.sum(jnp.where(sels[k], run_start + prior, 0.0), axis=0, keepdims=True).astype(I32)
        gate_ref[0, k:k + 1, :] = exps[k] / denom
    cnt_ref[0, 0] = jnp.broadcast_to(counts, (n_experts, LANES)).astype(I32)


def _cross_router(h1, mk, mv, w):
    batch, length, d_model = h1.shape
    m_len = mk.shape[1]
    head_dim = d_model // MEM_HEADS
    n_experts = w["w_router_t"].shape[0]
    tile = min(CROSS_TILE, length)
    assert length % tile == 0
    n_t = length // tile
    n_tiles = batch * n_t
    kern = functools.partial(_cross_router_kernel, tile=tile, head_dim=head_dim, n_experts=n_experts)

    def attended(i):
        return jnp.minimum(i, n_tiles - 1)

    def routed(i):
        return jnp.maximum(i - 1, 0)

    tok = pl.BlockSpec((1, tile, d_model), lambda i: (attended(i) // n_t, attended(i) % n_t, 0))
    tok_routed = pl.BlockSpec((1, tile, d_model), lambda i: (routed(i) // n_t, routed(i) % n_t, 0))
    mem = pl.BlockSpec((1, m_len, d_model), lambda i: (attended(i) // n_t, 0, 0))
    small = pl.BlockSpec((1, TOP_K, tile), lambda i: (routed(i) // n_t, 0, routed(i) % n_t))
    return pl.pallas_call(
        kern,
        out_shape=(jax.ShapeDtypeStruct(h1.shape, F32),
                   jax.ShapeDtypeStruct(h1.shape, BF16),
                   jax.ShapeDtypeStruct((batch, TOP_K, length), I32),
                   jax.ShapeDtypeStruct((batch, TOP_K, length), F32),
                   jax.ShapeDtypeStruct((batch, n_t, n_experts, LANES), I32)),
        grid=(n_tiles + 1,),
        in_specs=[tok, mem, mem,
                  _const_spec((1, d_model)), _const_spec((d_model, d_model)), _const_spec((d_model, d_model)),
                  _const_spec((1, d_model)), _const_spec((n_experts, d_model)), _const_spec((n_experts, 1))],
        out_specs=(tok, tok_routed, small, small,
                   pl.BlockSpec((1, 1, n_experts, LANES), lambda i: (routed(i) // n_t, routed(i) % n_t, 0, 0))),
        scratch_shapes=[pltpu.VMEM((tile, d_model), BF16), pltpu.VMEM((tile, d_model), F32)],
        compiler_params=_cparams(1),
        name="cross_router",
    )(h1, mk, mv, w["norm_cross_w"], w["w_cq"], w["w_co"], w["norm_ffn_w"], w["w_router_t"], w["b_router_c"])


def _unit_copy(src_ref, dst_ref, src_unit, dst_unit, sem, n_units=1):
    def rows_of(unit):
        start = unit * ROW_UNIT
        return pl.ds(start if isinstance(start, int) else pl.multiple_of(start, ROW_UNIT), n_units * ROW_UNIT)

    return pltpu.make_async_copy(src_ref.at[rows_of(src_unit)], dst_ref.at[rows_of(dst_unit)], sem)


def _run_copies(lists_ref, local_ref, global_ref, sem, n_big_max, n_small_max, to_global):
    def make(local_at, global_at, n_units):
        def copy_of(j):
            loc, glo = lists_ref[0, 0, local_at + j], lists_ref[0, 0, global_at + j]
            if to_global:
                return _unit_copy(local_ref, global_ref, loc, glo, sem, n_units)
            return _unit_copy(global_ref, local_ref, glo, loc, sem, n_units)
        return copy_of

    return (make(0, n_big_max, BIG_UNITS),
            make(2 * n_big_max, 2 * n_big_max + n_small_max, 1))


def _for_units(n, body):
    def group(g, c):
        for lane in range(DMA_UNROLL):
            body(g * DMA_UNROLL + lane, lane)
        return c

    n_groups = lax.div(n, DMA_UNROLL)
    lax.fori_loop(0, n_groups, group, 0)

    def rest(j, c):
        body(j, 0)
        return c

    lax.fori_loop(n_groups * DMA_UNROLL, n, rest, 0)


def _start_units(n, copy_of):
    _for_units(n, lambda j, lane: copy_of(j).start(priority=lane % 2))


def _wait_units(n, copy_of):
    _for_units(n, lambda j, lane: copy_of(j).wait())


def _pack_pairs(a, b):
    ua = lax.bitcast_convert_type(a, U32)
    ub = lax.bitcast_convert_type(b, U32)
    return lax.shift_right_logical(ua, U32(16)) | (ub & U32(HIGH_HALF))


def _unpack_pairs(u):
    a = lax.bitcast_convert_type(lax.shift_left(u, U32(16)), F32)
    b = lax.bitcast_convert_type(u & U32(HIGH_HALF), F32)
    return a, b


def _dispatch_kernel(nbig_ref, nsmall_ref, tail_ref, ntail_ref, lp_ref, gate_ref, xn_ref, lists_ref, listsp_ref,
                     *refs, tile, n_local, d_model, n_big_max, n_small_max, first_group):
    xs_ref, rows, zero_buf, sems, tail_sem = refs if first_group else refs[1:]
    i = pl.program_id(0)
    n_i = pl.num_programs(0)
    slot = lax.rem(i, 2)

    def for_tile(step, lists, at_slot, action):
        big, small = _run_copies(lists, rows.at[at_slot], xs_ref, sems.at[at_slot], n_big_max, n_small_max, True)
        action(nbig_ref[step], big)
        action(nsmall_ref[step], small)

    if first_group:
        @pl.when(i == 0)
        def _():
            zero_buf[...] = jnp.zeros(zero_buf.shape, U32)

            def tail_copy(j):
                return _unit_copy(zero_buf, xs_ref, 0, tail_ref[j], tail_sem)

            _start_units(ntail_ref[0], tail_copy)
            _wait_units(ntail_ref[0], tail_copy)

    half = d_model // 2
    xn = xn_ref[0]
    lps = [lp_ref[0, k:k + 1, :] for k in range(TOP_K)]
    gates = [gate_ref[0, k:k + 1, :] for k in range(TOP_K)]
    for c in range(n_local // DISPATCH_CHUNK):
        r_iota = lax.broadcasted_iota(I32, (DISPATCH_CHUNK, tile), 0) + c * DISPATCH_CHUNK
        gmat = jnp.full((DISPATCH_CHUNK, tile), -1.0, F32)
        for k in range(TOP_K):
            gmat = jnp.where(r_iota == lps[k], gates[k], gmat)
        perm = jnp.where(gmat >= 0.0, 1.0, 0.0).astype(BF16)
        wmat = jnp.maximum(gmat, 0.0)
        rs = pl.ds(c * DISPATCH_CHUNK, DISPATCH_CHUNK)
        xr = _dot(perm, xn)
        rows[slot, rs, 0:half] = _pack_pairs(xr[:, 0:half], xr[:, half:d_model])
        rows[slot, rs, half:half + LANES] = lax.bitcast_convert_type(
            jnp.broadcast_to(jnp.sum(wmat, axis=1, keepdims=True), (DISPATCH_CHUNK, LANES)), U32)

    for_tile(i, lists_ref, slot, _start_units)

    @pl.when(i > 0)
    def _():
        for_tile(i - 1, listsp_ref, 1 - slot, _wait_units)

    @pl.when(i == n_i - 1)
    def _():
        for_tile(i, lists_ref, slot, _wait_units)


def _dispatch(lp, gate, xn, layout, plan, xs_prev):
    batch, length, d_model = xn.shape
    tile, n_local = plan["tile"], plan["n_local"]
    n_big_max, n_small_max = plan["n_big_max"], plan["n_small_max"]
    n_list = 2 * (n_big_max + n_small_max)
    n_t = length // tile
    width = d_model // 2 + LANES
    first_group = xs_prev is None
    small = pl.BlockSpec((1, TOP_K, tile), lambda i, *_: (i // n_t, 0, i % n_t))
    n_prefetch = 4
    in_specs = [small, small,
                pl.BlockSpec((1, tile, d_model), lambda i, *_: (i // n_t, i % n_t, 0)),
                pl.BlockSpec((1, 1, n_list), lambda i, *_: (i, 0, 0), memory_space=pltpu.SMEM),
                pl.BlockSpec((1, 1, n_list), lambda i, *_: (jnp.maximum(i - 1, 0), 0, 0),
                             memory_space=pltpu.SMEM)]
    args = [plan["n_big"], plan["n_small"], layout["tail_units"], layout["n_tail"], lp, gate, xn,
            plan["copy_lists"], plan["copy_lists"]]
    aliases = {}
    if not first_group:
        aliases = {len(args): 0}
        in_specs.append(pl.BlockSpec(memory_space=pl.ANY))
        args.append(xs_prev)
    grid_spec = pltpu.PrefetchScalarGridSpec(
        num_scalar_prefetch=n_prefetch,
        grid=(batch * n_t,),
        in_specs=in_specs,
        out_specs=pl.BlockSpec(memory_space=pl.ANY),
        scratch_shapes=[pltpu.VMEM((2, n_local, width), U32), pltpu.VMEM((ROW_UNIT, width), U32),
                        pltpu.SemaphoreType.DMA((2,)), pltpu.SemaphoreType.DMA(())],
    )
    return pl.pallas_call(
        functools.partial(_dispatch_kernel, tile=tile, n_local=n_local, d_model=d_model,
                          n_big_max=n_big_max, n_small_max=n_small_max, first_group=first_group),
        out_shape=jax.ShapeDtypeStruct((layout["n_rows"], width), U32),
        grid_spec=grid_spec,
        input_output_aliases=aliases,
        compiler_params=_cparams(1),
        name="dispatch",
    )(*args)


def _expert_kernel(be_ref, nb_ref, xs_ref, wgu_ref, bgu_ref, wd_ref, bd_ref, ys_ref, wgu_bf, wd_bf,
                   *, d_model, d_expert):
    b = pl.program_id(0)

    @pl.when(b < nb_ref[0])
    def _():
        @pl.when((b == 0) | (be_ref[b] != be_ref[jnp.maximum(b - 1, 0)]))
        def _():
            wgu_bf[...] = wgu_ref[0].astype(BF16)
            wd_bf[...] = wd_ref[0].astype(BF16)

        half = d_model // 2
        xa, xb = _unpack_pairs(xs_ref[:, 0:half])
        row_gate = lax.bitcast_convert_type(xs_ref[:, half:half + 1], F32)
        gu = _dot(jnp.concatenate([xa, xb], axis=-1).astype(BF16), wgu_bf[...]) + bgu_ref[0]
        gl = jnp.minimum(gu[:, :d_expert], SWIGLU_LIMIT)
        up = jnp.clip(gu[:, d_expert:], -SWIGLU_LIMIT, SWIGLU_LIMIT)
        act = (gl * jax.nn.sigmoid(SWIGLU_ALPHA * gl) * (up + 1.0)).astype(BF16)
        y = ((_dot(act, wd_bf[...]) + bd_ref[0]) * row_gate).astype(BF16).astype(F32)
        ys_ref[...] = _pack_pairs(y[:, 0:half], y[:, half:d_model])


def _experts(xs, block_e, n_used, w, blk):
    n_rows, width = xs.shape
    d_model = 2 * (width - LANES)
    d_expert = w["w_down"].shape[1]

    def row_map(b, be, nb):
        return (jnp.minimum(b, nb[0] - 1), 0)

    def exp_map(b, be, nb):
        return (be[b], 0, 0)

    grid_spec = pltpu.PrefetchScalarGridSpec(
        num_scalar_prefetch=2,
        grid=(n_rows // blk,),
        in_specs=[pl.BlockSpec((blk, width), row_map),
                  pl.BlockSpec((1, d_model, 2 * d_expert), exp_map),
                  pl.BlockSpec((1, 1, 2 * d_expert), exp_map),
                  pl.BlockSpec((1, d_expert, d_model), exp_map),
                  pl.BlockSpec((1, 1, d_model), exp_map)],
        out_specs=pl.BlockSpec((blk, d_model // 2), row_map),
        scratch_shapes=[pltpu.VMEM((d_model, 2 * d_expert), BF16), pltpu.VMEM((d_expert, d_model), BF16)],
    )
    return pl.pallas_call(
        functools.partial(_expert_kernel, d_model=d_model, d_expert=d_expert),
        out_shape=jax.ShapeDtypeStruct((n_rows, d_model // 2), U32),
        grid_spec=grid_spec,
        compiler_params=_cparams(1),
        name="experts",
    )(block_e, n_used, xs, w["w_gu"], w["b_gu"], w["w_down"], w["b_down"])


def _combine_kernel(nbig_ref, nsmall_ref, lpc_ref, h2_ref, nw_ref, lists_ref, listsn_ref, ys_ref, out_ref,
                    rows, sems, *, tile, n_local, final_norm, n_big_max, n_small_max):
    i = pl.program_id(0)
    n_i = pl.num_programs(0)
    slot = lax.rem(i, 2)

    def for_tile(step, lists, at_slot, action):
        big, small = _run_copies(lists, rows.at[at_slot], ys_ref, sems.at[at_slot], n_big_max, n_small_max, False)
        action(nbig_ref[step], big)
        action(nsmall_ref[step], small)

    @pl.when(i == 0)
    def _():
        rows[...] = jnp.zeros(rows.shape, U32)
        for_tile(0, lists_ref, 0, _start_units)

    @pl.when(i + 1 < n_i)
    def _():
        for_tile(i + 1, listsn_ref, 1 - slot, _start_units)

    for_tile(i, lists_ref, slot, _wait_units)

    lps = [lpc_ref[:, k:k + 1] for k in range(TOP_K)]
    half = out_ref.shape[1] // 2
    ya = jnp.zeros((tile, half), F32)
    yb = jnp.zeros((tile, half), F32)
    for c in range(n_local // COMBINE_CHUNK):
        c_iota = lax.broadcasted_iota(I32, (tile, COMBINE_CHUNK), 1) + c * COMBINE_CHUNK
        sel = jnp.zeros((tile, COMBINE_CHUNK), F32)
        for k in range(TOP_K):
            sel = jnp.where(c_iota == lps[k], 1.0, sel)
        sel = sel.astype(BF16)
        ra, rb = _unpack_pairs(rows[slot, pl.ds(c * COMBINE_CHUNK, COMBINE_CHUNK), :])
        ya = ya + _dot(sel, ra.astype(BF16))
        yb = yb + _dot(sel, rb.astype(BF16))
    h3 = h2_ref[...] + jnp.concatenate([ya, yb], axis=-1)
    out_ref[...] = _rms(h3, nw_ref[...]) if final_norm else h3


def _combine(lp_cols, h2, ys, norm_w, plan, final_norm):
    n_tok, d_model = h2.shape
    tile, n_local = plan["tile"], plan["n_local"]
    n_big_max, n_small_max = plan["n_big_max"], plan["n_small_max"]
    n_list = 2 * (n_big_max + n_small_max)
    n_tiles = n_tok // tile
    grid_spec = pltpu.PrefetchScalarGridSpec(
        num_scalar_prefetch=2,
        grid=(n_tiles,),
        in_specs=[pl.BlockSpec((tile, TOP_K), lambda i, *_: (i, 0)),
                  pl.BlockSpec((tile, d_model), lambda i, *_: (i, 0)),
                  _const_spec((1, d_model)),
                  pl.BlockSpec((1, 1, n_list), lambda i, *_: (i, 0, 0), memory_space=pltpu.SMEM),
                  pl.BlockSpec((1, 1, n_list), lambda i, *_: (jnp.minimum(i + 1, n_tiles - 1), 0, 0),
                               memory_space=pltpu.SMEM),
                  pl.BlockSpec(memory_space=pl.ANY)],
        out_specs=pl.BlockSpec((tile, d_model), lambda i, *_: (i, 0)),
        scratch_shapes=[pltpu.VMEM((2, n_local, d_model // 2), U32), pltpu.SemaphoreType.DMA((2,))],
    )
    return pl.pallas_call(
        functools.partial(_combine_kernel, tile=tile, n_local=n_local, final_norm=final_norm,
                          n_big_max=n_big_max, n_small_max=n_small_max),
        out_shape=jax.ShapeDtypeStruct((n_tok, d_model), F32),
        grid_spec=grid_spec,
        compiler_params=_cparams(1),
        name="combine",
    )(plan["n_big"], plan["n_small"], lp_cols, h2, norm_w, plan["copy_lists"], plan["copy_lists"], ys)


def _moe_plan(cnts, tiles):
    n_experts = cnts[0].shape[2]
    group_counts = [c[..., 0].reshape(-1, n_experts) for c in cnts]
    counts = jnp.concatenate(group_counts, axis=0)
    n_tiles = counts.shape[0]
    n_assigned = sum(g.shape[0] * t for g, t in zip(group_counts, tiles)) * TOP_K
    blk = EXPERT_BLOCK
    while blk > MIN_EXPERT_BLOCK and n_assigned < n_experts * blk:
        blk //= 2
    run_units = (counts + ROW_UNIT - 1) // ROW_UNIT
    expert_units = jnp.sum(run_units, axis=0)
    blk_units = blk // ROW_UNIT
    padded_units = (expert_units + blk_units - 1) // blk_units * blk_units
    expert_end = jnp.cumsum(padded_units)
    expert_start = expert_end - padded_units
    run_base = expert_start[None, :] + jnp.cumsum(run_units, axis=0) - run_units
    tj = jnp.arange(n_experts * (blk_units - 1))
    tail_len = padded_units - expert_units
    tail_end = jnp.cumsum(tail_len)
    tshift = expert_start + expert_units - (tail_end - tail_len)
    tpast = tj[:, None] >= tail_end[None, :-1]
    tail_units = tj + tshift[0] + jnp.sum(jnp.where(tpast, (tshift[1:] - tshift[:-1])[None, :], 0), axis=-1)
    n_tail = tail_end[-1]
    tail_units = jnp.where(tj < n_tail, tail_units, 0)
    n_blocks = (n_assigned + n_tiles * n_experts * (ROW_UNIT - 1) + n_experts * (blk - 1)) // blk
    block_e = jnp.minimum(jnp.sum((jnp.arange(n_blocks) * blk_units)[:, None] >= expert_end[None, :], axis=-1),
                          n_experts - 1)
    layout = {
        "n_rows": n_blocks * blk, "blk": blk,
        "tail_units": tail_units.astype(I32), "n_tail": n_tail.astype(I32).reshape(1),
        "block_e": block_e.astype(I32), "n_used": (expert_end[-1] // blk_units).astype(I32).reshape(1),
    }
    plans, first = [], 0
    for g, tile in zip(group_counts, tiles):
        last = first + g.shape[0]
        plans.append(_group_plan(run_units[first:last], run_base[first:last], tile))
        first = last
    return layout, plans


def _group_plan(run_units, run_base, tile):
    n_tiles, n_experts = run_units.shape
    unit_end = jnp.cumsum(run_units, axis=1)
    n_local = _round_up(TOP_K * tile + (ROW_UNIT - 1) * n_experts, COMBINE_CHUNK)
    n_units = n_local // ROW_UNIT

    def expand(ends, shift, n, stride):
        i = jnp.arange(n)
        step = shift[:, 1:] - shift[:, :-1]
        past = i[None, :, None] >= ends[:, None, :-1]
        value = stride * i[None, :] + shift[:, 0:1] + jnp.sum(jnp.where(past, step[:, None, :], 0), axis=-1)
        return jnp.where(i[None, :] < ends[:, -1:], value, 0)

    run_first = unit_end - run_units
    big = run_units // BIG_UNITS
    small = run_units - BIG_UNITS * big
    big_end = jnp.cumsum(big, axis=1)
    small_end = jnp.cumsum(small, axis=1)
    big_first = big_end - big
    small_first = small_end - small
    n_big_max = n_units // BIG_UNITS
    n_small_max = (BIG_UNITS - 1) * n_experts
    copy_lists = jnp.concatenate([
        expand(big_end, run_first - BIG_UNITS * big_first, n_big_max, BIG_UNITS),
        expand(big_end, run_base - BIG_UNITS * big_first, n_big_max, BIG_UNITS),
        expand(small_end, run_first + BIG_UNITS * big - small_first, n_small_max, 1),
        expand(small_end, run_base + BIG_UNITS * big - small_first, n_small_max, 1)], axis=1)
    return {
        "tile": tile, "n_local": n_local, "n_big_max": n_big_max, "n_small_max": n_small_max,
        "n_big": big_end[:, -1].astype(I32), "n_small": small_end[:, -1].astype(I32),
        "copy_lists": copy_lists.astype(I32).reshape(n_tiles, 1, 2 * (n_big_max + n_small_max)),
    }


def _moe(groups, w, norm_w, final_norm):
    cnts = [g[4] for g in groups]
    tiles = [g[0].shape[1] // g[4].shape[1] for g in groups]
    layout, plans = _moe_plan(cnts, tiles)
    xs = None
    for (h2, xn, lp, gate, cnt), plan in zip(groups, plans):
        xs = _dispatch(lp, gate, xn, layout, plan, xs)
    ys = _experts(xs, layout["block_e"], layout["n_used"], w, layout["blk"])
    outs = []
    for (h2, xn, lp, gate, cnt), plan in zip(groups, plans):
        batch, length, d_model = h2.shape
        lp_cols = lp.transpose(0, 2, 1).reshape(batch * length, TOP_K)
        out = _combine(lp_cols, h2.reshape(batch * length, d_model), ys, norm_w, plan, final_norm)
        outs.append(out.reshape(batch, length, d_model))
    return outs


def _pre_moe(x, hist, s0, pos0, n_hist_valid, mk, mv, w):
    h1, new_hist, s_new = _mixer(x, hist, s0, pos0, n_hist_valid, w)
    return _cross_router(h1, mk.astype(BF16), mv.astype(BF16), w), new_hist, s_new


def kernel(x_prompt, x_sample, mem_prompt, state_pool, state_ret, cache_mem_k, cache_mem_v, norm_mix_w, w_in, pool_w_grp, pool_scale, w_pool_out, ret_gn_w, w_ret_out, w_o, norm_mem_w, w_mk, w_mv, norm_cross_w, w_cq, w_co, norm_ffn_w, w_router, b_router, w_gu, b_gu, w_down, b_down, norm_final_w):
    depth = w_in.shape[0]
    batch_p = x_prompt.shape[0]
    m_len = mem_prompt.shape[1]
    d_model = x_prompt.shape[-1]
    hp, hs = x_prompt, x_sample
    norm_w = norm_final_w[None, :]
    mem_k_p, mem_v_p, pool_p, ret_p, pool_s, ret_s = [], [], [], [], [], []
    for l in range(depth):
        w = {
            "norm_mix_w": norm_mix_w[l][None, :], "w_in": w_in[l].astype(BF16),
            "pool_w_grp": pool_w_grp[l].astype(BF16), "pool_scale": pool_scale[l][None, :],
            "w_pool_out": w_pool_out[l].astype(BF16), "ret_gn_w": ret_gn_w[l][None, :],
            "w_ret_out": w_ret_out[l].astype(BF16), "w_o": w_o[l].astype(BF16),
            "norm_mem_w": norm_mem_w[l][None, :], "w_mk": w_mk[l].astype(BF16), "w_mv": w_mv[l].astype(BF16),
            "norm_cross_w": norm_cross_w[l][None, :], "w_cq": w_cq[l].astype(BF16), "w_co": w_co[l].astype(BF16),
            "norm_ffn_w": norm_ffn_w[l][None, :], "w_router_t": w_router[l].T.astype(BF16),
            "b_router_c": b_router[l][:, None],
            "w_gu": w_gu[l], "b_gu": b_gu[l][:, None, :],
            "w_down": w_down[l], "b_down": b_down[l][:, None, :],
        }
        last = l == depth - 1
        mk, mv = _mem_kv(mem_prompt, w)
        zero_hist = jnp.zeros((batch_p, POOL_HIST, POOL_WIDTH), F32)
        zero_state = jnp.zeros((batch_p, RET_HEADS, RET_DK, RET_DV), F32)
        routed_p, hist_p, s_p = _pre_moe(hp, zero_hist, zero_state, 0, 0, mk, mv, w)
        mem_k_p.append(mk.reshape(batch_p, m_len, MEM_HEADS, d_model // MEM_HEADS))
        mem_v_p.append(mv.reshape(batch_p, m_len, MEM_HEADS, d_model // MEM_HEADS))
        pool_p.append(hist_p)
        ret_p.append(s_p)
        ck = cache_mem_k[l].reshape(cache_mem_k.shape[1], m_len, d_model)
        cv = cache_mem_v[l].reshape(cache_mem_v.shape[1], m_len, d_model)
        routed_s, hist_s, s_s = _pre_moe(hs, state_pool[l], state_ret[l], PAST_LEN, POOL_HIST, ck, cv, w)
        pool_s.append(hist_s)
        ret_s.append(s_s)
        hp, hs = _moe([routed_p, routed_s], w, norm_w, last)
    return (hp, hs, jnp.stack(mem_k_p), jnp.stack(mem_v_p), jnp.stack(pool_p), jnp.stack(ret_p),
            jnp.stack(pool_s), jnp.stack(ret_s))
```

```python
import functools

import jax
import jax.numpy as jnp
from jax import lax
from jax.experimental import pallas as pl
from jax.experimental.pallas import tpu as pltpu

F32 = jnp.float32
BF16 = jnp.bfloat16
I32 = jnp.int32
U32 = jnp.uint32
HIGH_HALF = 0xFFFF0000

EPS = 1e-6
PAST_LEN = 1024
POOL_WINDOWS = (2, 4, 8, 16)
POOL_GROUP_DIM = 128
POOL_WIDTH = POOL_GROUP_DIM * len(POOL_WINDOWS)
POOL_HIST = max(POOL_WINDOWS) - 1
RET_HEADS = 4
RET_DK = 128
RET_DV = 256
ROPE_BASE = 10000.0
MEM_HEADS = 4
TOP_K = 4
SWIGLU_LIMIT = 7.0
SWIGLU_ALPHA = 1.702

SUBLANES = 8
LANES = 128
VMEM_LIMIT_BYTES = 56 * 1024 * 1024

MIXER_TILE = 512
RET_CHUNK = 256
CROSS_TILE = 512
ROPE_TILE = 512
EXPERT_BLOCK = 1024
MIN_EXPERT_BLOCK = 128
ROW_UNIT = SUBLANES
DISPATCH_CHUNK = 128
COMBINE_CHUNK = 256
assert COMBINE_CHUNK % DISPATCH_CHUNK == 0
BIG_UNITS = 4
FILL_UNITS = MIN_EXPERT_BLOCK // ROW_UNIT
DMA_UNROLL = 4


def _cparams(n_axes):
    return pltpu.CompilerParams(dimension_semantics=("arbitrary",) * n_axes,
                                vmem_limit_bytes=VMEM_LIMIT_BYTES)


def _const_spec(shape):
    nd = len(shape)
    return pl.BlockSpec(shape, lambda *_: (0,) * nd, pipeline_mode=pl.Buffered(1))


def _round_up(n, m):
    return (n + m - 1) // m * m


def _rms(x32, w_row):
    ms = jnp.mean(x32 * x32, axis=-1, keepdims=True)
    return x32 * lax.rsqrt(ms + EPS) * w_row


def _dot(a, b):
    return jnp.dot(a, b, preferred_element_type=F32)


def _dot_nt(a, b):
    return lax.dot_general(a, b, (((1,), (1,)), ((), ())), preferred_element_type=F32)


def _dot_tn(a, b):
    return lax.dot_general(a, b, (((0,), (0,)), ((), ())), preferred_element_type=F32)


def _rope_kernel(inv_ref, sign_ref, cos_ref, sin_ref, *, pos0, tile):
    i = pl.program_id(0)
    pos = (lax.broadcasted_iota(I32, (tile, RET_DK), 0) + (pos0 + i * tile)).astype(F32)
    ang = pos * inv_ref[...]
    cos_ref[...] = jnp.cos(ang)
    sin_ref[...] = jnp.sin(ang) * sign_ref[...]


def _rope_tables(length, pos0):
    half = RET_DK // 2
    inv = 1.0 / (ROPE_BASE ** (jnp.arange(half, dtype=F32) / half))
    inv2 = jnp.concatenate([inv, inv])[None, :]
    sign = jnp.concatenate([-jnp.ones((half,), F32), jnp.ones((half,), F32)])[None, :]
    tile = min(length, ROPE_TILE)
    assert length % tile == 0
    return pl.pallas_call(
        functools.partial(_rope_kernel, pos0=pos0, tile=tile),
        out_shape=(jax.ShapeDtypeStruct((length, RET_DK), F32),) * 2,
        grid=(length // tile,),
        in_specs=[pl.BlockSpec((1, RET_DK), lambda i: (0, 0))] * 2,
        out_specs=(pl.BlockSpec((tile, RET_DK), lambda i: (i, 0)),) * 2,
        compiler_params=_cparams(1),
        name="rope_tables",
    )(inv2, sign)


def _mixer_kernel(x_ref, hist_ref, s0_ref, cos_ref, sin_ref, nmw_ref, win_ref, pwg_ref, pscale_ref,
                  wpo_ref, gnw_ref, wro_ref, wo_ref, dec_ref, qd_ref, kd_ref, gc_ref,
                  h_ref, hist_out_ref, s_out_ref,
                  ubuf, s_scr, z_scr, r_scr, yp_scr, *, tile, chunk, n_hist_valid, d_model):
    t = pl.program_id(1)
    n_t = pl.num_programs(1)
    q_off = POOL_WIDTH
    k_off = q_off + RET_HEADS * RET_DK
    v_off = k_off + RET_HEADS * RET_DK
    g_off = v_off + RET_HEADS * RET_DV
    ap_off = g_off + RET_HEADS * RET_DV
    ar_off = ap_off + d_model

    @pl.when(t == 0)
    def _():
        ubuf[0:1, :] = jnp.zeros((1, POOL_WIDTH), F32)
        ubuf[1:1 + POOL_HIST, :] = hist_ref[0]
        s_scr[...] = s0_ref[0]

    n_chunks = tile // chunk
    hist_rows = POOL_HIST + 1
    for c in range(n_chunks):
        rows = slice(c * chunk, (c + 1) * chunk)
        xn = _rms(x_ref[0, rows, :], nmw_ref[...]).astype(BF16)
        z_scr[rows, :] = _dot(xn, win_ref[...])
        ubuf[hist_rows + c * chunk:hist_rows + (c + 1) * chunk, :] = z_scr[rows, 0:POOL_WIDTH]

    for c in range(n_chunks):
        rows = slice(c * chunk, (c + 1) * chunk)
        full = ubuf[c * chunk:c * chunk + hist_rows + chunk, :]
        pos = t * tile + c * chunk + lax.broadcasted_iota(I32, (chunk, 1), 0)
        ys = []
        for g, w in enumerate(POOL_WINDOWS):
            f = full[:, g * POOL_GROUP_DIM:(g + 1) * POOL_GROUP_DIM]
            s = f
            shift = 1
            while shift < w:
                s = s + pltpu.roll(s, shift, 0)
                shift *= 2
            cnt = jnp.minimum(w, pos + 1 + n_hist_valid).astype(F32)
            d = s[hist_rows:, :] / cnt - f[hist_rows:, :]
            ys.append(_dot(d.astype(BF16), pwg_ref[g]))
        yp_scr[rows, :] = (jnp.concatenate(ys, axis=-1) * pscale_ref[...]).astype(BF16)
    ubuf[1:1 + POOL_HIST, :] = ubuf[tile + 1:tile + 1 + POOL_HIST, :]

    for c in range(n_chunks):
        rows = slice(c * chunk, (c + 1) * chunk)
        cosc = cos_ref[rows, :]
        sinc = sin_ref[rows, :]
        for h in range(RET_HEADS):
            qh = z_scr[rows, q_off + h * RET_DK:q_off + (h + 1) * RET_DK]
            kh = z_scr[rows, k_off + h * RET_DK:k_off + (h + 1) * RET_DK]
            vb = z_scr[rows, v_off + h * RET_DV:v_off + (h + 1) * RET_DV].astype(BF16)
            qr = qh * cosc + pltpu.roll(qh, RET_DK // 2, 1) * sinc
            kr = (kh * cosc + pltpu.roll(kh, RET_DK // 2, 1) * sinc) * (RET_DK ** -0.5)
            qb = qr.astype(BF16)
            kb = kr.astype(BF16)
            scores = _dot_nt(qb, kb) * dec_ref[h]
            state = s_scr[h]
            o = _dot(scores.astype(BF16), vb) + _dot(qb, state.astype(BF16)) * qd_ref[h]
            s_scr[h] = gc_ref[h] * state + _dot_tn((kr * kd_ref[h]).astype(BF16), vb)
            mu = jnp.mean(o, axis=-1, keepdims=True)
            dlt = o - mu
            var = jnp.mean(dlt * dlt, axis=-1, keepdims=True)
            on = dlt * lax.rsqrt(var + EPS) * gnw_ref[:, h * RET_DV:(h + 1) * RET_DV]
            gate = z_scr[rows, g_off + h * RET_DV:g_off + (h + 1) * RET_DV]
            r_scr[rows, h * RET_DV:(h + 1) * RET_DV] = (on * (gate * jax.nn.sigmoid(gate))).astype(BF16)

    for c in range(n_chunks):
        rows = slice(c * chunk, (c + 1) * chunk)
        branch_pool = _dot(yp_scr[rows, :], wpo_ref[...])
        branch_ret = _dot(r_scr[rows, :], wro_ref[...])
        merged = (jax.nn.sigmoid(z_scr[rows, ap_off:ap_off + d_model]) * branch_pool
                  + jax.nn.sigmoid(z_scr[rows, ar_off:ar_off + d_model]) * branch_ret)
        h_ref[0, rows, :] = x_ref[0, rows, :] + _dot(merged.astype(BF16), wo_ref[...])

    @pl.when(t == n_t - 1)
    def _():
        hist_out_ref[0] = ubuf[1:1 + POOL_HIST, :]
        s_out_ref[0] = s_scr[...]


def _ret_tables(chunk):
    lg = jnp.log(1.0 - 2.0 ** (-5.0 - jnp.arange(RET_HEADS, dtype=F32)))
    idx = jnp.arange(chunk)
    rel = idx[:, None] - idx[None, :]
    decay = jnp.where(rel[None] >= 0, jnp.exp(jnp.maximum(rel, 0)[None].astype(F32) * lg[:, None, None]), 0.0)
    q_decay = jnp.exp((idx + 1).astype(F32)[None, :] * lg[:, None])[:, :, None]
    k_decay = jnp.exp((chunk - 1 - idx).astype(F32)[None, :] * lg[:, None])[:, :, None]
    g_chunk = jnp.exp(chunk * lg)
    return decay, q_decay, k_decay, g_chunk


def _mixer(x, hist, s0, pos0, n_hist_valid, w):
    batch, length, d_model = x.shape
    tile = min(MIXER_TILE, length)
    chunk = min(RET_CHUNK, tile)
    assert length % tile == 0 and tile % chunk == 0 and tile >= POOL_HIST + 1
    in_width = w["w_in"].shape[1]
    cos, sin = _rope_tables(length, pos0)
    decay, q_decay, k_decay, g_chunk = _ret_tables(chunk)
    kern = functools.partial(_mixer_kernel, tile=tile, chunk=chunk, n_hist_valid=n_hist_valid, d_model=d_model)
    return pl.pallas_call(
        kern,
        out_shape=(jax.ShapeDtypeStruct(x.shape, F32),
                   jax.ShapeDtypeStruct(hist.shape, F32),
                   jax.ShapeDtypeStruct(s0.shape, F32)),
        grid=(batch, length // tile),
        in_specs=[
            pl.BlockSpec((1, tile, d_model), lambda b, t: (b, t, 0)),
            pl.BlockSpec((1, POOL_HIST, POOL_WIDTH), lambda b, t: (b, 0, 0)),
            pl.BlockSpec((1, RET_HEADS, RET_DK, RET_DV), lambda b, t: (b, 0, 0, 0)),
            pl.BlockSpec((tile, RET_DK), lambda b, t: (t, 0)),
            pl.BlockSpec((tile, RET_DK), lambda b, t: (t, 0)),
            _const_spec((1, d_model)),
            _const_spec((d_model, in_width)),
            _const_spec(w["pool_w_grp"].shape),
            _const_spec((1, POOL_WIDTH)),
            _const_spec((POOL_WIDTH, d_model)),
            _const_spec((1, RET_HEADS * RET_DV)),
            _const_spec((RET_HEADS * RET_DV, d_model)),
            _const_spec((d_model, d_model)),
            _const_spec(decay.shape),
            _const_spec(q_decay.shape),
            _const_spec(k_decay.shape),
            pl.BlockSpec(memory_space=pltpu.SMEM),
        ],
        out_specs=(
            pl.BlockSpec((1, tile, d_model), lambda b, t: (b, t, 0)),
            pl.BlockSpec((1, POOL_HIST, POOL_WIDTH), lambda b, t: (b, 0, 0)),
            pl.BlockSpec((1, RET_HEADS, RET_DK, RET_DV), lambda b, t: (b, 0, 0, 0)),
        ),
        scratch_shapes=[
            pltpu.VMEM((POOL_HIST + 1 + tile, POOL_WIDTH), F32),
            pltpu.VMEM((RET_HEADS, RET_DK, RET_DV), F32),
            pltpu.VMEM((tile, in_width), F32),
            pltpu.VMEM((tile, RET_HEADS * RET_DV), BF16),
            pltpu.VMEM((tile, POOL_WIDTH), BF16),
        ],
        compiler_params=_cparams(2),
        name="mixer",
    )(x, hist, s0, cos, sin, w["norm_mix_w"], w["w_in"], w["pool_w_grp"], w["pool_scale"], w["w_pool_out"],
      w["ret_gn_w"], w["w_ret_out"], w["w_o"], decay, q_decay, k_decay, g_chunk)


def _mem_kv_kernel(mem_ref, nw_ref, wk_ref, wv_ref, k_ref, v_ref):
    mn = _rms(mem_ref[0], nw_ref[...]).astype(BF16)
    k_ref[0] = _dot(mn, wk_ref[...])
    v_ref[0] = _dot(mn, wv_ref[...])


def _mem_kv(mem, w):
    batch, m_len, d_model = mem.shape
    out = jax.ShapeDtypeStruct((batch, m_len, d_model), F32)
    blk = pl.BlockSpec((1, m_len, d_model), lambda b: (b, 0, 0))
    return pl.pallas_call(
        _mem_kv_kernel,
        out_shape=(out, out),
        grid=(batch,),
        in_specs=[blk, _const_spec((1, d_model)), _const_spec((d_model, d_model)), _const_spec((d_model, d_model))],
        out_specs=(blk, blk),
        compiler_params=_cparams(1),
        name="mem_kv",
    )(mem, w["norm_mem_w"], w["w_mk"], w["w_mv"])


def _cross_router_kernel(h_ref, mk_ref, mv_ref, ncw_ref, wcq_ref, wco_ref, nfw_ref, wrt_ref, brt_ref,
                         h2_ref, xn_ref, lp_ref, gate_ref, cnt_ref,
                         o_scr, h2_prev, *, tile, head_dim, n_experts):
    @pl.when(pl.program_id(0) == 0)
    def _():
        h2_prev[...] = jnp.zeros(h2_prev.shape, F32)

    d_model = h_ref.shape[2]
    half = d_model // 2
    h1 = h_ref[0]
    hn = _rms(h1, ncw_ref[...]).astype(BF16)
    q = _dot(hn, wcq_ref[...]).astype(BF16)
    xn = _rms(h2_prev[...], nfw_ref[...]).astype(BF16)
    xn_ref[0] = xn
    for hd in range(MEM_HEADS):
        cols = slice(hd * head_dim, (hd + 1) * head_dim)
        s = _dot_nt(q[:, cols], mk_ref[0, :, cols]) * (head_dim ** -0.5)
        e = jnp.exp(s - jnp.max(s, axis=-1, keepdims=True))
        p = e / jnp.sum(e, axis=-1, keepdims=True)
        o_scr[:, cols] = _dot(p.astype(BF16), mv_ref[0, :, cols]).astype(BF16)

    work = _dot_nt(wrt_ref[...], xn) + brt_ref[...]
    h2_a = h1[:, 0:half] + _dot(o_scr[...], wco_ref[:, 0:half])

    e_iota = lax.broadcasted_iota(I32, (n_experts, tile), 0)
    vals, sels = [], []
    for _ in range(TOP_K):
        m = jnp.max(work, axis=0, keepdims=True)
        first = jnp.min(jnp.where(work == m, e_iota, n_experts), axis=0, keepdims=True)
        sel = e_iota == first
        vals.append(m)
        sels.append(sel)
        work = jnp.where(sel, -jnp.inf, work)
    exps = [jnp.exp(v - vals[0]) for v in vals]
    denom = exps[0] + exps[1] + exps[2] + exps[3]
    assigned = jnp.zeros((n_experts, tile), F32)
    for sel in sels:
        assigned = assigned + sel.astype(F32)
    r_iota = lax.broadcasted_iota(I32, (tile, tile), 0)
    c_iota = lax.broadcasted_iota(I32, (tile, tile), 1)
    prior = _dot(assigned.astype(BF16), (r_iota < c_iota).astype(BF16))
    counts = jnp.sum(assigned, axis=1, keepdims=True)
    run = jnp.floor((counts + (ROW_UNIT - 1)) * (1.0 / ROW_UNIT)) * ROW_UNIT
    er = lax.broadcasted_iota(I32, (n_experts, n_experts), 0)
    ec = lax.broadcasted_iota(I32, (n_experts, n_experts), 1)
    run_start = _dot((ec < er).astype(BF16), jnp.broadcast_to(run, (n_experts, LANES)).astype(BF16))[:, 0:1]

    h2_b = h1[:, half:d_model] + _dot(o_scr[...], wco_ref[:, half:d_model])
    h2 = jnp.concatenate([h2_a, h2_b], axis=-1)
    h2_ref[0] = h2
    h2_prev[...] = h2

    for k in range(TOP_K):
        lp_ref[0, k:k + 1, :] = jnp.sum(jnp.where(sels[k], run_start + prior, 0.0), axis=0, keepdims=True).astype(I32)
        gate_ref[0, k:k + 1, :] = exps[k] / denom
    cnt_ref[0, 0] = jnp.broadcast_to(counts, (n_experts, LANES)).astype(I32)


def _cross_router(h1, mk, mv, w):
    batch, length, d_model = h1.shape
    m_len = mk.shape[1]
    head_dim = d_model // MEM_HEADS
    n_experts = w["w_router_t"].shape[0]
    tile = min(CROSS_TILE, length)
    assert length % tile == 0
    n_t = length // tile
    n_tiles = batch * n_t
    kern = functools.partial(_cross_router_kernel, tile=tile, head_dim=head_dim, n_experts=n_experts)

    def attended(i):
        return jnp.minimum(i, n_tiles - 1)

    def routed(i):
        return jnp.maximum(i - 1, 0)

    tok = pl.BlockSpec((1, tile, d_model), lambda i: (attended(i) // n_t, attended(i) % n_t, 0))
    tok_routed = pl.BlockSpec((1, tile, d_model), lambda i: (routed(i) // n_t, routed(i) % n_t, 0))
    mem = pl.BlockSpec((1, m_len, d_model), lambda i: (attended(i) // n_t, 0, 0))
    small = pl.BlockSpec((1, TOP_K, tile), lambda i: (routed(i) // n_t, 0, routed(i) % n_t))
    return pl.pallas_call(
        kern,
        out_shape=(jax.ShapeDtypeStruct(h1.shape, F32),
                   jax.ShapeDtypeStruct(h1.shape, BF16),
                   jax.ShapeDtypeStruct((batch, TOP_K, length), I32),
                   jax.ShapeDtypeStruct((batch, TOP_K, length), F32),
                   jax.ShapeDtypeStruct((batch, n_t, n_experts, LANES), I32)),
        grid=(n_tiles + 1,),
        in_specs=[tok, mem, mem,
                  _const_spec((1, d_model)), _const_spec((d_model, d_model)), _const_spec((d_model, d_model)),
                  _const_spec((1, d_model)), _const_spec((n_experts, d_model)), _const_spec((n_experts, 1))],
        out_specs=(tok, tok_routed, small, small,
                   pl.BlockSpec((1, 1, n_experts, LANES), lambda i: (routed(i) // n_t, routed(i) % n_t, 0, 0))),
        scratch_shapes=[pltpu.VMEM((tile, d_model), BF16), pltpu.VMEM((tile, d_model), F32)],
        compiler_params=_cparams(1),
        name="cross_router",
    )(h1, mk, mv, w["norm_cross_w"], w["w_cq"], w["w_co"], w["norm_ffn_w"], w["w_router_t"], w["b_router_c"])


def _unit_copy(src_ref, dst_ref, src_unit, dst_unit, sem, n_units=1):
    def rows_of(unit):
        start = unit * ROW_UNIT
        return pl.ds(start if isinstance(start, int) else pl.multiple_of(start, ROW_UNIT), n_units * ROW_UNIT)

    return pltpu.make_async_copy(src_ref.at[rows_of(src_unit)], dst_ref.at[rows_of(dst_unit)], sem)


def _run_copies(lists_ref, local_ref, global_ref, sem, n_big_max, n_small_max, to_global):
    def make(local_at, global_at, n_units):
        def copy_of(j):
            loc, glo = lists_ref[0, 0, local_at + j], lists_ref[0, 0, global_at + j]
            if to_global:
                return _unit_copy(local_ref, global_ref, loc, glo, sem, n_units)
            return _unit_copy(global_ref, local_ref, glo, loc, sem, n_units)
        return copy_of

    return (make(0, n_big_max, BIG_UNITS),
            make(2 * n_big_max, 2 * n_big_max + n_small_max, 1))


def _for_units(n, body):
    def group(g, c):
        for lane in range(DMA_UNROLL):
            body(g * DMA_UNROLL + lane, lane)
        return c

    n_groups = lax.div(n, DMA_UNROLL)
    lax.fori_loop(0, n_groups, group, 0)

    def rest(j, c):
        body(j, 0)
        return c

    lax.fori_loop(n_groups * DMA_UNROLL, n, rest, 0)


def _start_units(n, copy_of):
    _for_units(n, lambda j, lane: copy_of(j).start(priority=lane % 2))


def _wait_units(n, copy_of):
    _for_units(n, lambda j, lane: copy_of(j).wait())


def _pack_pairs(a, b):
    ua = lax.bitcast_convert_type(a, U32)
    ub = lax.bitcast_convert_type(b, U32)
    return lax.shift_right_logical(ua, U32(16)) | (ub & U32(HIGH_HALF))


def _unpack_pairs(u):
    a = lax.bitcast_convert_type(lax.shift_left(u, U32(16)), F32)
    b = lax.bitcast_convert_type(u & U32(HIGH_HALF), F32)
    return a, b


def _dispatch_kernel(nbig_ref, nsmall_ref, tail_ref, ntail_ref, lp_ref, gate_ref, xn_ref, lists_ref, listsp_ref,
                     *refs, tile, n_local, d_model, n_big_max, n_small_max, first_group):
    xs_ref, rows, zero_buf, sems, tail_sem = refs if first_group else refs[1:]
    i = pl.program_id(0)
    n_i = pl.num_programs(0)
    slot = lax.rem(i, 2)

    def for_tile(step, lists, at_slot, action):
        big, small = _run_copies(lists, rows.at[at_slot], xs_ref, sems.at[at_slot], n_big_max, n_small_max, True)
        action(nbig_ref[step], big)
        action(nsmall_ref[step], small)

    if first_group:
        @pl.when(i == 0)
        def _():
            zero_buf[...] = jnp.zeros(zero_buf.shape, U32)

            def tail_copy(j):
                return _unit_copy(zero_buf, xs_ref, 0, tail_ref[j], tail_sem)

            def spare_copy(j):
                return _unit_copy(zero_buf, xs_ref, 0, ntail_ref[1] + j * FILL_UNITS, tail_sem, FILL_UNITS)

            _start_units(ntail_ref[0], tail_copy)
            _start_units(ntail_ref[2], spare_copy)
            _wait_units(ntail_ref[0], tail_copy)
            _wait_units(ntail_ref[2], spare_copy)

    half = d_model // 2
    xn = xn_ref[0]
    lps = [lp_ref[0, k:k + 1, :] for k in range(TOP_K)]
    gates = [gate_ref[0, k:k + 1, :] for k in range(TOP_K)]
    for c in range(n_local // DISPATCH_CHUNK):
        r_iota = lax.broadcasted_iota(I32, (DISPATCH_CHUNK, tile), 0) + c * DISPATCH_CHUNK
        gmat = jnp.full((DISPATCH_CHUNK, tile), -1.0, F32)
        for k in range(TOP_K):
            gmat = jnp.where(r_iota == lps[k], gates[k], gmat)
        perm = jnp.where(gmat >= 0.0, 1.0, 0.0).astype(BF16)
        wmat = jnp.maximum(gmat, 0.0)
        rs = pl.ds(c * DISPATCH_CHUNK, DISPATCH_CHUNK)
        xr = _dot(perm, xn)
        rows[slot, rs, 0:half] = _pack_pairs(xr[:, 0:half], xr[:, half:d_model])
        rows[slot, rs, half:half + LANES] = lax.bitcast_convert_type(
            jnp.broadcast_to(jnp.sum(wmat, axis=1, keepdims=True), (DISPATCH_CHUNK, LANES)), U32)

    for_tile(i, lists_ref, slot, _start_units)

    @pl.when(i > 0)
    def _():
        for_tile(i - 1, listsp_ref, 1 - slot, _wait_units)

    @pl.when(i == n_i - 1)
    def _():
        for_tile(i, lists_ref, slot, _wait_units)


def _dispatch(lp, gate, xn, layout, plan, xs_prev):
    batch, length, d_model = xn.shape
    tile, n_local = plan["tile"], plan["n_local"]
    n_big_max, n_small_max = plan["n_big_max"], plan["n_small_max"]
    n_list = 2 * (n_big_max + n_small_max)
    n_t = length // tile
    width = d_model // 2 + LANES
    first_group = xs_prev is None
    small = pl.BlockSpec((1, TOP_K, tile), lambda i, *_: (i // n_t, 0, i % n_t))
    n_prefetch = 4
    in_specs = [small, small,
                pl.BlockSpec((1, tile, d_model), lambda i, *_: (i // n_t, i % n_t, 0)),
                pl.BlockSpec((1, 1, n_list), lambda i, *_: (i, 0, 0), memory_space=pltpu.SMEM),
                pl.BlockSpec((1, 1, n_list), lambda i, *_: (jnp.maximum(i - 1, 0), 0, 0),
                             memory_space=pltpu.SMEM)]
    args = [plan["n_big"], plan["n_small"], layout["tail_units"], layout["n_tail"], lp, gate, xn,
            plan["copy_lists"], plan["copy_lists"]]
    aliases = {}
    if not first_group:
        aliases = {len(args): 0}
        in_specs.append(pl.BlockSpec(memory_space=pl.ANY))
        args.append(xs_prev)
    grid_spec = pltpu.PrefetchScalarGridSpec(
        num_scalar_prefetch=n_prefetch,
        grid=(batch * n_t,),
        in_specs=in_specs,
        out_specs=pl.BlockSpec(memory_space=pl.ANY),
        scratch_shapes=[pltpu.VMEM((2, n_local, width), U32), pltpu.VMEM((FILL_UNITS * ROW_UNIT, width), U32),
                        pltpu.SemaphoreType.DMA((2,)), pltpu.SemaphoreType.DMA(())],
    )
    return pl.pallas_call(
        functools.partial(_dispatch_kernel, tile=tile, n_local=n_local, d_model=d_model,
                          n_big_max=n_big_max, n_small_max=n_small_max, first_group=first_group),
        out_shape=jax.ShapeDtypeStruct((layout["n_rows"], width), U32),
        grid_spec=grid_spec,
        input_output_aliases=aliases,
        compiler_params=_cparams(1),
        name="dispatch",
    )(*args)


def _expert_kernel(be_ref, nb_ref, xs_ref, wgu_ref, bgu_ref, wd_ref, bd_ref, ys_ref, wgu_bf, wd_bf,
                   *, d_model, d_expert):
    b = pl.program_id(0)

    @pl.when(b >= nb_ref[0])
    def _():
        ys_ref[...] = jnp.zeros(ys_ref.shape, U32)

    @pl.when(b < nb_ref[0])
    def _():
        @pl.when((b == 0) | (be_ref[b] != be_ref[jnp.maximum(b - 1, 0)]))
        def _():
            wgu_bf[...] = wgu_ref[0].astype(BF16)
            wd_bf[...] = wd_ref[0].astype(BF16)

        half = d_model // 2
        xa, xb = _unpack_pairs(xs_ref[:, 0:half])
        row_gate = lax.bitcast_convert_type(xs_ref[:, half:half + 1], F32)
        gu = _dot(jnp.concatenate([xa, xb], axis=-1).astype(BF16), wgu_bf[...]) + bgu_ref[0]
        gl = jnp.minimum(gu[:, :d_expert], SWIGLU_LIMIT)
        up = jnp.clip(gu[:, d_expert:], -SWIGLU_LIMIT, SWIGLU_LIMIT)
        act = (gl * jax.nn.sigmoid(SWIGLU_ALPHA * gl) * (up + 1.0)).astype(BF16)
        y = ((_dot(act, wd_bf[...]) + bd_ref[0]) * row_gate).astype(BF16).astype(F32)
        ys_ref[...] = _pack_pairs(y[:, 0:half], y[:, half:d_model])


def _experts(xs, block_e, n_used, w, blk):
    n_rows, width = xs.shape
    d_model = 2 * (width - LANES)
    d_expert = w["w_down"].shape[1]

    def row_map(b, be, nb):
        return (jnp.minimum(b, nb[0] - 1), 0)

    def exp_map(b, be, nb):
        return (be[b], 0, 0)

    grid_spec = pltpu.PrefetchScalarGridSpec(
        num_scalar_prefetch=2,
        grid=(n_rows // blk,),
        in_specs=[pl.BlockSpec((blk, width), row_map),
                  pl.BlockSpec((1, d_model, 2 * d_expert), exp_map),
                  pl.BlockSpec((1, 1, 2 * d_expert), exp_map),
                  pl.BlockSpec((1, d_expert, d_model), exp_map),
                  pl.BlockSpec((1, 1, d_model), exp_map)],
        out_specs=pl.BlockSpec((blk, d_model // 2), lambda b, be, nb: (b, 0)),
        scratch_shapes=[pltpu.VMEM((d_model, 2 * d_expert), BF16), pltpu.VMEM((d_expert, d_model), BF16)],
    )
    return pl.pallas_call(
        functools.partial(_expert_kernel, d_model=d_model, d_expert=d_expert),
        out_shape=jax.ShapeDtypeStruct((n_rows, d_model // 2), U32),
        grid_spec=grid_spec,
        compiler_params=_cparams(1),
        name="experts",
    )(block_e, n_used, xs, w["w_gu"], w["b_gu"], w["w_down"], w["b_down"])


def _combine_kernel(nbig_ref, nsmall_ref, lpc_ref, h2_ref, nw_ref, lists_ref, listsn_ref, ys_ref, out_ref,
                    rows, sems, *, tile, n_local, final_norm, n_big_max, n_small_max):
    i = pl.program_id(0)
    n_i = pl.num_programs(0)
    slot = lax.rem(i, 2)

    def for_tile(step, lists, at_slot, action):
        big, small = _run_copies(lists, rows.at[at_slot], ys_ref, sems.at[at_slot], n_big_max, n_small_max, False)
        action(nbig_ref[step], big)
        action(nsmall_ref[step], small)

    @pl.when(i == 0)
    def _():
        rows[...] = jnp.zeros(rows.shape, U32)
        for_tile(0, lists_ref, 0, _start_units)

    @pl.when(i + 1 < n_i)
    def _():
        for_tile(i + 1, listsn_ref, 1 - slot, _start_units)

    for_tile(i, lists_ref, slot, _wait_units)

    lps = [lpc_ref[:, k:k + 1] for k in range(TOP_K)]
    half = out_ref.shape[1] // 2
    ya = jnp.zeros((tile, half), F32)
    yb = jnp.zeros((tile, half), F32)
    for c in range(n_local // COMBINE_CHUNK):
        c_iota = lax.broadcasted_iota(I32, (tile, COMBINE_CHUNK), 1) + c * COMBINE_CHUNK
        sel = jnp.zeros((tile, COMBINE_CHUNK), F32)
        for k in range(TOP_K):
            sel = jnp.where(c_iota == lps[k], 1.0, sel)
        sel = sel.astype(BF16)
        ra, rb = _unpack_pairs(rows[slot, pl.ds(c * COMBINE_CHUNK, COMBINE_CHUNK), :])
        ya = ya + _dot(sel, ra.astype(BF16))
        yb = yb + _dot(sel, rb.astype(BF16))
    h3 = h2_ref[...] + jnp.concatenate([ya, yb], axis=-1)
    out_ref[...] = _rms(h3, nw_ref[...]) if final_norm else h3


def _combine(lp_cols, h2, ys, norm_w, plan, final_norm):
    n_tok, d_model = h2.shape
    tile, n_local = plan["tile"], plan["n_local"]
    n_big_max, n_small_max = plan["n_big_max"], plan["n_small_max"]
    n_list = 2 * (n_big_max + n_small_max)
    n_tiles = n_tok // tile
    grid_spec = pltpu.PrefetchScalarGridSpec(
        num_scalar_prefetch=2,
        grid=(n_tiles,),
        in_specs=[pl.BlockSpec((tile, TOP_K), lambda i, *_: (i, 0)),
                  pl.BlockSpec((tile, d_model), lambda i, *_: (i, 0)),
                  _const_spec((1, d_model)),
                  pl.BlockSpec((1, 1, n_list), lambda i, *_: (i, 0, 0), memory_space=pltpu.SMEM),
                  pl.BlockSpec((1, 1, n_list), lambda i, *_: (jnp.minimum(i + 1, n_tiles - 1), 0, 0),
                               memory_space=pltpu.SMEM),
                  pl.BlockSpec(memory_space=pl.ANY)],
        out_specs=pl.BlockSpec((tile, d_model), lambda i, *_: (i, 0)),
        scratch_shapes=[pltpu.VMEM((2, n_local, d_model // 2), U32), pltpu.SemaphoreType.DMA((2,))],
    )
    return pl.pallas_call(
        functools.partial(_combine_kernel, tile=tile, n_local=n_local, final_norm=final_norm,
                          n_big_max=n_big_max, n_small_max=n_small_max),
        out_shape=jax.ShapeDtypeStruct((n_tok, d_model), F32),
        grid_spec=grid_spec,
        compiler_params=_cparams(1),
        name="combine",
    )(plan["n_big"], plan["n_small"], lp_cols, h2, norm_w, plan["copy_lists"], plan["copy_lists"], ys)


def _moe_plan(cnts, tiles):
    n_experts = cnts[0].shape[2]
    group_counts = [c[..., 0].reshape(-1, n_experts) for c in cnts]
    counts = jnp.concatenate(group_counts, axis=0)
    n_tiles = counts.shape[0]
    n_assigned = sum(g.shape[0] * t for g, t in zip(group_counts, tiles)) * TOP_K
    blk = EXPERT_BLOCK
    while blk > MIN_EXPERT_BLOCK and n_assigned < n_experts * blk:
        blk //= 2
    run_units = (counts + ROW_UNIT - 1) // ROW_UNIT
    expert_units = jnp.sum(run_units, axis=0)
    blk_units = blk // ROW_UNIT
    padded_units = (expert_units + blk_units - 1) // blk_units * blk_units
    expert_end = jnp.cumsum(padded_units)
    expert_start = expert_end - padded_units
    run_base = expert_start[None, :] + jnp.cumsum(run_units, axis=0) - run_units
    first_units = jnp.sum(run_units[:group_counts[0].shape[0]], axis=0)
    later_units_max = sum((g.shape[0] * t * TOP_K) // ROW_UNIT + g.shape[0] * n_experts
                          for g, t in zip(group_counts[1:], tiles[1:]))
    tj = jnp.arange(n_experts * (blk_units - 1) + later_units_max)
    tail_len = padded_units - first_units
    tail_end = jnp.cumsum(tail_len)
    tshift = expert_start + first_units - (tail_end - tail_len)
    tpast = tj[:, None] >= tail_end[None, :-1]
    tail_units = tj + tshift[0] + jnp.sum(jnp.where(tpast, (tshift[1:] - tshift[:-1])[None, :], 0), axis=-1)
    n_tail = tail_end[-1]
    tail_units = jnp.where(tj < n_tail, tail_units, 0)
    n_blocks = (n_assigned + n_tiles * n_experts * (ROW_UNIT - 1) + n_experts * (blk - 1)) // blk
    block_e = jnp.minimum(jnp.sum((jnp.arange(n_blocks) * blk_units)[:, None] >= expert_end[None, :], axis=-1),
                          n_experts - 1)
    layout = {
        "n_rows": n_blocks * blk, "blk": blk,
        "tail_units": tail_units.astype(I32),
        "n_tail": jnp.stack([n_tail, expert_end[-1],
                             (n_blocks * blk_units - expert_end[-1]) // FILL_UNITS]).astype(I32),
        "block_e": block_e.astype(I32), "n_used": (expert_end[-1] // blk_units).astype(I32).reshape(1),
    }
    plans, first = [], 0
    for g, tile in zip(group_counts, tiles):
        last = first + g.shape[0]
        plans.append(_group_plan(run_units[first:last], run_base[first:last], tile))
        first = last
    return layout, plans


def _group_plan(run_units, run_base, tile):
    n_tiles, n_experts = run_units.shape
    unit_end = jnp.cumsum(run_units, axis=1)
    n_local = _round_up(TOP_K * tile + (ROW_UNIT - 1) * n_experts, COMBINE_CHUNK)
    n_units = n_local // ROW_UNIT

    def expand(ends, shift, n, stride):
        i = jnp.arange(n)
        step = shift[:, 1:] - shift[:, :-1]
        past = i[None, :, None] >= ends[:, None, :-1]
        value = stride * i[None, :] + shift[:, 0:1] + jnp.sum(jnp.where(past, step[:, None, :], 0), axis=-1)
        return jnp.where(i[None, :] < ends[:, -1:], value, 0)

    run_first = unit_end - run_units
    big = run_units // BIG_UNITS
    small = run_units - BIG_UNITS * big
    big_end = jnp.cumsum(big, axis=1)
    small_end = jnp.cumsum(small, axis=1)
    big_first = big_end - big
    small_first = small_end - small
    n_big_max = n_units // BIG_UNITS
    n_small_max = (BIG_UNITS - 1) * n_experts
    copy_lists = jnp.concatenate([
        expand(big_end, run_first - BIG_UNITS * big_first, n_big_max, BIG_UNITS),
        expand(big_end, run_base - BIG_UNITS * big_first, n_big_max, BIG_UNITS),
        expand(small_end, run_first + BIG_UNITS * big - small_first, n_small_max, 1),
        expand(small_end, run_base + BIG_UNITS * big - small_first, n_small_max, 1)], axis=1)
    return {
        "tile": tile, "n_local": n_local, "n_big_max": n_big_max, "n_small_max": n_small_max,
        "n_big": big_end[:, -1].astype(I32), "n_small": small_end[:, -1].astype(I32),
        "copy_lists": copy_lists.astype(I32).reshape(n_tiles, 1, 2 * (n_big_max + n_small_max)),
    }


def _moe(groups, w, norm_w, final_norm):
    cnts = [g[4] for g in groups]
    tiles = [g[0].shape[1] // g[4].shape[1] for g in groups]
    layout, plans = _moe_plan(cnts, tiles)
    xs = None
    for (h2, xn, lp, gate, cnt), plan in zip(groups, plans):
        xs = _dispatch(lp, gate, xn, layout, plan, xs)
    ys = _experts(xs, layout["block_e"], layout["n_used"], w, layout["blk"])
    outs = []
    for (h2, xn, lp, gate, cnt), plan in zip(groups, plans):
        batch, length, d_model = h2.shape
        lp_cols = lp.transpose(0, 2, 1).reshape(batch * length, TOP_K)
        out = _combine(lp_cols, h2.reshape(batch * length, d_model), ys, norm_w, plan, final_norm)
        outs.append(out.reshape(batch, length, d_model))
    return outs


def _pre_moe(x, hist, s0, pos0, n_hist_valid, mk, mv, w):
    h1, new_hist, s_new = _mixer(x, hist, s0, pos0, n_hist_valid, w)
    return _cross_router(h1, mk.astype(BF16), mv.astype(BF16), w), new_hist, s_new


def kernel(x_prompt, x_sample, mem_prompt, state_pool, state_ret, cache_mem_k, cache_mem_v, norm_mix_w, w_in, pool_w_grp, pool_scale, w_pool_out, ret_gn_w, w_ret_out, w_o, norm_mem_w, w_mk, w_mv, norm_cross_w, w_cq, w_co, norm_ffn_w, w_router, b_router, w_gu, b_gu, w_down, b_down, norm_final_w):
    depth = w_in.shape[0]
    batch_p = x_prompt.shape[0]
    m_len = mem_prompt.shape[1]
    d_model = x_prompt.shape[-1]
    hp, hs = x_prompt, x_sample
    norm_w = norm_final_w[None, :]
    mem_k_p, mem_v_p, pool_p, ret_p, pool_s, ret_s = [], [], [], [], [], []
    for l in range(depth):
        w = {
            "norm_mix_w": norm_mix_w[l][None, :], "w_in": w_in[l].astype(BF16),
            "pool_w_grp": pool_w_grp[l].astype(BF16), "pool_scale": pool_scale[l][None, :],
            "w_pool_out": w_pool_out[l].astype(BF16), "ret_gn_w": ret_gn_w[l][None, :],
            "w_ret_out": w_ret_out[l].astype(BF16), "w_o": w_o[l].astype(BF16),
            "norm_mem_w": norm_mem_w[l][None, :], "w_mk": w_mk[l].astype(BF16), "w_mv": w_mv[l].astype(BF16),
            "norm_cross_w": norm_cross_w[l][None, :], "w_cq": w_cq[l].astype(BF16), "w_co": w_co[l].astype(BF16),
            "norm_ffn_w": norm_ffn_w[l][None, :], "w_router_t": w_router[l].T.astype(BF16),
            "b_router_c": b_router[l][:, None],
            "w_gu": w_gu[l], "b_gu": b_gu[l][:, None, :],
            "w_down": w_down[l], "b_down": b_down[l][:, None, :],
        }
        last = l == depth - 1
        mk, mv = _mem_kv(mem_prompt, w)
        zero_hist = jnp.zeros((batch_p, POOL_HIST, POOL_WIDTH), F32)
        zero_state = jnp.zeros((batch_p, RET_HEADS, RET_DK, RET_DV), F32)
        routed_p, hist_p, s_p = _pre_moe(hp, zero_hist, zero_state, 0, 0, mk, mv, w)
        mem_k_p.append(mk.reshape(batch_p, m_len, MEM_HEADS, d_model // MEM_HEADS))
        mem_v_p.append(mv.reshape(batch_p, m_len, MEM_HEADS, d_model // MEM_HEADS))
        pool_p.append(hist_p)
        ret_p.append(s_p)
        ck = cache_mem_k[l].reshape(cache_mem_k.shape[1], m_len, d_model)
        cv = cache_mem_v[l].reshape(cache_mem_v.shape[1], m_len, d_model)
        routed_s, hist_s, s_s = _pre_moe(hs, state_pool[l], state_ret[l], PAST_LEN, POOL_HIST, ck, cv, w)
        pool_s.append(hist_s)
        ret_s.append(s_s)
        hp, hs = _moe([routed_p, routed_s], w, norm_w, last)
    return (hp, hs, jnp.stack(mem_k_p), jnp.stack(mem_v_p), jnp.stack(pool_p), jnp.stack(ret_p),
            jnp.stack(pool_s), jnp.stack(ret_s))
```

```python
import functools

import jax
import jax.numpy as jnp
from jax import lax
from jax.experimental import pallas as pl
from jax.experimental.pallas import tpu as pltpu

F32 = jnp.float32
BF16 = jnp.bfloat16
I32 = jnp.int32
U32 = jnp.uint32
HIGH_HALF = 0xFFFF0000

EPS = 1e-6
PAST_LEN = 1024
POOL_WINDOWS = (2, 4, 8, 16)
POOL_GROUP_DIM = 128
POOL_WIDTH = POOL_GROUP_DIM * len(POOL_WINDOWS)
POOL_HIST = max(POOL_WINDOWS) - 1
RET_HEADS = 4
RET_DK = 128
RET_DV = 256
ROPE_BASE = 10000.0
MEM_HEADS = 4
TOP_K = 4
SWIGLU_LIMIT = 7.0
SWIGLU_ALPHA = 1.702

SUBLANES = 8
LANES = 128
VMEM_LIMIT_BYTES = 56 * 1024 * 1024

MIXER_TILE = 512
RET_CHUNK = 256
CROSS_TILE = 512
ROPE_TILE = 512
EXPERT_BLOCK = 1024
MIN_EXPERT_BLOCK = 128
ROW_UNIT = SUBLANES
DISPATCH_CHUNK = 128
COMBINE_CHUNK = 256
assert COMBINE_CHUNK % DISPATCH_CHUNK == 0
BIG_UNITS = 4
FILL_UNITS = MIN_EXPERT_BLOCK // ROW_UNIT
DMA_UNROLL = 4


def _cparams(n_axes):
    return pltpu.CompilerParams(dimension_semantics=("arbitrary",) * n_axes,
                                vmem_limit_bytes=VMEM_LIMIT_BYTES)


def _const_spec(shape):
    nd = len(shape)
    return pl.BlockSpec(shape, lambda *_: (0,) * nd, pipeline_mode=pl.Buffered(1))


def _round_up(n, m):
    return (n + m - 1) // m * m


def _rms(x32, w_row):
    ms = jnp.mean(x32 * x32, axis=-1, keepdims=True)
    return x32 * lax.rsqrt(ms + EPS) * w_row


def _dot(a, b):
    return jnp.dot(a, b, preferred_element_type=F32)


def _dot_nt(a, b):
    return lax.dot_general(a, b, (((1,), (1,)), ((), ())), preferred_element_type=F32)


def _dot_tn(a, b):
    return lax.dot_general(a, b, (((0,), (0,)), ((), ())), preferred_element_type=F32)


def _rope_kernel(inv_ref, sign_ref, cos_ref, sin_ref, *, pos0, tile):
    i = pl.program_id(0)
    pos = (lax.broadcasted_iota(I32, (tile, RET_DK), 0) + (pos0 + i * tile)).astype(F32)
    ang = pos * inv_ref[...]
    cos_ref[...] = jnp.cos(ang)
    sin_ref[...] = jnp.sin(ang) * sign_ref[...]


def _rope_tables(length, pos0):
    half = RET_DK // 2
    inv = 1.0 / (ROPE_BASE ** (jnp.arange(half, dtype=F32) / half))
    inv2 = jnp.concatenate([inv, inv])[None, :]
    sign = jnp.concatenate([-jnp.ones((half,), F32), jnp.ones((half,), F32)])[None, :]
    tile = min(length, ROPE_TILE)
    assert length % tile == 0
    return pl.pallas_call(
        functools.partial(_rope_kernel, pos0=pos0, tile=tile),
        out_shape=(jax.ShapeDtypeStruct((length, RET_DK), F32),) * 2,
        grid=(length // tile,),
        in_specs=[pl.BlockSpec((1, RET_DK), lambda i: (0, 0))] * 2,
        out_specs=(pl.BlockSpec((tile, RET_DK), lambda i: (i, 0)),) * 2,
        compiler_params=_cparams(1),
        name="rope_tables",
    )(inv2, sign)


def _mixer_kernel(x_ref, hist_ref, s0_ref, cos_ref, sin_ref, nmw_ref, win_ref, pwg_ref, pscale_ref,
                  wpo_ref, gnw_ref, wro_ref, wo_ref, dec_ref, qd_ref, kd_ref, gc_ref,
                  h_ref, hist_out_ref, s_out_ref,
                  ubuf, s_scr, z_scr, r_scr, yp_scr, *, tile, chunk, n_hist_valid, d_model):
    t = pl.program_id(1)
    n_t = pl.num_programs(1)
    q_off = POOL_WIDTH
    k_off = q_off + RET_HEADS * RET_DK
    v_off = k_off + RET_HEADS * RET_DK
    g_off = v_off + RET_HEADS * RET_DV
    ap_off = g_off + RET_HEADS * RET_DV
    ar_off = ap_off + d_model

    @pl.when(t == 0)
    def _():
        ubuf[0:1, :] = jnp.zeros((1, POOL_WIDTH), F32)
        ubuf[1:1 + POOL_HIST, :] = hist_ref[0]
        s_scr[...] = s0_ref[0]

    n_chunks = tile // chunk
    hist_rows = POOL_HIST + 1
    for c in range(n_chunks):
        rows = slice(c * chunk, (c + 1) * chunk)
        xn = _rms(x_ref[0, rows, :], nmw_ref[...]).astype(BF16)
        z_scr[rows, :] = _dot(xn, win_ref[...])
        ubuf[hist_rows + c * chunk:hist_rows + (c + 1) * chunk, :] = z_scr[rows, 0:POOL_WIDTH]

    for c in range(n_chunks):
        rows = slice(c * chunk, (c + 1) * chunk)
        full = ubuf[c * chunk:c * chunk + hist_rows + chunk, :]
        pos = t * tile + c * chunk + lax.broadcasted_iota(I32, (chunk, 1), 0)
        ys = []
        for g, w in enumerate(POOL_WINDOWS):
            f = full[:, g * POOL_GROUP_DIM:(g + 1) * POOL_GROUP_DIM]
            s = f
            shift = 1
            while shift < w:
                s = s + pltpu.roll(s, shift, 0)
                shift *= 2
            cnt = jnp.minimum(w, pos + 1 + n_hist_valid).astype(F32)
            d = s[hist_rows:, :] / cnt - f[hist_rows:, :]
            ys.append(_dot(d.astype(BF16), pwg_ref[g]))
        yp_scr[rows, :] = (jnp.concatenate(ys, axis=-1) * pscale_ref[...]).astype(BF16)
    ubuf[1:1 + POOL_HIST, :] = ubuf[tile + 1:tile + 1 + POOL_HIST, :]

    for c in range(n_chunks):
        rows = slice(c * chunk, (c + 1) * chunk)
        cosc = cos_ref[rows, :]
        sinc = sin_ref[rows, :]
        for h in range(RET_HEADS):
            qh = z_scr[rows, q_off + h * RET_DK:q_off + (h + 1) * RET_DK]
            kh = z_scr[rows, k_off + h * RET_DK:k_off + (h + 1) * RET_DK]
            vb = z_scr[rows, v_off + h * RET_DV:v_off + (h + 1) * RET_DV].astype(BF16)
            qr = qh * cosc + pltpu.roll(qh, RET_DK // 2, 1) * sinc
            kr = (kh * cosc + pltpu.roll(kh, RET_DK // 2, 1) * sinc) * (RET_DK ** -0.5)
            qb = qr.astype(BF16)
            kb = kr.astype(BF16)
            scores = _dot_nt(qb, kb) * dec_ref[h]
            state = s_scr[h]
            o = _dot(scores.astype(BF16), vb) + _dot(qb, state.astype(BF16)) * qd_ref[h]
            s_scr[h] = gc_ref[h] * state + _dot_tn((kr * kd_ref[h]).astype(BF16), vb)
            mu = jnp.mean(o, axis=-1, keepdims=True)
            dlt = o - mu
            var = jnp.mean(dlt * dlt, axis=-1, keepdims=True)
            on = dlt * lax.rsqrt(var + EPS) * gnw_ref[:, h * RET_DV:(h + 1) * RET_DV]
            gate = z_scr[rows, g_off + h * RET_DV:g_off + (h + 1) * RET_DV]
            r_scr[rows, h * RET_DV:(h + 1) * RET_DV] = (on * (gate * jax.nn.sigmoid(gate))).astype(BF16)

    for c in range(n_chunks):
        rows = slice(c * chunk, (c + 1) * chunk)
        branch_pool = _dot(yp_scr[rows, :], wpo_ref[...])
        branch_ret = _dot(r_scr[rows, :], wro_ref[...])
        merged = (jax.nn.sigmoid(z_scr[rows, ap_off:ap_off + d_model]) * branch_pool
                  + jax.nn.sigmoid(z_scr[rows, ar_off:ar_off + d_model]) * branch_ret)
        h_ref[0, rows, :] = x_ref[0, rows, :] + _dot(merged.astype(BF16), wo_ref[...])

    @pl.when(t == n_t - 1)
    def _():
        hist_out_ref[0] = ubuf[1:1 + POOL_HIST, :]
        s_out_ref[0] = s_scr[...]


def _ret_tables(chunk):
    lg = jnp.log(1.0 - 2.0 ** (-5.0 - jnp.arange(RET_HEADS, dtype=F32)))
    idx = jnp.arange(chunk)
    rel = idx[:, None] - idx[None, :]
    decay = jnp.where(rel[None] >= 0, jnp.exp(jnp.maximum(rel, 0)[None].astype(F32) * lg[:, None, None]), 0.0)
    q_decay = jnp.exp((idx + 1).astype(F32)[None, :] * lg[:, None])[:, :, None]
    k_decay = jnp.exp((chunk - 1 - idx).astype(F32)[None, :] * lg[:, None])[:, :, None]
    g_chunk = jnp.exp(chunk * lg)
    return decay, q_decay, k_decay, g_chunk


def _mixer(x, hist, s0, pos0, n_hist_valid, w):
    batch, length, d_model = x.shape
    tile = min(MIXER_TILE, length)
    chunk = min(RET_CHUNK, tile)
    assert length % tile == 0 and tile % chunk == 0 and tile >= POOL_HIST + 1
    in_width = w["w_in"].shape[1]
    cos, sin = _rope_tables(length, pos0)
    decay, q_decay, k_decay, g_chunk = _ret_tables(chunk)
    kern = functools.partial(_mixer_kernel, tile=tile, chunk=chunk, n_hist_valid=n_hist_valid, d_model=d_model)
    return pl.pallas_call(
        kern,
        out_shape=(jax.ShapeDtypeStruct(x.shape, F32),
                   jax.ShapeDtypeStruct(hist.shape, F32),
                   jax.ShapeDtypeStruct(s0.shape, F32)),
        grid=(batch, length // tile),
        in_specs=[
            pl.BlockSpec((1, tile, d_model), lambda b, t: (b, t, 0)),
            pl.BlockSpec((1, POOL_HIST, POOL_WIDTH), lambda b, t: (b, 0, 0)),
            pl.BlockSpec((1, RET_HEADS, RET_DK, RET_DV), lambda b, t: (b, 0, 0, 0)),
            pl.BlockSpec((tile, RET_DK), lambda b, t: (t, 0)),
            pl.BlockSpec((tile, RET_DK), lambda b, t: (t, 0)),
            _const_spec((1, d_model)),
            _const_spec((d_model, in_width)),
            _const_spec(w["pool_w_grp"].shape),
            _const_spec((1, POOL_WIDTH)),
            _const_spec((POOL_WIDTH, d_model)),
            _const_spec((1, RET_HEADS * RET_DV)),
            _const_spec((RET_HEADS * RET_DV, d_model)),
            _const_spec((d_model, d_model)),
            _const_spec(decay.shape),
            _const_spec(q_decay.shape),
            _const_spec(k_decay.shape),
            pl.BlockSpec(memory_space=pltpu.SMEM),
        ],
        out_specs=(
            pl.BlockSpec((1, tile, d_model), lambda b, t: (b, t, 0)),
            pl.BlockSpec((1, POOL_HIST, POOL_WIDTH), lambda b, t: (b, 0, 0)),
            pl.BlockSpec((1, RET_HEADS, RET_DK, RET_DV), lambda b, t: (b, 0, 0, 0)),
        ),
        scratch_shapes=[
            pltpu.VMEM((POOL_HIST + 1 + tile, POOL_WIDTH), F32),
            pltpu.VMEM((RET_HEADS, RET_DK, RET_DV), F32),
            pltpu.VMEM((tile, in_width), F32),
            pltpu.VMEM((tile, RET_HEADS * RET_DV), BF16),
            pltpu.VMEM((tile, POOL_WIDTH), BF16),
        ],
        compiler_params=_cparams(2),
        name="mixer",
    )(x, hist, s0, cos, sin, w["norm_mix_w"], w["w_in"], w["pool_w_grp"], w["pool_scale"], w["w_pool_out"],
      w["ret_gn_w"], w["w_ret_out"], w["w_o"], decay, q_decay, k_decay, g_chunk)


def _mem_kv_kernel(mem_ref, nw_ref, wk_ref, wv_ref, k_ref, v_ref):
    mn = _rms(mem_ref[0], nw_ref[...]).astype(BF16)
    k_ref[0] = _dot(mn, wk_ref[...])
    v_ref[0] = _dot(mn, wv_ref[...])


def _mem_kv(mem, w):
    batch, m_len, d_model = mem.shape
    out = jax.ShapeDtypeStruct((batch, m_len, d_model), F32)
    blk = pl.BlockSpec((1, m_len, d_model), lambda b: (b, 0, 0))
    return pl.pallas_call(
        _mem_kv_kernel,
        out_shape=(out, out),
        grid=(batch,),
        in_specs=[blk, _const_spec((1, d_model)), _const_spec((d_model, d_model)), _const_spec((d_model, d_model))],
        out_specs=(blk, blk),
        compiler_params=_cparams(1),
        name="mem_kv",
    )(mem, w["norm_mem_w"], w["w_mk"], w["w_mv"])


def _cross_router_kernel(h_ref, mk_ref, mv_ref, ncw_ref, wcq_ref, wco_ref, nfw_ref, wrt_ref, brt_ref,
                         h2_ref, xn_ref, lp_ref, gate_ref, cnt_ref,
                         o_scr, h2_prev, *, tile, head_dim, n_experts):
    @pl.when(pl.program_id(0) == 0)
    def _():
        h2_prev[...] = jnp.zeros(h2_prev.shape, F32)

    d_model = h_ref.shape[2]
    half = d_model // 2
    h1 = h_ref[0]
    hn = _rms(h1, ncw_ref[...]).astype(BF16)
    q = _dot(hn, wcq_ref[...]).astype(BF16)
    xn = _rms(h2_prev[...], nfw_ref[...]).astype(BF16)
    xn_ref[0] = xn
    n_seq = mk_ref.shape[0]
    seq_len = tile // n_seq
    for hd in range(MEM_HEADS):
        cols = slice(hd * head_dim, (hd + 1) * head_dim)
        scores = [_dot_nt(q[j * seq_len:(j + 1) * seq_len, cols], mk_ref[j, :, cols]) * (head_dim ** -0.5)
                  for j in range(n_seq)]
        for j, s in enumerate(scores):
            e = jnp.exp(s - jnp.max(s, axis=-1, keepdims=True))
            p = e / jnp.sum(e, axis=-1, keepdims=True)
            o_scr[j * seq_len:(j + 1) * seq_len, cols] = _dot(p.astype(BF16), mv_ref[j, :, cols]).astype(BF16)

    work = _dot_nt(wrt_ref[...], xn) + brt_ref[...]
    h2_a = h1[:, 0:half] + _dot(o_scr[...], wco_ref[:, 0:half])

    e_iota = lax.broadcasted_iota(I32, (n_experts, tile), 0)
    vals, sels = [], []
    for _ in range(TOP_K):
        m = jnp.max(work, axis=0, keepdims=True)
        first = jnp.min(jnp.where(work == m, e_iota, n_experts), axis=0, keepdims=True)
        sel = e_iota == first
        vals.append(m)
        sels.append(sel)
        work = jnp.where(sel, -jnp.inf, work)
    exps = [jnp.exp(v - vals[0]) for v in vals]
    denom = exps[0] + exps[1] + exps[2] + exps[3]
    assigned = jnp.zeros((n_experts, tile), F32)
    for sel in sels:
        assigned = assigned + sel.astype(F32)
    r_iota = lax.broadcasted_iota(I32, (tile, tile), 0)
    c_iota = lax.broadcasted_iota(I32, (tile, tile), 1)
    prior = _dot(assigned.astype(BF16), (r_iota < c_iota).astype(BF16))
    counts = jnp.sum(assigned, axis=1, keepdims=True)
    run = jnp.floor((counts + (ROW_UNIT - 1)) * (1.0 / ROW_UNIT)) * ROW_UNIT
    er = lax.broadcasted_iota(I32, (n_experts, n_experts), 0)
    ec = lax.broadcasted_iota(I32, (n_experts, n_experts), 1)
    run_start = _dot((ec < er).astype(BF16), jnp.broadcast_to(run, (n_experts, LANES)).astype(BF16))[:, 0:1]

    h2_b = h1[:, half:d_model] + _dot(o_scr[...], wco_ref[:, half:d_model])
    h2 = jnp.concatenate([h2_a, h2_b], axis=-1)
    h2_ref[0] = h2
    h2_prev[...] = h2

    for k in range(TOP_K):
        lp_ref[0, k:k + 1, :] = jnp.sum(jnp.where(sels[k], run_start + prior, 0.0), axis=0, keepdims=True).astype(I32)
        gate_ref[0, k:k + 1, :] = exps[k] / denom
    cnt_ref[0, 0] = jnp.broadcast_to(counts, (n_experts, LANES)).astype(I32)


def _cross_router(h1, mk, mv, w):
    n_seq = 1
    if h1.shape[1] < CROSS_TILE:
        n_seq = max(1, min(h1.shape[0], CROSS_TILE // h1.shape[1]))
        while h1.shape[0] % n_seq:
            n_seq -= 1
        h1 = h1.reshape(h1.shape[0] // n_seq, n_seq * h1.shape[1], h1.shape[2])
    batch, length, d_model = h1.shape
    m_len = mk.shape[1]
    head_dim = d_model // MEM_HEADS
    n_experts = w["w_router_t"].shape[0]
    tile = min(CROSS_TILE, length)
    assert length % tile == 0
    n_t = length // tile
    n_tiles = batch * n_t
    kern = functools.partial(_cross_router_kernel, tile=tile, head_dim=head_dim, n_experts=n_experts)

    def attended(i):
        return jnp.minimum(i, n_tiles - 1)

    def routed(i):
        return jnp.maximum(i - 1, 0)

    tok = pl.BlockSpec((1, tile, d_model), lambda i: (attended(i) // n_t, attended(i) % n_t, 0))
    tok_routed = pl.BlockSpec((1, tile, d_model), lambda i: (routed(i) // n_t, routed(i) % n_t, 0))
    mem = pl.BlockSpec((n_seq, m_len, d_model), lambda i: (attended(i) // n_t, 0, 0))
    small = pl.BlockSpec((1, TOP_K, tile), lambda i: (routed(i) // n_t, 0, routed(i) % n_t))
    return pl.pallas_call(
        kern,
        out_shape=(jax.ShapeDtypeStruct(h1.shape, F32),
                   jax.ShapeDtypeStruct(h1.shape, BF16),
                   jax.ShapeDtypeStruct((batch, TOP_K, length), I32),
                   jax.ShapeDtypeStruct((batch, TOP_K, length), F32),
                   jax.ShapeDtypeStruct((batch, n_t, n_experts, LANES), I32)),
        grid=(n_tiles + 1,),
        in_specs=[tok, mem, mem,
                  _const_spec((1, d_model)), _const_spec((d_model, d_model)), _const_spec((d_model, d_model)),
                  _const_spec((1, d_model)), _const_spec((n_experts, d_model)), _const_spec((n_experts, 1))],
        out_specs=(tok, tok_routed, small, small,
                   pl.BlockSpec((1, 1, n_experts, LANES), lambda i: (routed(i) // n_t, routed(i) % n_t, 0, 0))),
        scratch_shapes=[pltpu.VMEM((tile, d_model), BF16), pltpu.VMEM((tile, d_model), F32)],
        compiler_params=_cparams(1),
        name="cross_router",
    )(h1, mk, mv, w["norm_cross_w"], w["w_cq"], w["w_co"], w["norm_ffn_w"], w["w_router_t"], w["b_router_c"])


def _unit_copy(src_ref, dst_ref, src_unit, dst_unit, sem, n_units=1):
    def rows_of(unit):
        start = unit * ROW_UNIT
        return pl.ds(start if isinstance(start, int) else pl.multiple_of(start, ROW_UNIT), n_units * ROW_UNIT)

    return pltpu.make_async_copy(src_ref.at[rows_of(src_unit)], dst_ref.at[rows_of(dst_unit)], sem)


def _run_copies(lists_ref, local_ref, global_ref, sem, n_big_max, n_small_max, to_global):
    def make(local_at, global_at, n_units):
        def copy_of(j):
            loc, glo = lists_ref[0, 0, local_at + j], lists_ref[0, 0, global_at + j]
            if to_global:
                return _unit_copy(local_ref, global_ref, loc, glo, sem, n_units)
            return _unit_copy(global_ref, local_ref, glo, loc, sem, n_units)
        return copy_of

    return (make(0, n_big_max, BIG_UNITS),
            make(2 * n_big_max, 2 * n_big_max + n_small_max, 1))


def _for_units(n, body):
    def group(g, c):
        for lane in range(DMA_UNROLL):
            body(g * DMA_UNROLL + lane, lane)
        return c

    n_groups = lax.div(n, DMA_UNROLL)
    lax.fori_loop(0, n_groups, group, 0)

    def rest(j, c):
        body(j, 0)
        return c

    lax.fori_loop(n_groups * DMA_UNROLL, n, rest, 0)


def _start_units(n, copy_of):
    _for_units(n, lambda j, lane: copy_of(j).start(priority=lane % 2))


def _wait_units(n, copy_of):
    _for_units(n, lambda j, lane: copy_of(j).wait())


def _pack_pairs(a, b):
    ua = lax.bitcast_convert_type(a, U32)
    ub = lax.bitcast_convert_type(b, U32)
    return lax.shift_right_logical(ua, U32(16)) | (ub & U32(HIGH_HALF))


def _unpack_pairs(u):
    a = lax.bitcast_convert_type(lax.shift_left(u, U32(16)), F32)
    b = lax.bitcast_convert_type(u & U32(HIGH_HALF), F32)
    return a, b


def _dispatch_kernel(nbig_ref, nsmall_ref, tail_ref, ntail_ref, lp_ref, gate_ref, xn_ref, lists_ref, listsp_ref,
                     *refs, tile, n_local, d_model, n_big_max, n_small_max, first_group):
    xs_ref, rows, zero_buf, sems, tail_sem = refs if first_group else refs[1:]
    i = pl.program_id(0)
    n_i = pl.num_programs(0)
    slot = lax.rem(i, 2)

    def for_tile(step, lists, at_slot, action):
        big, small = _run_copies(lists, rows.at[at_slot], xs_ref, sems.at[at_slot], n_big_max, n_small_max, True)
        action(nbig_ref[step], big)
        action(nsmall_ref[step], small)

    if first_group:
        @pl.when(i == 0)
        def _():
            zero_buf[...] = jnp.zeros(zero_buf.shape, U32)

            def tail_copy(j):
                return _unit_copy(zero_buf, xs_ref, 0, tail_ref[j], tail_sem)

            def spare_copy(j):
                return _unit_copy(zero_buf, xs_ref, 0, ntail_ref[1] + j * FILL_UNITS, tail_sem, FILL_UNITS)

            _start_units(ntail_ref[0], tail_copy)
            _start_units(ntail_ref[2], spare_copy)
            _wait_units(ntail_ref[0], tail_copy)
            _wait_units(ntail_ref[2], spare_copy)

    half = d_model // 2
    xn = xn_ref[0]
    lps = [lp_ref[0, k:k + 1, :] for k in range(TOP_K)]
    gates = [gate_ref[0, k:k + 1, :] for k in range(TOP_K)]
    for c in range(n_local // DISPATCH_CHUNK):
        r_iota = lax.broadcasted_iota(I32, (DISPATCH_CHUNK, tile), 0) + c * DISPATCH_CHUNK
        gmat = jnp.full((DISPATCH_CHUNK, tile), -1.0, F32)
        for k in range(TOP_K):
            gmat = jnp.where(r_iota == lps[k], gates[k], gmat)
        perm = jnp.where(gmat >= 0.0, 1.0, 0.0).astype(BF16)
        wmat = jnp.maximum(gmat, 0.0)
        rs = pl.ds(c * DISPATCH_CHUNK, DISPATCH_CHUNK)
        xr = _dot(perm, xn)
        rows[slot, rs, 0:half] = _pack_pairs(xr[:, 0:half], xr[:, half:d_model])
        rows[slot, rs, half:half + LANES] = lax.bitcast_convert_type(
            jnp.broadcast_to(jnp.sum(wmat, axis=1, keepdims=True), (DISPATCH_CHUNK, LANES)), U32)

    for_tile(i, lists_ref, slot, _start_units)

    @pl.when(i > 0)
    def _():
        for_tile(i - 1, listsp_ref, 1 - slot, _wait_units)

    @pl.when(i == n_i - 1)
    def _():
        for_tile(i, lists_ref, slot, _wait_units)


def _dispatch(lp, gate, xn, layout, plan, xs_prev):
    batch, length, d_model = xn.shape
    tile, n_local = plan["tile"], plan["n_local"]
    n_big_max, n_small_max = plan["n_big_max"], plan["n_small_max"]
    n_list = 2 * (n_big_max + n_small_max)
    n_t = length // tile
    width = d_model // 2 + LANES
    first_group = xs_prev is None
    small = pl.BlockSpec((1, TOP_K, tile), lambda i, *_: (i // n_t, 0, i % n_t))
    n_prefetch = 4
    in_specs = [small, small,
                pl.BlockSpec((1, tile, d_model), lambda i, *_: (i // n_t, i % n_t, 0)),
                pl.BlockSpec((1, 1, n_list), lambda i, *_: (i, 0, 0), memory_space=pltpu.SMEM),
                pl.BlockSpec((1, 1, n_list), lambda i, *_: (jnp.maximum(i - 1, 0), 0, 0),
                             memory_space=pltpu.SMEM)]
    args = [plan["n_big"], plan["n_small"], layout["tail_units"], layout["n_tail"], lp, gate, xn,
            plan["copy_lists"], plan["copy_lists"]]
    aliases = {}
    if not first_group:
        aliases = {len(args): 0}
        in_specs.append(pl.BlockSpec(memory_space=pl.ANY))
        args.append(xs_prev)
    grid_spec = pltpu.PrefetchScalarGridSpec(
        num_scalar_prefetch=n_prefetch,
        grid=(batch * n_t,),
        in_specs=in_specs,
        out_specs=pl.BlockSpec(memory_space=pl.ANY),
        scratch_shapes=[pltpu.VMEM((2, n_local, width), U32), pltpu.VMEM((FILL_UNITS * ROW_UNIT, width), U32),
                        pltpu.SemaphoreType.DMA((2,)), pltpu.SemaphoreType.DMA(())],
    )
    return pl.pallas_call(
        functools.partial(_dispatch_kernel, tile=tile, n_local=n_local, d_model=d_model,
                          n_big_max=n_big_max, n_small_max=n_small_max, first_group=first_group),
        out_shape=jax.ShapeDtypeStruct((layout["n_rows"], width), U32),
        grid_spec=grid_spec,
        input_output_aliases=aliases,
        compiler_params=_cparams(1),
        name="dispatch",
    )(*args)


def _expert_kernel(be_ref, nb_ref, xs_ref, wgu_ref, bgu_ref, wd_ref, bd_ref, ys_ref, wgu_bf, wd_bf,
                   *, d_model, d_expert):
    b = pl.program_id(0)

    @pl.when(b >= nb_ref[0])
    def _():
        ys_ref[...] = jnp.zeros(ys_ref.shape, U32)

    @pl.when(b < nb_ref[0])
    def _():
        @pl.when((b == 0) | (be_ref[b] != be_ref[jnp.maximum(b - 1, 0)]))
        def _():
            wgu_bf[...] = wgu_ref[0].astype(BF16)
            wd_bf[...] = wd_ref[0].astype(BF16)

        half = d_model // 2
        xa, xb = _unpack_pairs(xs_ref[:, 0:half])
        row_gate = lax.bitcast_convert_type(xs_ref[:, half:half + 1], F32)
        gu = _dot(jnp.concatenate([xa, xb], axis=-1).astype(BF16), wgu_bf[...]) + bgu_ref[0]
        gl = jnp.minimum(gu[:, :d_expert], SWIGLU_LIMIT)
        up = jnp.clip(gu[:, d_expert:], -SWIGLU_LIMIT, SWIGLU_LIMIT)
        act = (gl * jax.nn.sigmoid(SWIGLU_ALPHA * gl) * (up + 1.0)).astype(BF16)
        y = ((_dot(act, wd_bf[...]) + bd_ref[0]) * row_gate).astype(BF16).astype(F32)
        ys_ref[...] = _pack_pairs(y[:, 0:half], y[:, half:d_model])


def _experts(xs, block_e, n_used, w, blk):
    n_rows, width = xs.shape
    d_model = 2 * (width - LANES)
    d_expert = w["w_down"].shape[1]

    def row_map(b, be, nb):
        return (jnp.minimum(b, nb[0] - 1), 0)

    def exp_map(b, be, nb):
        return (be[b], 0, 0)

    grid_spec = pltpu.PrefetchScalarGridSpec(
        num_scalar_prefetch=2,
        grid=(n_rows // blk,),
        in_specs=[pl.BlockSpec((blk, width), row_map),
                  pl.BlockSpec((1, d_model, 2 * d_expert), exp_map),
                  pl.BlockSpec((1, 1, 2 * d_expert), exp_map),
                  pl.BlockSpec((1, d_expert, d_model), exp_map),
                  pl.BlockSpec((1, 1, d_model), exp_map)],
        out_specs=pl.BlockSpec((blk, d_model // 2), lambda b, be, nb: (b, 0)),
        scratch_shapes=[pltpu.VMEM((d_model, 2 * d_expert), BF16), pltpu.VMEM((d_expert, d_model), BF16)],
    )
    return pl.pallas_call(
        functools.partial(_expert_kernel, d_model=d_model, d_expert=d_expert),
        out_shape=jax.ShapeDtypeStruct((n_rows, d_model // 2), U32),
        grid_spec=grid_spec,
        compiler_params=_cparams(1),
        name="experts",
    )(block_e, n_used, xs, w["w_gu"], w["b_gu"], w["w_down"], w["b_down"])


def _combine_kernel(nbig_ref, nsmall_ref, lpc_ref, h2_ref, nw_ref, lists_ref, listsn_ref, ys_ref, out_ref,
                    rows, sems, *, tile, n_local, final_norm, n_big_max, n_small_max):
    i = pl.program_id(0)
    n_i = pl.num_programs(0)
    slot = lax.rem(i, 2)

    def for_tile(step, lists, at_slot, action):
        big, small = _run_copies(lists, rows.at[at_slot], ys_ref, sems.at[at_slot], n_big_max, n_small_max, False)
        action(nbig_ref[step], big)
        action(nsmall_ref[step], small)

    @pl.when(i == 0)
    def _():
        rows[...] = jnp.zeros(rows.shape, U32)
        for_tile(0, lists_ref, 0, _start_units)

    @pl.when(i + 1 < n_i)
    def _():
        for_tile(i + 1, listsn_ref, 1 - slot, _start_units)

    for_tile(i, lists_ref, slot, _wait_units)

    lps = [lpc_ref[:, k:k + 1] for k in range(TOP_K)]
    half = out_ref.shape[1] // 2
    ya = jnp.zeros((tile, half), F32)
    yb = jnp.zeros((tile, half), F32)
    for c in range(n_local // COMBINE_CHUNK):
        c_iota = lax.broadcasted_iota(I32, (tile, COMBINE_CHUNK), 1) + c * COMBINE_CHUNK
        sel = jnp.zeros((tile, COMBINE_CHUNK), F32)
        for k in range(TOP_K):
            sel = jnp.where(c_iota == lps[k], 1.0, sel)
        sel = sel.astype(BF16)
        ra, rb = _unpack_pairs(rows[slot, pl.ds(c * COMBINE_CHUNK, COMBINE_CHUNK), :])
        ya = ya + _dot(sel, ra.astype(BF16))
        yb = yb + _dot(sel, rb.astype(BF16))
    h3 = h2_ref[...] + jnp.concatenate([ya, yb], axis=-1)
    out_ref[...] = _rms(h3, nw_ref[...]) if final_norm else h3


def _combine(lp_cols, h2, ys, norm_w, plan, final_norm):
    n_tok, d_model = h2.shape
    tile, n_local = plan["tile"], plan["n_local"]
    n_big_max, n_small_max = plan["n_big_max"], plan["n_small_max"]
    n_list = 2 * (n_big_max + n_small_max)
    n_tiles = n_tok // tile
    grid_spec = pltpu.PrefetchScalarGridSpec(
        num_scalar_prefetch=2,
        grid=(n_tiles,),
        in_specs=[pl.BlockSpec((tile, TOP_K), lambda i, *_: (i, 0)),
                  pl.BlockSpec((tile, d_model), lambda i, *_: (i, 0)),
                  _const_spec((1, d_model)),
                  pl.BlockSpec((1, 1, n_list), lambda i, *_: (i, 0, 0), memory_space=pltpu.SMEM),
                  pl.BlockSpec((1, 1, n_list), lambda i, *_: (jnp.minimum(i + 1, n_tiles - 1), 0, 0),
                               memory_space=pltpu.SMEM),
                  pl.BlockSpec(memory_space=pl.ANY)],
        out_specs=pl.BlockSpec((tile, d_model), lambda i, *_: (i, 0)),
        scratch_shapes=[pltpu.VMEM((2, n_local, d_model // 2), U32), pltpu.SemaphoreType.DMA((2,))],
    )
    return pl.pallas_call(
        functools.partial(_combine_kernel, tile=tile, n_local=n_local, final_norm=final_norm,
                          n_big_max=n_big_max, n_small_max=n_small_max),
        out_shape=jax.ShapeDtypeStruct((n_tok, d_model), F32),
        grid_spec=grid_spec,
        compiler_params=_cparams(1),
        name="combine",
    )(plan["n_big"], plan["n_small"], lp_cols, h2, norm_w, plan["copy_lists"], plan["copy_lists"], ys)


def _moe_plan(cnts, tiles):
    n_experts = cnts[0].shape[2]
    group_counts = [c[..., 0].reshape(-1, n_experts) for c in cnts]
    counts = jnp.concatenate(group_counts, axis=0)
    n_tiles = counts.shape[0]
    n_assigned = sum(g.shape[0] * t for g, t in zip(group_counts, tiles)) * TOP_K
    blk = EXPERT_BLOCK
    while blk > MIN_EXPERT_BLOCK and n_assigned < n_experts * blk:
        blk //= 2
    run_units = (counts + ROW_UNIT - 1) // ROW_UNIT
    expert_units = jnp.sum(run_units, axis=0)
    blk_units = blk // ROW_UNIT
    padded_units = (expert_units + blk_units - 1) // blk_units * blk_units
    expert_end = jnp.cumsum(padded_units)
    expert_start = expert_end - padded_units
    run_base = expert_start[None, :] + jnp.cumsum(run_units, axis=0) - run_units
    first_units = jnp.sum(run_units[:group_counts[0].shape[0]], axis=0)
    later_units_max = sum((g.shape[0] * t * TOP_K) // ROW_UNIT + g.shape[0] * n_experts
                          for g, t in zip(group_counts[1:], tiles[1:]))
    tj = jnp.arange(n_experts * (blk_units - 1) + later_units_max)
    tail_len = padded_units - first_units
    tail_end = jnp.cumsum(tail_len)
    tshift = expert_start + first_units - (tail_end - tail_len)
    tpast = tj[:, None] >= tail_end[None, :-1]
    tail_units = tj + tshift[0] + jnp.sum(jnp.where(tpast, (tshift[1:] - tshift[:-1])[None, :], 0), axis=-1)
    n_tail = tail_end[-1]
    tail_units = jnp.where(tj < n_tail, tail_units, 0)
    n_blocks = (n_assigned + n_tiles * n_experts * (ROW_UNIT - 1) + n_experts * (blk - 1)) // blk
    block_e = jnp.minimum(jnp.sum((jnp.arange(n_blocks) * blk_units)[:, None] >= expert_end[None, :], axis=-1),
                          n_experts - 1)
    layout = {
        "n_rows": n_blocks * blk, "blk": blk,
        "tail_units": tail_units.astype(I32),
        "n_tail": jnp.stack([n_tail, expert_end[-1],
                             (n_blocks * blk_units - expert_end[-1]) // FILL_UNITS]).astype(I32),
        "block_e": block_e.astype(I32), "n_used": (expert_end[-1] // blk_units).astype(I32).reshape(1),
    }
    plans, first = [], 0
    for g, tile in zip(group_counts, tiles):
        last = first + g.shape[0]
        plans.append(_group_plan(run_units[first:last], run_base[first:last], tile))
        first = last
    return layout, plans


def _group_plan(run_units, run_base, tile):
    n_tiles, n_experts = run_units.shape
    unit_end = jnp.cumsum(run_units, axis=1)
    n_local = _round_up(TOP_K * tile + (ROW_UNIT - 1) * n_experts, COMBINE_CHUNK)
    n_units = n_local // ROW_UNIT

    def expand(ends, shift, n, stride):
        i = jnp.arange(n)
        step = shift[:, 1:] - shift[:, :-1]
        past = i[None, :, None] >= ends[:, None, :-1]
        value = stride * i[None, :] + shift[:, 0:1] + jnp.sum(jnp.where(past, step[:, None, :], 0), axis=-1)
        return jnp.where(i[None, :] < ends[:, -1:], value, 0)

    run_first = unit_end - run_units
    big = run_units // BIG_UNITS
    small = run_units - BIG_UNITS * big
    big_end = jnp.cumsum(big, axis=1)
    small_end = jnp.cumsum(small, axis=1)
    big_first = big_end - big
    small_first = small_end - small
    n_big_max = n_units // BIG_UNITS
    n_small_max = (BIG_UNITS - 1) * n_experts
    copy_lists = jnp.concatenate([
        expand(big_end, run_first - BIG_UNITS * big_first, n_big_max, BIG_UNITS),
        expand(big_end, run_base - BIG_UNITS * big_first, n_big_max, BIG_UNITS),
        expand(small_end, run_first + BIG_UNITS * big - small_first, n_small_max, 1),
        expand(small_end, run_base + BIG_UNITS * big - small_first, n_small_max, 1)], axis=1)
    return {
        "tile": tile, "n_local": n_local, "n_big_max": n_big_max, "n_small_max": n_small_max,
        "n_big": big_end[:, -1].astype(I32), "n_small": small_end[:, -1].astype(I32),
        "copy_lists": copy_lists.astype(I32).reshape(n_tiles, 1, 2 * (n_big_max + n_small_max)),
    }


def _moe(groups, w, norm_w, final_norm):
    cnts = [g[4] for g in groups]
    tiles = [g[0].shape[1] // g[4].shape[1] for g in groups]
    layout, plans = _moe_plan(cnts, tiles)
    xs = None
    for (h2, xn, lp, gate, cnt), plan in zip(groups, plans):
        xs = _dispatch(lp, gate, xn, layout, plan, xs)
    ys = _experts(xs, layout["block_e"], layout["n_used"], w, layout["blk"])
    outs = []
    for (h2, xn, lp, gate, cnt), plan in zip(groups, plans):
        batch, length, d_model = h2.shape
        lp_cols = lp.transpose(0, 2, 1).reshape(batch * length, TOP_K)
        out = _combine(lp_cols, h2.reshape(batch * length, d_model), ys, norm_w, plan, final_norm)
        outs.append(out.reshape(batch, length, d_model))
    return outs


def _pre_moe(x, hist, s0, pos0, n_hist_valid, mk, mv, w):
    h1, new_hist, s_new = _mixer(x, hist, s0, pos0, n_hist_valid, w)
    return _cross_router(h1, mk.astype(BF16), mv.astype(BF16), w), new_hist, s_new


def kernel(x_prompt, x_sample, mem_prompt, state_pool, state_ret, cache_mem_k, cache_mem_v, norm_mix_w, w_in, pool_w_grp, pool_scale, w_pool_out, ret_gn_w, w_ret_out, w_o, norm_mem_w, w_mk, w_mv, norm_cross_w, w_cq, w_co, norm_ffn_w, w_router, b_router, w_gu, b_gu, w_down, b_down, norm_final_w):
    depth = w_in.shape[0]
    batch_p = x_prompt.shape[0]
    m_len = mem_prompt.shape[1]
    d_model = x_prompt.shape[-1]
    hp, hs = x_prompt, x_sample
    norm_w = norm_final_w[None, :]
    mem_k_p, mem_v_p, pool_p, ret_p, pool_s, ret_s = [], [], [], [], [], []
    for l in range(depth):
        w = {
            "norm_mix_w": norm_mix_w[l][None, :], "w_in": w_in[l].astype(BF16),
            "pool_w_grp": pool_w_grp[l].astype(BF16), "pool_scale": pool_scale[l][None, :],
            "w_pool_out": w_pool_out[l].astype(BF16), "ret_gn_w": ret_gn_w[l][None, :],
            "w_ret_out": w_ret_out[l].astype(BF16), "w_o": w_o[l].astype(BF16),
            "norm_mem_w": norm_mem_w[l][None, :], "w_mk": w_mk[l].astype(BF16), "w_mv": w_mv[l].astype(BF16),
            "norm_cross_w": norm_cross_w[l][None, :], "w_cq": w_cq[l].astype(BF16), "w_co": w_co[l].astype(BF16),
            "norm_ffn_w": norm_ffn_w[l][None, :], "w_router_t": w_router[l].T.astype(BF16),
            "b_router_c": b_router[l][:, None],
            "w_gu": w_gu[l], "b_gu": b_gu[l][:, None, :],
            "w_down": w_down[l], "b_down": b_down[l][:, None, :],
        }
        last = l == depth - 1
        mk, mv = _mem_kv(mem_prompt, w)
        zero_hist = jnp.zeros((batch_p, POOL_HIST, POOL_WIDTH), F32)
        zero_state = jnp.zeros((batch_p, RET_HEADS, RET_DK, RET_DV), F32)
        routed_p, hist_p, s_p = _pre_moe(hp, zero_hist, zero_state, 0, 0, mk, mv, w)
        mem_k_p.append(mk.reshape(batch_p, m_len, MEM_HEADS, d_model // MEM_HEADS))
        mem_v_p.append(mv.reshape(batch_p, m_len, MEM_HEADS, d_model // MEM_HEADS))
        pool_p.append(hist_p)
        ret_p.append(s_p)
        ck = cache_mem_k[l].reshape(cache_mem_k.shape[1], m_len, d_model)
        cv = cache_mem_v[l].reshape(cache_mem_v.shape[1], m_len, d_model)
        routed_s, hist_s, s_s = _pre_moe(hs, state_pool[l], state_ret[l], PAST_LEN, POOL_HIST, ck, cv, w)
        pool_s.append(hist_s)
        ret_s.append(s_s)
        hp, hs = _moe([routed_p, routed_s], w, norm_w, last)
        hp, hs = hp.reshape(x_prompt.shape), hs.reshape(x_sample.shape)
    return (hp, hs, jnp.stack(mem_k_p), jnp.stack(mem_v_p), jnp.stack(pool_p), jnp.stack(ret_p),
            jnp.stack(pool_s), jnp.stack(ret_s))
```
